```python
import jax, jax.numpy as jnp
from jax import lax
import numpy as np

D_MODEL = 1024
BATCH = 2
SEQ = 8192
DEPTH = 2

N_MIXERS = 2
NORM_EPS = 1e-5
DILATED_GROUPS = ((128, 1), (512, 4), (2048, 16))
N_GROUPS = len(DILATED_GROUPS)
ATTN_HEADS = 8
HEAD_DIM = D_MODEL // ATTN_HEADS
ATTN_WIDTH = ATTN_HEADS * HEAD_DIM
ATTN_IN = N_GROUPS * 3 * ATTN_WIDTH
ATTN_BLOCK = 128
ROPE_THETA = 10000.0
GLA_HEADS = 4
GLA_KEY_DIM = D_MODEL // 2
GLA_VAL_DIM = D_MODEL
GLA_DK = GLA_KEY_DIM // GLA_HEADS
GLA_DV = GLA_VAL_DIM // GLA_HEADS
GLA_GATE_RANK = 16
GLA_GATE_NORMALIZER = 16.0
GLA_CHUNK = 64
GLA_IN = 2 * GLA_KEY_DIM + 2 * GLA_VAL_DIM + GLA_GATE_RANK
D_FF = 4 * D_MODEL

kernel_name = "hybrid_dilated_attn_gla_sqrelu"


def rmsnorm(x, w):
    xf = x.astype(jnp.float32)
    y = xf * lax.rsqrt(jnp.mean(xf * xf, axis=-1, keepdims=True) + NORM_EPS)
    return (y * w.astype(jnp.float32)).astype(x.dtype)


def rope_tables(seq_len, dim):
    pos = jnp.arange(seq_len, dtype=jnp.float32)
    inv_freq = ROPE_THETA ** (-jnp.arange(0, dim, 2, dtype=jnp.float32) / dim)
    ang = pos[:, None] * inv_freq[None, :]
    return jnp.cos(ang), jnp.sin(ang)


def apply_rope(t, cos, sin):
    tf = t.astype(jnp.float32)
    half = tf.shape[-1] // 2
    t1, t2 = tf[..., :half], tf[..., half:]
    c = cos[None, :, None, :]
    s = sin[None, :, None, :]
    return jnp.concatenate([t1 * c - t2 * s, t2 * c + t1 * s], axis=-1)


def dilated_group_attention(q, k, v, dil, n_back):
    B, S, H, D = q.shape
    L = S // dil

    def to_res(t):
        return t.reshape(B, L, dil, H, D).transpose(0, 2, 3, 1, 4)

    C = min(ATTN_BLOCK, L)
    nb = -(-L // C)
    Lp = nb * C
    K = n_back + C
    qr = jnp.pad(to_res(q), ((0, 0), (0, 0), (0, 0), (0, Lp - L), (0, 0)))
    qr = qr.reshape(B, dil, H, nb, C, D)
    kv_pad = ((0, 0), (0, 0), (0, 0), (n_back, Lp - L), (0, 0))
    kr = jnp.pad(to_res(k), kv_pad)
    vr = jnp.pad(to_res(v), kv_pad)
    idx = jnp.arange(nb)[:, None] * C + jnp.arange(K)[None, :]
    kb = kr[:, :, :, idx, :]
    vb = vr[:, :, :, idx, :]
    s = jnp.einsum('bzhnqd,bzhnkd->bzhnqk', qr, kb)
    qi = jnp.arange(C)[:, None]
    kj = jnp.arange(K)[None, :]
    rel = qi - kj + n_back
    key_idx = jnp.arange(nb)[:, None, None] * C + kj[None] - n_back
    mask = (rel >= 0) & (rel <= n_back) & (key_idx >= 0)
    s = jnp.where(mask, s, -jnp.inf)
    m = jnp.max(s, axis=-1, keepdims=True)
    p = jnp.exp(s - m)
    l = jnp.sum(p, axis=-1)
    o = jnp.einsum('bzhnqk,bzhnkd->bzhnqd', p, vb) / l[..., None]
    lse = m[..., 0] + jnp.log(l)
    o = o.reshape(B, dil, H, Lp, D)[:, :, :, :L]
    o = o.transpose(0, 3, 1, 2, 4).reshape(B, S, H, D)
    lse = lse.reshape(B, dil, H, Lp)[:, :, :, :L]
    lse = lse.transpose(0, 3, 1, 2).reshape(B, S, H)
    return o, lse


def dilated_attention(h, w_in, w_out):
    B, S, _ = h.shape
    qkv = (h @ w_in).reshape(B, S, N_GROUPS, 3, ATTN_HEADS, HEAD_DIM)
    cos, sin = rope_tables(S, HEAD_DIM)
    outs, lses = [], []
    for g, (window, dil) in enumerate(DILATED_GROUPS):
        q = apply_rope(qkv[:, :, g, 0], cos, sin) * (HEAD_DIM ** -0.5)
        k = apply_rope(qkv[:, :, g, 1], cos, sin)
        v = qkv[:, :, g, 2].astype(jnp.float32)
        o_g, lse_g = dilated_group_attention(q, k, v, dil, window // dil)
        outs.append(o_g)
        lses.append(lse_g)
    alpha = jax.nn.softmax(jnp.stack(lses, axis=0), axis=0)
    o = jnp.sum(alpha[..., None] * jnp.stack(outs, axis=0), axis=0)
    return o.reshape(B, S, ATTN_WIDTH).astype(h.dtype) @ w_out


def gla_mixer(h, w_in, w_gate_up, b_gate, norm_w, w_out):
    B, S, _ = h.shape
    nc = S // GLA_CHUNK
    proj = h @ w_in
    s1 = GLA_KEY_DIM
    s2 = 2 * GLA_KEY_DIM
    s3 = s2 + GLA_VAL_DIM
    s4 = s3 + GLA_VAL_DIM
    q, k, v = proj[..., :s1], proj[..., s1:s2], proj[..., s2:s3]
    g_out, gate_lr = proj[..., s3:s4], proj[..., s4:]
    gk = gate_lr @ w_gate_up + b_gate
    log_a = jax.nn.log_sigmoid(gk.astype(jnp.float32)) / GLA_GATE_NORMALIZER

    def chunked(t, d):
        t = t.astype(jnp.float32).reshape(B, nc, GLA_CHUNK, GLA_HEADS, d)
        return t.transpose(0, 3, 1, 2, 4)

    qc = chunked(q, GLA_DK) * (GLA_DK ** -0.5)
    kc = chunked(k, GLA_DK)
    vc = chunked(v, GLA_DV)
    b = jnp.cumsum(chunked(log_a, GLA_DK), axis=-2)
    b_last = b[..., -1:, :]
    q_t = qc * jnp.exp(b)
    k_t = kc * jnp.exp(-b)
    causal = jnp.tril(jnp.ones((GLA_CHUNK, GLA_CHUNK), dtype=bool))
    A = jnp.where(causal, jnp.einsum('bhncd,bhnsd->bhncs', q_t, k_t), 0.0)
    o_intra = jnp.einsum('bhncs,bhnsv->bhncv', A, vc)
    U = jnp.einsum('bhncd,bhncv->bhndv', kc * jnp.exp(b_last - b), vc)
    decay = jnp.exp(b_last[..., 0, :])

    def step(state, inp):
        u, a = inp
        return a[..., None] * state + u, state

    init = jnp.zeros((B, GLA_HEADS, GLA_DK, GLA_DV), jnp.float32)
    _, s_prev = lax.scan(step, init, (jnp.moveaxis(U, 2, 0), jnp.moveaxis(decay, 2, 0)))
    s_prev = jnp.moveaxis(s_prev, 0, 2)
    o = o_intra + jnp.einsum('bhncd,bhndv->bhncv', q_t, s_prev)
    o = o.transpose(0, 2, 3, 1, 4).reshape(B, S, GLA_HEADS, GLA_DV)
    o = rmsnorm(o, norm_w)
    gate = jax.nn.silu(g_out.astype(jnp.float32)).reshape(B, S, GLA_HEADS, GLA_DV)
    o = (o * gate).reshape(B, S, GLA_VAL_DIM).astype(h.dtype)
    return o @ w_out


def sq_relu_mlp(h, w_up, w_down):
    a = jax.nn.relu(h @ w_up)
    return (a * a) @ w_down


def setup_inputs(seed: int = 0) -> dict:
    key = jax.random.key(seed)
    ks = jax.random.split(key, 16)
    n_attn = (DEPTH + 1) // N_MIXERS
    n_gla = DEPTH // N_MIXERS

    def w(k, shape, fan_in):
        return jax.random.normal(k, shape, jnp.float32) * (fan_in ** -0.5)

    def gain(k, shape):
        return 1.0 + 0.02 * jax.random.normal(k, shape, jnp.float32)

    return {
        "x": jax.random.normal(ks[0], (BATCH, SEQ, D_MODEL), jnp.float32),
        "norm_mix_w": gain(ks[1], (DEPTH, D_MODEL)),
        "norm_mlp_w": gain(ks[2], (DEPTH, D_MODEL)),
        "final_norm_w": gain(ks[3], (D_MODEL,)),
        "attn_w_in": w(ks[4], (n_attn, D_MODEL, ATTN_IN), D_MODEL),
        "attn_w_out": w(ks[5], (n_attn, ATTN_WIDTH, D_MODEL), ATTN_WIDTH),
        "gla_w_in": w(ks[6], (n_gla, D_MODEL, GLA_IN), D_MODEL),
        "gla_w_gate_up": w(ks[7], (n_gla, GLA_GATE_RANK, GLA_KEY_DIM), GLA_GATE_RANK),
        "gla_b_gate": 0.02 * jax.random.normal(ks[8], (n_gla, GLA_KEY_DIM), jnp.float32),
        "gla_norm_w": gain(ks[9], (n_gla, GLA_DV)),
        "gla_w_out": w(ks[10], (n_gla, GLA_VAL_DIM, D_MODEL), GLA_VAL_DIM),
        "mlp_w_up": w(ks[11], (DEPTH, D_MODEL, D_FF), D_MODEL),
        "mlp_w_down": w(ks[12], (DEPTH, D_FF, D_MODEL), D_FF),
    }


def reference(x, norm_mix_w, norm_mlp_w, final_norm_w, attn_w_in, attn_w_out,
              gla_w_in, gla_w_gate_up, gla_b_gate, gla_norm_w, gla_w_out,
              mlp_w_up, mlp_w_down):
    for i in range(DEPTH):
        h = rmsnorm(x, norm_mix_w[i])
        j = i // N_MIXERS
        if i % N_MIXERS == 0:
            x = x + dilated_attention(h, attn_w_in[j], attn_w_out[j])
        else:
            x = x + gla_mixer(h, gla_w_in[j], gla_w_gate_up[j], gla_b_gate[j],
                              gla_norm_w[j], gla_w_out[j])
        h = rmsnorm(x, norm_mlp_w[i])
        x = x + sq_relu_mlp(h, mlp_w_up[i], mlp_w_down[i])
    return rmsnorm(x, final_norm_w)
```

```python
from functools import partial

import jax
import jax.numpy as jnp
from jax import lax
from jax.experimental import pallas as pl
from jax.experimental.pallas import tpu as pltpu

D_MODEL = 1024
NORM_EPS = 1e-5
DILATED_GROUPS = ((128, 1), (512, 4), (2048, 16))
ATTN_HEADS = 8
HEAD_DIM = 128
ATTN_WIDTH = ATTN_HEADS * HEAD_DIM
ATTN_BLOCK = 128
ROPE_THETA = 10000.0
GLA_HEADS = 4
GLA_DK = 128
GLA_DV = 256
GLA_KEY_DIM = GLA_HEADS * GLA_DK
GLA_VAL_DIM = GLA_HEADS * GLA_DV
GLA_GATE_RANK = 16
GLA_GATE_NORMALIZER = 16.0
GLA_CHUNK = 64
D_FF = 4 * D_MODEL

LANES = 128
VMEM_LIMIT_BYTES = 56 * 1024 * 1024
NEG_BIG = -1e30

BF16 = jnp.bfloat16
F32 = jnp.float32


def _params(*semantics):
    return pltpu.CompilerParams(dimension_semantics=semantics,
                                vmem_limit_bytes=VMEM_LIMIT_BYTES)


def _rmsnorm_rows(x, w):
    var = jnp.mean(x * x, axis=-1, keepdims=True)
    return x * lax.rsqrt(var + NORM_EPS) * w


def _dot(a, b):
    return jnp.dot(a, b, preferred_element_type=F32)


def _dot_nt(a, b):
    return lax.dot_general(a, b, (((1,), (1,)), ((), ())), preferred_element_type=F32)


def _dot_tn(a, b):
    return lax.dot_general(a, b, (((0,), (0,)), ((), ())), preferred_element_type=F32)


QKV_CHUNK = 256


def _qkv_kernel(x_ref, nw_ref, w_ref, cos_ref, sin_ref, q_ref, k_ref, v_ref):
    h = _rmsnorm_rows(x_ref[...], nw_ref[...]).astype(BF16)
    cos = cos_ref[...]
    sin = sin_ref[...]
    outs = (q_ref, k_ref, v_ref)
    n_chunks = ATTN_WIDTH // QKV_CHUNK
    for part in range(3):
        for j in range(n_chunks):
            col = part * ATTN_WIDTH + j * QKV_CHUNK
            acc = _dot(h, w_ref[:, col:col + QKV_CHUNK])
            for half in range(QKV_CHUNK // HEAD_DIM):
                t = acc[:, half * HEAD_DIM:(half + 1) * HEAD_DIM]
                if part < 2:
                    t = t * cos + pltpu.roll(t, HEAD_DIM // 2, 1) * sin
                if part == 0:
                    t = t * (HEAD_DIM ** -0.5)
                lo = j * QKV_CHUNK + half * HEAD_DIM
                outs[part][:, lo:lo + HEAD_DIM] = t.astype(BF16)


def _qkv_project(x, nw, w_g, cos2, sin2, dil, rows):
    B, S, D = x.shape
    L = S // dil
    rows = min(rows, L)
    xr = x.reshape(B, L, dil * D)
    cr = cos2.reshape(L, dil * HEAD_DIM)
    sr = sin2.reshape(L, dil * HEAD_DIM)
    out = jax.ShapeDtypeStruct((B, dil, L, ATTN_WIDTH), BF16)
    out_spec = pl.BlockSpec((None, None, rows, ATTN_WIDTH), lambda b, r, i: (b, r, i, 0))
    tab_spec = pl.BlockSpec((rows, HEAD_DIM), lambda b, r, i: (i, r))
    return pl.pallas_call(
        _qkv_kernel,
        grid=(B, dil, L // rows),
        in_specs=[
            pl.BlockSpec((None, rows, D), lambda b, r, i: (b, i, r)),
            pl.BlockSpec((1, D), lambda b, r, i: (0, 0)),
            pl.BlockSpec((D, 3 * ATTN_WIDTH), lambda b, r, i: (0, 0),
                         pipeline_mode=pl.Buffered(1)),
            tab_spec, tab_spec,
        ],
        out_specs=[out_spec, out_spec, out_spec],
        out_shape=[out, out, out],
        compiler_params=_params("parallel", "parallel", "parallel"),
        name="attn_qkv",
    )(xr, nw, w_g, cr, sr)


def _attn_kernel(q_ref, k_ref, v_ref, kh_ref, vh_ref, o_ref, lse_ref, *, n_sub):
    i = pl.program_id(2)
    qi = lax.broadcasted_iota(jnp.int32, (ATTN_BLOCK, 2 * ATTN_BLOCK), 0)
    kj = lax.broadcasted_iota(jnp.int32, (ATTN_BLOCK, 2 * ATTN_BLOCK), 1)
    rel = qi - kj + ATTN_BLOCK
    band = (rel >= 0) & (rel <= ATTN_BLOCK)
    band_first = band & (kj >= jnp.where(i > 0, 0, ATTN_BLOCK))
    lane = lax.broadcasted_iota(jnp.int32, (ATTN_BLOCK, LANES), 1)
    for c in range(n_sub):
        rows = slice(c * ATTN_BLOCK, (c + 1) * ATTN_BLOCK)
        mask = band_first if c == 0 else band
        lse_tile = jnp.zeros((ATTN_BLOCK, LANES), F32)
        for h in range(ATTN_HEADS):
            cols = slice(h * HEAD_DIM, (h + 1) * HEAD_DIM)
            q = q_ref[rows, cols]
            if c == 0:
                kk = jnp.concatenate([kh_ref[:, cols], k_ref[rows, cols]], axis=0)
                vv = jnp.concatenate([vh_ref[:, cols], v_ref[rows, cols]], axis=0)
            else:
                hist = slice((c - 1) * ATTN_BLOCK, (c + 1) * ATTN_BLOCK)
                kk = k_ref[hist, cols]
                vv = v_ref[hist, cols]
            s = jnp.where(mask, _dot_nt(q, kk), NEG_BIG)
            m = jnp.max(s, axis=-1, keepdims=True)
            p = jnp.exp(s - m)
            l = jnp.sum(p, axis=-1, keepdims=True)
            o = _dot(p.astype(BF16), vv) / l
            o_ref[rows, cols] = o
            lse_tile = jnp.where(lane == h, m + jnp.log(l), lse_tile)
        lse_ref[rows, :] = lse_tile


def _group_attention(q, k, v, rows):
    B, dil, L, W = q.shape
    S = L * dil
    rows = min(rows, L)
    n_sub = rows // ATTN_BLOCK
    blk = pl.BlockSpec((None, None, rows, W), lambda b, r, i: (b, r, i, 0))
    halo = pl.BlockSpec((None, None, ATTN_BLOCK, W),
                        lambda b, r, i: (b, r, jnp.maximum(i * n_sub - 1, 0), 0))
    o, lse = pl.pallas_call(
        partial(_attn_kernel, n_sub=n_sub),
        grid=(B, dil, L // rows),
        in_specs=[blk, blk, blk, halo, halo],
        out_specs=[
            pl.BlockSpec((None, rows, W), lambda b, r, i: (b, i, r)),
            pl.BlockSpec((None, rows, LANES), lambda b, r, i: (b, i, r)),
        ],
        out_shape=[
            jax.ShapeDtypeStruct((B, L, dil * W), F32),
            jax.ShapeDtypeStruct((B, L, dil * LANES), F32),
        ],
        compiler_params=_params("parallel", "parallel", "parallel"),
        name="attn_band",
    )(q, k, v, k, v)
    return o.reshape(B, S, W), lse.reshape(B, S, LANES)


def _merge_proj_kernel(o1_ref, o2_ref, o3_ref, l1_ref, l2_ref, l3_ref, x_ref, w_ref, nw_ref,
                       xo_ref, h_ref, merged_ref):
    l1, l2, l3 = l1_ref[...], l2_ref[...], l3_ref[...]
    m = jnp.maximum(jnp.maximum(l1, l2), l3)
    e1, e2, e3 = jnp.exp(l1 - m), jnp.exp(l2 - m), jnp.exp(l3 - m)
    tot = e1 + e2 + e3
    a1, a2, a3 = e1 / tot, e2 / tot, e3 / tot
    for h in range(ATTN_HEADS):
        cols = slice(h * HEAD_DIM, (h + 1) * HEAD_DIM)
        o = (a1[:, h:h + 1] * o1_ref[:, cols] + a2[:, h:h + 1] * o2_ref[:, cols]
             + a3[:, h:h + 1] * o3_ref[:, cols])
        merged_ref[:, cols] = o.astype(BF16)
    xo = x_ref[...] + _dot(merged_ref[...], w_ref[...])
    xo_ref[...] = xo
    h_ref[...] = _rmsnorm_rows(xo, nw_ref[...]).astype(BF16)


def _merge_project(o_list, lse_list, x, w_out, nw, rows):
    B, S, D = x.shape
    T = B * S
    W = ATTN_WIDTH
    row_spec = lambda width: pl.BlockSpec((rows, width), lambda i: (i, 0))
    const = lambda shape: pl.BlockSpec(shape, lambda i: (0, 0))
    xo, h = pl.pallas_call(
        _merge_proj_kernel,
        grid=(T // rows,),
        in_specs=[row_spec(W)] * 3 + [row_spec(LANES)] * 3
        + [row_spec(D), const((W, D)), const((1, D))],
        out_specs=[row_spec(D), row_spec(D)],
        out_shape=[jax.ShapeDtypeStruct((T, D), F32), jax.ShapeDtypeStruct((T, D), BF16)],
        scratch_shapes=[pltpu.VMEM((rows, W), BF16)],
        compiler_params=_params("parallel"),
        name="attn_merge_proj",
    )(*[o.reshape(T, W) for o in o_list], *[l.reshape(T, LANES) for l in lse_list],
      x.reshape(T, D), w_out, nw)
    return xo.reshape(B, S, D), h.reshape(B, S, D)


def _proj_kernel(o_ref, x_ref, w_ref, nw_ref, xo_ref, h_ref):
    xo = x_ref[...] + _dot(o_ref[...], w_ref[...])
    xo_ref[...] = xo
    h_ref[...] = _rmsnorm_rows(xo, nw_ref[...]).astype(BF16)


def _project(o, x, w_out, nw, rows):
    B, S, D = x.shape
    T = B * S
    W = o.shape[-1]
    row_spec = lambda width: pl.BlockSpec((rows, width), lambda i: (i, 0))
    const = lambda shape: pl.BlockSpec(shape, lambda i: (0, 0))
    xo, h = pl.pallas_call(
        _proj_kernel,
        grid=(T // rows,),
        in_specs=[row_spec(W), row_spec(D), const((W, D)), const((1, D))],
        out_specs=[row_spec(D), row_spec(D)],
        out_shape=[jax.ShapeDtypeStruct((T, D), F32), jax.ShapeDtypeStruct((T, D), BF16)],
        compiler_params=_params("parallel"),
        name="out_proj",
    )(o.reshape(T, W), x.reshape(T, D), w_out, nw)
    return xo.reshape(B, S, D), h.reshape(B, S, D)


MLP_FF_CHUNK = 1024


def _mlp_kernel(h_ref, x_ref, wu_ref, wd_ref, nw_ref, o_ref, *, final_norm):
    h = h_ref[...]
    acc = x_ref[...]
    for f in range(D_FF // MLP_FF_CHUNK):
        cols = slice(f * MLP_FF_CHUNK, (f + 1) * MLP_FF_CHUNK)
        a = jnp.maximum(_dot(h, wu_ref[:, cols]), 0.0)
        acc = acc + _dot((a * a).astype(BF16), wd_ref[cols, :])
    if final_norm:
        acc = _rmsnorm_rows(acc, nw_ref[...])
    o_ref[...] = acc


def _mlp(h, x, w_up, w_down, nw, rows, final_norm):
    B, S, D = x.shape
    T = B * S
    row_spec = pl.BlockSpec((rows, D), lambda i: (i, 0))
    const = lambda shape: pl.BlockSpec(shape, lambda i: (0, 0), pipeline_mode=pl.Buffered(1))
    out = pl.pallas_call(
        partial(_mlp_kernel, final_norm=final_norm),
        grid=(T // rows,),
        in_specs=[row_spec, row_spec, const((D, D_FF)), const((D_FF, D)), const((1, D))],
        out_specs=row_spec,
        out_shape=jax.ShapeDtypeStruct((T, D), F32),
        compiler_params=_params("parallel"),
        name="mlp",
    )(h.reshape(T, D), x.reshape(T, D), w_up, w_down, nw)
    return out.reshape(B, S, D)


def _gla_in_kernel(x_ref, nw_ref, w_ref, wg_ref, wgu_ref, bg_ref,
                   q_ref, k_ref, v_ref, g_ref, la_ref):
    h = _rmsnorm_rows(x_ref[...], nw_ref[...]).astype(BF16)
    q_ref[...] = _dot(h, w_ref[:, :GLA_KEY_DIM])
    k_ref[...] = _dot(h, w_ref[:, GLA_KEY_DIM:2 * GLA_KEY_DIM])
    v_lo = 2 * GLA_KEY_DIM
    v_ref[...] = _dot(h, w_ref[:, v_lo:v_lo + GLA_VAL_DIM]).astype(BF16)
    g_lo = v_lo + GLA_VAL_DIM
    g_ref[...] = _dot(h, w_ref[:, g_lo:g_lo + GLA_VAL_DIM])
    gate_lr = _dot(h, wg_ref[...]).astype(BF16)
    gk = _dot(gate_lr, wgu_ref[...]) + bg_ref[...]
    log_sig = jnp.minimum(gk, 0.0) - jnp.log(1.0 + jnp.exp(-jnp.abs(gk)))
    la_ref[...] = log_sig * (1.0 / GLA_GATE_NORMALIZER)


def _gla_in_project(x, nw, w_main, w_gate, w_gate_up, b_gate, rows):
    B, S, D = x.shape
    T = B * S
    row_spec = lambda width: pl.BlockSpec((rows, width), lambda i: (i, 0))
    const = lambda shape: pl.BlockSpec(shape, lambda i: (0, 0))
    n_main = 2 * GLA_KEY_DIM + 2 * GLA_VAL_DIM
    return pl.pallas_call(
        _gla_in_kernel,
        grid=(T // rows,),
        in_specs=[row_spec(D), const((1, D)), const((D, n_main)), const((D, LANES)),
                  const((LANES, GLA_KEY_DIM)), const((1, GLA_KEY_DIM))],
        out_specs=[row_spec(GLA_KEY_DIM), row_spec(GLA_KEY_DIM), row_spec(GLA_VAL_DIM),
                   row_spec(GLA_VAL_DIM), row_spec(GLA_KEY_DIM)],
        out_shape=[
            jax.ShapeDtypeStruct((T, GLA_KEY_DIM), F32),
            jax.ShapeDtypeStruct((T, GLA_KEY_DIM), F32),
            jax.ShapeDtypeStruct((T, GLA_VAL_DIM), BF16),
            jax.ShapeDtypeStruct((T, GLA_VAL_DIM), F32),
            jax.ShapeDtypeStruct((T, GLA_KEY_DIM), F32),
        ],
        compiler_params=_params("parallel"),
        name="gla_in",
    )(x.reshape(T, D), nw, w_main, w_gate, w_gate_up, b_gate)


def _gla_kernel(q_ref, k_ref, v_ref, g_ref, la_ref, nw_ref, o_ref, state_ref, *, n_chunks):
    @pl.when(pl.program_id(1) == 0)
    def _():
        state_ref[...] = jnp.zeros_like(state_ref)

    ci = lax.broadcasted_iota(jnp.int32, (GLA_CHUNK, GLA_CHUNK), 0)
    cj = lax.broadcasted_iota(jnp.int32, (GLA_CHUNK, GLA_CHUNK), 1)
    causal = ci >= cj
    tril = causal.astype(BF16)
    ones = jnp.ones((GLA_CHUNK, LANES), BF16)
    nw = nw_ref[...]
    for c in range(n_chunks):
        rows = slice(c * GLA_CHUNK, (c + 1) * GLA_CHUNK)
        la = la_ref[rows, :]
        la_hi = la.astype(BF16)
        la_lo = (la - la_hi.astype(F32)).astype(BF16)
        b = _dot(tril, la_hi) + _dot(tril, la_lo)
        b_last = b[GLA_CHUNK - 1:GLA_CHUNK, :]
        eb = jnp.exp(b)
        enb = jnp.exp(-b)
        eu = jnp.exp(b_last - b)
        q_t = (q_ref[rows, :] * (GLA_DK ** -0.5) * eb).astype(BF16)
        k_all = k_ref[rows, :]
        k_t = (k_all * enb).astype(BF16)
        k_u = (k_all * eu).astype(BF16)
        for h in range(GLA_HEADS):
            kc = slice(h * GLA_DK, (h + 1) * GLA_DK)
            vc = slice(h * GLA_DV, (h + 1) * GLA_DV)
            v = v_ref[rows, vc]
            a = jnp.where(causal, _dot_nt(q_t[:, kc], k_t[:, kc]), 0.0).astype(BF16)
            state = state_ref[h]
            o = _dot(a, v) + _dot(q_t[:, kc], state.astype(BF16))
            bl_t = _dot_tn(la_hi[:, kc], ones) + _dot_tn(la_lo[:, kc], ones)
            decay = jnp.exp(bl_t)
            decay = jnp.concatenate([decay] * (GLA_DV // LANES), axis=1)
            state_ref[h] = decay * state + _dot_tn(k_u[:, kc], v)
            var = jnp.mean(o * o, axis=-1, keepdims=True)
            o = o * lax.rsqrt(var + NORM_EPS) * nw
            g = g_ref[rows, vc]
            o_ref[rows, vc] = (o * (g / (1.0 + jnp.exp(-g)))).astype(BF16)


def _gla_recurrence(q, k, v, g, la, nw, B, S, rows):
    T = B * S
    n_blocks = S // rows
    row_spec = lambda width: pl.BlockSpec((rows, width), lambda b, i: (b * n_blocks + i, 0))
    return pl.pallas_call(
        partial(_gla_kernel, n_chunks=rows // GLA_CHUNK),
        grid=(B, n_blocks),
        in_specs=[row_spec(GLA_KEY_DIM), row_spec(GLA_KEY_DIM), row_spec(GLA_VAL_DIM),
                  row_spec(GLA_VAL_DIM), row_spec(GLA_KEY_DIM),
                  pl.BlockSpec((1, GLA_DV), lambda b, i: (0, 0))],
        out_specs=row_spec(GLA_VAL_DIM),
        out_shape=jax.ShapeDtypeStruct((T, GLA_VAL_DIM), BF16),
        scratch_shapes=[pltpu.VMEM((GLA_HEADS, GLA_DK, GLA_DV), F32)],
        compiler_params=_params("parallel", "arbitrary"),
        name="gla_recurrence",
    )(q, k, v, g, la, nw)


def _rope_tables(seq_len):
    pos = jnp.arange(seq_len, dtype=F32)
    inv_freq = ROPE_THETA ** (-jnp.arange(0, HEAD_DIM, 2, dtype=F32) / HEAD_DIM)
    ang = pos[:, None] * inv_freq[None, :]
    cos, sin = jnp.cos(ang), jnp.sin(ang)
    return jnp.concatenate([cos, cos], axis=-1), jnp.concatenate([-sin, sin], axis=-1)


def kernel(x, norm_mix_w, norm_mlp_w, final_norm_w, attn_w_in, attn_w_out, gla_w_in,
           gla_w_gate_up, gla_b_gate, gla_norm_w, gla_w_out, mlp_w_up, mlp_w_down):
    B, S, D = x.shape
    row = lambda w: w.reshape(1, -1).astype(F32)

    cos2, sin2 = _rope_tables(S)
    w_in = attn_w_in[0].astype(BF16)
    group_w = 3 * ATTN_WIDTH
    outs, lses = [], []
    for g, (window, dil) in enumerate(DILATED_GROUPS):
        assert window // dil == ATTN_BLOCK and (S // dil) % ATTN_BLOCK == 0
        q, k, v = _qkv_project(x, row(norm_mix_w[0]), w_in[:, g * group_w:(g + 1) * group_w],
                               cos2, sin2, dil, rows=512)
        o, lse = _group_attention(q, k, v, rows=512)
        outs.append(o)
        lses.append(lse)
    x, h = _merge_project(outs, lses, x, attn_w_out[0].astype(BF16), row(norm_mlp_w[0]), rows=512)
    x = _mlp(h, x, mlp_w_up[0].astype(BF16), mlp_w_down[0].astype(BF16), row(final_norm_w),
             rows=512, final_norm=False)

    n_main = 2 * GLA_KEY_DIM + 2 * GLA_VAL_DIM
    w_gla = gla_w_in[0]
    w_gate = jnp.pad(w_gla[:, n_main:], ((0, 0), (0, LANES - GLA_GATE_RANK))).astype(BF16)
    w_gate_up = jnp.pad(gla_w_gate_up[0], ((0, LANES - GLA_GATE_RANK), (0, 0))).astype(BF16)
    q, k, v, g, la = _gla_in_project(x, row(norm_mix_w[1]), w_gla[:, :n_main].astype(BF16),
                                     w_gate, w_gate_up, row(gla_b_gate[0]), rows=512)
    o = _gla_recurrence(q, k, v, g, la, row(gla_norm_w[0]), B, S, rows=512)
    x, h = _project(o.reshape(B, S, GLA_VAL_DIM), x, gla_w_out[0].astype(BF16),
                    row(norm_mlp_w[1]), rows=512)
    x = _mlp(h, x, mlp_w_up[1].astype(BF16), mlp_w_down[1].astype(BF16), row(final_norm_w),
             rows=512, final_norm=True)
    return x
```

```python
from functools import partial

import jax
import jax.numpy as jnp
from jax import lax
from jax.experimental import pallas as pl
from jax.experimental.pallas import tpu as pltpu

D_MODEL = 1024
NORM_EPS = 1e-5
DILATED_GROUPS = ((128, 1), (512, 4), (2048, 16))
ATTN_HEADS = 8
HEAD_DIM = 128
ATTN_WIDTH = ATTN_HEADS * HEAD_DIM
ATTN_BLOCK = 128
ROPE_THETA = 10000.0
GLA_HEADS = 4
GLA_DK = 128
GLA_DV = 256
GLA_KEY_DIM = GLA_HEADS * GLA_DK
GLA_VAL_DIM = GLA_HEADS * GLA_DV
GLA_GATE_RANK = 16
GLA_GATE_NORMALIZER = 16.0
GLA_CHUNK = 64
D_FF = 4 * D_MODEL

LANES = 128
VMEM_LIMIT_BYTES = 56 * 1024 * 1024
NEG_BIG = -1e30

BF16 = jnp.bfloat16
F32 = jnp.float32


def _params(*semantics):
    return pltpu.CompilerParams(dimension_semantics=semantics,
                                vmem_limit_bytes=VMEM_LIMIT_BYTES)


def _rmsnorm_rows(x, w):
    var = jnp.mean(x * x, axis=-1, keepdims=True)
    return x * lax.rsqrt(var + NORM_EPS) * w


def _dot(a, b):
    return jnp.dot(a, b, preferred_element_type=F32)


def _dot_nt(a, b):
    return lax.dot_general(a, b, (((1,), (1,)), ((), ())), preferred_element_type=F32)


def _dot_tn(a, b):
    return lax.dot_general(a, b, (((0,), (0,)), ((), ())), preferred_element_type=F32)


QKV_CHUNK = 256


def _qkv_kernel(x_ref, nw_ref, w_ref, cos_ref, sin_ref, *refs):
    out_refs, perm_ref = refs[:-1], refs[-1]
    rows = x_ref.shape[0]
    h = _rmsnorm_rows(x_ref[...], nw_ref[...]).astype(BF16)
    cos = cos_ref[...]
    sin = sin_ref[...]
    n_chunks = ATTN_WIDTH // QKV_CHUNK
    for g, (_, dil) in enumerate(DILATED_GROUPS):
        for part in range(3):
            o_ref = out_refs[3 * g + part]
            for j in range(n_chunks):
                col = (3 * g + part) * ATTN_WIDTH + j * QKV_CHUNK
                acc = _dot(h, w_ref[:, col:col + QKV_CHUNK])
                for half in range(QKV_CHUNK // HEAD_DIM):
                    t = acc[:, half * HEAD_DIM:(half + 1) * HEAD_DIM]
                    if part < 2:
                        t = t * cos + pltpu.roll(t, HEAD_DIM // 2, 1) * sin
                    if part == 0:
                        t = t * (HEAD_DIM ** -0.5)
                    cols = slice(j * QKV_CHUNK + half * HEAD_DIM,
                                 j * QKV_CHUNK + (half + 1) * HEAD_DIM)
                    if dil == 1:
                        o_ref[0, :, cols] = t.astype(BF16)
                        continue
                    perm_ref[half] = t
                    for r in range(dil):
                        o_ref[r, :, cols] = perm_ref[
                            half, pl.ds(r, rows // dil, stride=dil), :].astype(BF16)


def _qkv_project(x, nw, w_in, cos2, sin2, rows):
    B, S, D = x.shape
    n_blocks = S // rows
    out_shapes, out_specs = [], []
    for _, dil in DILATED_GROUPS:
        for _ in range(3):
            out_shapes.append(jax.ShapeDtypeStruct((B, dil, S // dil, ATTN_WIDTH), BF16))
            out_specs.append(pl.BlockSpec((None, dil, rows // dil, ATTN_WIDTH),
                                          lambda i: (i // n_blocks, 0, i % n_blocks, 0)))
    tab_spec = pl.BlockSpec((rows, HEAD_DIM), lambda i: (i % n_blocks, 0))
    const = lambda shape: pl.BlockSpec(shape, lambda i: (0, 0), pipeline_mode=pl.Buffered(1))
    return pl.pallas_call(
        _qkv_kernel,
        grid=(B * n_blocks,),
        in_specs=[
            pl.BlockSpec((rows, D), lambda i: (i, 0)),
            const((1, D)),
            const((D, w_in.shape[1])),
            tab_spec, tab_spec,
        ],
        out_specs=out_specs,
        out_shape=out_shapes,
        scratch_shapes=[pltpu.VMEM((QKV_CHUNK // HEAD_DIM, rows, LANES), F32)],
        compiler_params=_params("parallel"),
        name="attn_qkv",
    )(x.reshape(B * S, D), nw, w_in, cos2, sin2)


def _attn_kernel(q_ref, k_ref, v_ref, kh_ref, vh_ref, o_ref, lse_ref, *, n_sub):
    i = pl.program_id(2)
    qi = lax.broadcasted_iota(jnp.int32, (ATTN_BLOCK, 2 * ATTN_BLOCK), 0)
    kj = lax.broadcasted_iota(jnp.int32, (ATTN_BLOCK, 2 * ATTN_BLOCK), 1)
    rel = qi - kj + ATTN_BLOCK
    band = (rel >= 0) & (rel <= ATTN_BLOCK)
    band_first = band & (kj >= jnp.where(i > 0, 0, ATTN_BLOCK))
    lane = lax.broadcasted_iota(jnp.int32, (ATTN_BLOCK, LANES), 1)
    for c in range(n_sub):
        rows = slice(c * ATTN_BLOCK, (c + 1) * ATTN_BLOCK)
        mask = band_first if c == 0 else band
        lse_tile = jnp.zeros((ATTN_BLOCK, LANES), F32)
        for h in range(ATTN_HEADS):
            cols = slice(h * HEAD_DIM, (h + 1) * HEAD_DIM)
            q = q_ref[rows, cols]
            if c == 0:
                kk = jnp.concatenate([kh_ref[:, cols], k_ref[rows, cols]], axis=0)
                vv = jnp.concatenate([vh_ref[:, cols], v_ref[rows, cols]], axis=0)
            else:
                hist = slice((c - 1) * ATTN_BLOCK, (c + 1) * ATTN_BLOCK)
                kk = k_ref[hist, cols]
                vv = v_ref[hist, cols]
            s = jnp.where(mask, _dot_nt(q, kk), NEG_BIG)
            m = jnp.max(s, axis=-1, keepdims=True)
            p = jnp.exp(s - m)
            l = jnp.sum(p, axis=-1, keepdims=True)
            o = _dot(p.astype(BF16), vv) / l
            o_ref[rows, cols] = o
            lse_tile = jnp.where(lane == h, m + jnp.log(l), lse_tile)
        lse_ref[rows, :] = lse_tile


def _group_attention(q, k, v, rows):
    B, dil, L, W = q.shape
    S = L * dil
    rows = min(rows, L)
    n_sub = rows // ATTN_BLOCK
    blk = pl.BlockSpec((None, None, rows, W), lambda b, r, i: (b, r, i, 0))
    halo = pl.BlockSpec((None, None, ATTN_BLOCK, W),
                        lambda b, r, i: (b, r, jnp.maximum(i * n_sub - 1, 0), 0))
    o, lse = pl.pallas_call(
        partial(_attn_kernel, n_sub=n_sub),
        grid=(B, dil, L // rows),
        in_specs=[blk, blk, blk, halo, halo],
        out_specs=[
            pl.BlockSpec((None, rows, W), lambda b, r, i: (b, i, r)),
            pl.BlockSpec((None, rows, LANES), lambda b, r, i: (b, i, r)),
        ],
        out_shape=[
            jax.ShapeDtypeStruct((B, L, dil * W), F32),
            jax.ShapeDtypeStruct((B, L, dil * LANES), F32),
        ],
        compiler_params=_params("parallel", "parallel", "parallel"),
        name="attn_band",
    )(q, k, v, k, v)
    return o.reshape(B, S, W), lse.reshape(B, S, LANES)


def _merge_proj_kernel(o1_ref, o2_ref, o3_ref, l1_ref, l2_ref, l3_ref, x_ref, w_ref, nw_ref,
                       xo_ref, h_ref, merged_ref):
    l1, l2, l3 = l1_ref[...], l2_ref[...], l3_ref[...]
    m = jnp.maximum(jnp.maximum(l1, l2), l3)
    e1, e2, e3 = jnp.exp(l1 - m), jnp.exp(l2 - m), jnp.exp(l3 - m)
    tot = e1 + e2 + e3
    a1, a2, a3 = e1 / tot, e2 / tot, e3 / tot
    for h in range(ATTN_HEADS):
        cols = slice(h * HEAD_DIM, (h + 1) * HEAD_DIM)
        o = (a1[:, h:h + 1] * o1_ref[:, cols] + a2[:, h:h + 1] * o2_ref[:, cols]
             + a3[:, h:h + 1] * o3_ref[:, cols])
        merged_ref[:, cols] = o.astype(BF16)
    xo = x_ref[...] + _dot(merged_ref[...], w_ref[...])
    xo_ref[...] = xo
    h_ref[...] = _rmsnorm_rows(xo, nw_ref[...]).astype(BF16)


def _merge_project(o_list, lse_list, x, w_out, nw, rows):
    B, S, D = x.shape
    T = B * S
    W = ATTN_WIDTH
    row_spec = lambda width: pl.BlockSpec((rows, width), lambda i: (i, 0))
    const = lambda shape: pl.BlockSpec(shape, lambda i: (0, 0))
    xo, h = pl.pallas_call(
        _merge_proj_kernel,
        grid=(T // rows,),
        in_specs=[row_spec(W)] * 3 + [row_spec(LANES)] * 3
        + [row_spec(D), const((W, D)), const((1, D))],
        out_specs=[row_spec(D), row_spec(D)],
        out_shape=[jax.ShapeDtypeStruct((T, D), F32), jax.ShapeDtypeStruct((T, D), BF16)],
        scratch_shapes=[pltpu.VMEM((rows, W), BF16)],
        compiler_params=_params("parallel"),
        name="attn_merge_proj",
    )(*[o.reshape(T, W) for o in o_list], *[l.reshape(T, LANES) for l in lse_list],
      x.reshape(T, D), w_out, nw)
    return xo.reshape(B, S, D), h.reshape(B, S, D)


def _proj_kernel(o_ref, x_ref, w_ref, nw_ref, xo_ref, h_ref):
    xo = x_ref[...] + _dot(o_ref[...], w_ref[...])
    xo_ref[...] = xo
    h_ref[...] = _rmsnorm_rows(xo, nw_ref[...]).astype(BF16)


def _project(o, x, w_out, nw, rows):
    B, S, D = x.shape
    T = B * S
    W = o.shape[-1]
    row_spec = lambda width: pl.BlockSpec((rows, width), lambda i: (i, 0))
    const = lambda shape: pl.BlockSpec(shape, lambda i: (0, 0))
    xo, h = pl.pallas_call(
        _proj_kernel,
        grid=(T // rows,),
        in_specs=[row_spec(W), row_spec(D), const((W, D)), const((1, D))],
        out_specs=[row_spec(D), row_spec(D)],
        out_shape=[jax.ShapeDtypeStruct((T, D), F32), jax.ShapeDtypeStruct((T, D), BF16)],
        compiler_params=_params("parallel"),
        name="out_proj",
    )(o.reshape(T, W), x.reshape(T, D), w_out, nw)
    return xo.reshape(B, S, D), h.reshape(B, S, D)


MLP_FF_CHUNK = 1024


def _mlp_kernel(h_ref, x_ref, wu_ref, wd_ref, nw_ref, o_ref, *, final_norm):
    h = h_ref[...]
    acc = x_ref[...]
    for f in range(D_FF // MLP_FF_CHUNK):
        cols = slice(f * MLP_FF_CHUNK, (f + 1) * MLP_FF_CHUNK)
        a = jnp.maximum(_dot(h, wu_ref[:, cols]), 0.0)
        acc = acc + _dot((a * a).astype(BF16), wd_ref[cols, :])
    if final_norm:
        acc = _rmsnorm_rows(acc, nw_ref[...])
    o_ref[...] = acc


def _mlp(h, x, w_up, w_down, nw, rows, final_norm):
    B, S, D = x.shape
    T = B * S
    row_spec = pl.BlockSpec((rows, D), lambda i: (i, 0))
    const = lambda shape: pl.BlockSpec(shape, lambda i: (0, 0), pipeline_mode=pl.Buffered(1))
    out = pl.pallas_call(
        partial(_mlp_kernel, final_norm=final_norm),
        grid=(T // rows,),
        in_specs=[row_spec, row_spec, const((D, D_FF)), const((D_FF, D)), const((1, D))],
        out_specs=row_spec,
        out_shape=jax.ShapeDtypeStruct((T, D), F32),
        compiler_params=_params("parallel"),
        name="mlp",
    )(h.reshape(T, D), x.reshape(T, D), w_up, w_down, nw)
    return out.reshape(B, S, D)


def _gla_in_kernel(x_ref, nw_ref, w_ref, wg_ref, wgu_ref, bg_ref,
                   q_ref, k_ref, v_ref, g_ref, la_ref):
    h = _rmsnorm_rows(x_ref[...], nw_ref[...]).astype(BF16)
    q_ref[...] = _dot(h, w_ref[:, :GLA_KEY_DIM])
    k_ref[...] = _dot(h, w_ref[:, GLA_KEY_DIM:2 * GLA_KEY_DIM])
    v_lo = 2 * GLA_KEY_DIM
    v_ref[...] = _dot(h, w_ref[:, v_lo:v_lo + GLA_VAL_DIM]).astype(BF16)
    g_lo = v_lo + GLA_VAL_DIM
    g_ref[...] = _dot(h, w_ref[:, g_lo:g_lo + GLA_VAL_DIM])
    gate_lr = _dot(h, wg_ref[...]).astype(BF16)
    gk = _dot(gate_lr, wgu_ref[...]) + bg_ref[...]
    log_sig = jnp.minimum(gk, 0.0) - jnp.log(1.0 + jnp.exp(-jnp.abs(gk)))
    la_ref[...] = log_sig * (1.0 / GLA_GATE_NORMALIZER)


def _gla_in_project(x, nw, w_main, w_gate, w_gate_up, b_gate, rows):
    B, S, D = x.shape
    T = B * S
    row_spec = lambda width: pl.BlockSpec((rows, width), lambda i: (i, 0))
    const = lambda shape: pl.BlockSpec(shape, lambda i: (0, 0))
    n_main = 2 * GLA_KEY_DIM + 2 * GLA_VAL_DIM
    return pl.pallas_call(
        _gla_in_kernel,
        grid=(T // rows,),
        in_specs=[row_spec(D), const((1, D)), const((D, n_main)), const((D, LANES)),
                  const((LANES, GLA_KEY_DIM)), const((1, GLA_KEY_DIM))],
        out_specs=[row_spec(GLA_KEY_DIM), row_spec(GLA_KEY_DIM), row_spec(GLA_VAL_DIM),
                   row_spec(GLA_VAL_DIM), row_spec(GLA_KEY_DIM)],
        out_shape=[
            jax.ShapeDtypeStruct((T, GLA_KEY_DIM), F32),
            jax.ShapeDtypeStruct((T, GLA_KEY_DIM), F32),
            jax.ShapeDtypeStruct((T, GLA_VAL_DIM), BF16),
            jax.ShapeDtypeStruct((T, GLA_VAL_DIM), F32),
            jax.ShapeDtypeStruct((T, GLA_KEY_DIM), F32),
        ],
        compiler_params=_params("parallel"),
        name="gla_in",
    )(x.reshape(T, D), nw, w_main, w_gate, w_gate_up, b_gate)


def _gla_kernel(q_ref, k_ref, v_ref, g_ref, la_ref, nw_ref, o_ref, state_ref, *, n_chunks):
    @pl.when(pl.program_id(1) == 0)
    def _():
        state_ref[...] = jnp.zeros_like(state_ref)

    ci = lax.broadcasted_iota(jnp.int32, (GLA_CHUNK, GLA_CHUNK), 0)
    cj = lax.broadcasted_iota(jnp.int32, (GLA_CHUNK, GLA_CHUNK), 1)
    causal = ci >= cj
    tril = causal.astype(BF16)
    ones = jnp.ones((GLA_CHUNK, LANES), BF16)
    nw = nw_ref[...]
    for c in range(n_chunks):
        rows = slice(c * GLA_CHUNK, (c + 1) * GLA_CHUNK)
        la = la_ref[rows, :]
        la_hi = la.astype(BF16)
        la_lo = (la - la_hi.astype(F32)).astype(BF16)
        b = _dot(tril, la_hi) + _dot(tril, la_lo)
        b_last = b[GLA_CHUNK - 1:GLA_CHUNK, :]
        eb = jnp.exp(b)
        enb = jnp.exp(-b)
        eu = jnp.exp(b_last - b)
        q_t = (q_ref[rows, :] * (GLA_DK ** -0.5) * eb).astype(BF16)
        k_all = k_ref[rows, :]
        k_t = (k_all * enb).astype(BF16)
        k_u = (k_all * eu).astype(BF16)
        for h in range(GLA_HEADS):
            kc = slice(h * GLA_DK, (h + 1) * GLA_DK)
            vc = slice(h * GLA_DV, (h + 1) * GLA_DV)
            v = v_ref[rows, vc]
            a = jnp.where(causal, _dot_nt(q_t[:, kc], k_t[:, kc]), 0.0).astype(BF16)
            state = state_ref[h]
            o = _dot(a, v) + _dot(q_t[:, kc], state.astype(BF16))
            bl_t = _dot_tn(la_hi[:, kc], ones) + _dot_tn(la_lo[:, kc], ones)
            decay = jnp.exp(bl_t)
            decay = jnp.concatenate([decay] * (GLA_DV // LANES), axis=1)
            state_ref[h] = decay * state + _dot_tn(k_u[:, kc], v)
            var = jnp.mean(o * o, axis=-1, keepdims=True)
            o = o * lax.rsqrt(var + NORM_EPS) * nw
            g = g_ref[rows, vc]
            o_ref[rows, vc] = (o * (g / (1.0 + jnp.exp(-g)))).astype(BF16)


def _gla_recurrence(q, k, v, g, la, nw, B, S, rows):
    T = B * S
    n_blocks = S // rows
    row_spec = lambda width: pl.BlockSpec((rows, width), lambda b, i: (b * n_blocks + i, 0))
    return pl.pallas_call(
        partial(_gla_kernel, n_chunks=rows // GLA_CHUNK),
        grid=(B, n_blocks),
        in_specs=[row_spec(GLA_KEY_DIM), row_spec(GLA_KEY_DIM), row_spec(GLA_VAL_DIM),
                  row_spec(GLA_VAL_DIM), row_spec(GLA_KEY_DIM),
                  pl.BlockSpec((1, GLA_DV), lambda b, i: (0, 0))],
        out_specs=row_spec(GLA_VAL_DIM),
        out_shape=jax.ShapeDtypeStruct((T, GLA_VAL_DIM), BF16),
        scratch_shapes=[pltpu.VMEM((GLA_HEADS, GLA_DK, GLA_DV), F32)],
        compiler_params=_params("parallel", "arbitrary"),
        name="gla_recurrence",
    )(q, k, v, g, la, nw)


def _rope_tables(seq_len):
    pos = jnp.arange(seq_len, dtype=F32)
    inv_freq = ROPE_THETA ** (-jnp.arange(0, HEAD_DIM, 2, dtype=F32) / HEAD_DIM)
    ang = pos[:, None] * inv_freq[None, :]
    cos, sin = jnp.cos(ang), jnp.sin(ang)
    return jnp.concatenate([cos, cos], axis=-1), jnp.concatenate([-sin, sin], axis=-1)


def kernel(x, norm_mix_w, norm_mlp_w, final_norm_w, attn_w_in, attn_w_out, gla_w_in,
           gla_w_gate_up, gla_b_gate, gla_norm_w, gla_w_out, mlp_w_up, mlp_w_down):
    B, S, D = x.shape
    row = lambda w: w.reshape(1, -1).astype(F32)

    cos2, sin2 = _rope_tables(S)
    qkv = _qkv_project(x, row(norm_mix_w[0]), attn_w_in[0].astype(BF16), cos2, sin2, rows=512)
    outs, lses = [], []
    for g, (window, dil) in enumerate(DILATED_GROUPS):
        assert window // dil == ATTN_BLOCK and (S // dil) % ATTN_BLOCK == 0
        q, k, v = qkv[3 * g:3 * g + 3]
        o, lse = _group_attention(q, k, v, rows=512)
        outs.append(o)
        lses.append(lse)
    x, h = _merge_project(outs, lses, x, attn_w_out[0].astype(BF16), row(norm_mlp_w[0]), rows=512)
    x = _mlp(h, x, mlp_w_up[0].astype(BF16), mlp_w_down[0].astype(BF16), row(final_norm_w),
             rows=512, final_norm=False)

    n_main = 2 * GLA_KEY_DIM + 2 * GLA_VAL_DIM
    w_gla = gla_w_in[0]
    w_gate = jnp.pad(w_gla[:, n_main:], ((0, 0), (0, LANES - GLA_GATE_RANK))).astype(BF16)
    w_gate_up = jnp.pad(gla_w_gate_up[0], ((0, LANES - GLA_GATE_RANK), (0, 0))).astype(BF16)
    q, k, v, g, la = _gla_in_project(x, row(norm_mix_w[1]), w_gla[:, :n_main].astype(BF16),
                                     w_gate, w_gate_up, row(gla_b_gate[0]), rows=512)
    o = _gla_recurrence(q, k, v, g, la, row(gla_norm_w[0]), B, S, rows=512)
    x, h = _project(o.reshape(B, S, GLA_VAL_DIM), x, gla_w_out[0].astype(BF16),
                    row(norm_mlp_w[1]), rows=512)
    x = _mlp(h, x, mlp_w_up[1].astype(BF16), mlp_w_down[1].astype(BF16), row(final_norm_w),
             rows=512, final_norm=True)
    return x
```

```python
from functools import partial

import jax
import jax.numpy as jnp
from jax import lax
from jax.experimental import pallas as pl
from jax.experimental.pallas import tpu as pltpu

D_MODEL = 1024
NORM_EPS = 1e-5
DILATED_GROUPS = ((128, 1), (512, 4), (2048, 16))
ATTN_HEADS = 8
HEAD_DIM = 128
ATTN_WIDTH = ATTN_HEADS * HEAD_DIM
ATTN_BLOCK = 128
ROPE_THETA = 10000.0
GLA_HEADS = 4
GLA_DK = 128
GLA_DV = 256
GLA_KEY_DIM = GLA_HEADS * GLA_DK
GLA_VAL_DIM = GLA_HEADS * GLA_DV
GLA_GATE_RANK = 16
GLA_GATE_NORMALIZER = 16.0
GLA_CHUNK = 64
D_FF = 4 * D_MODEL

LANES = 128
VMEM_LIMIT_BYTES = 56 * 1024 * 1024
NEG_BIG = -1e30

BF16 = jnp.bfloat16
F32 = jnp.float32


def _params(*semantics):
    return pltpu.CompilerParams(dimension_semantics=semantics,
                                vmem_limit_bytes=VMEM_LIMIT_BYTES)


def _rmsnorm_rows(x, w):
    var = jnp.mean(x * x, axis=-1, keepdims=True)
    return x * lax.rsqrt(var + NORM_EPS) * w


def _dot(a, b):
    return jnp.dot(a, b, preferred_element_type=F32)


def _dot_nt(a, b):
    return lax.dot_general(a, b, (((1,), (1,)), ((), ())), preferred_element_type=F32)


def _dot_tn(a, b):
    return lax.dot_general(a, b, (((0,), (0,)), ((), ())), preferred_element_type=F32)


QKV_CHUNK = 256


def _tile_residue_major(t, rows, dil):
    n, width = t.shape
    return t.reshape(n // rows, rows // dil, dil, width).transpose(0, 2, 1, 3).reshape(n, width)


def _qkv_kernel(*refs):
    n_slabs = D_MODEL // LANES
    x_refs = refs[:n_slabs]
    nw_ref, w_ref, cos_ref, sin_ref = refs[n_slabs:n_slabs + 4]
    out_refs = refs[n_slabs + 4:]
    rows = x_refs[0].shape[0]
    n_chunks = ATTN_WIDTH // QKV_CHUNK
    for g, (_, dil) in enumerate(DILATED_GROUPS):
        n = rows // dil

        def residue_major(x_ref):
            if dil == 1:
                return x_ref[...]
            return jnp.concatenate([x_ref[pl.ds(r, n, stride=dil), :] for r in range(dil)], axis=0)

        xg = jnp.concatenate([residue_major(x_ref) for x_ref in x_refs], axis=1)
        h = _rmsnorm_rows(xg, nw_ref[...]).astype(BF16)
        cos = cos_ref[g]
        sin = sin_ref[g]
        for part in range(3):
            o_ref = out_refs[3 * g + part]
            for j in range(n_chunks):
                col = (3 * g + part) * ATTN_WIDTH + j * QKV_CHUNK
                acc = _dot(h, w_ref[:, col:col + QKV_CHUNK])
                for half in range(QKV_CHUNK // HEAD_DIM):
                    t = acc[:, half * HEAD_DIM:(half + 1) * HEAD_DIM]
                    if part < 2:
                        t = t * cos + pltpu.roll(t, HEAD_DIM // 2, 1) * sin
                    if part == 0:
                        t = t * (HEAD_DIM ** -0.5)
                    t = t.astype(BF16)
                    lo = j * QKV_CHUNK + half * HEAD_DIM
                    for r in range(dil):
                        o_ref[r, :, lo:lo + HEAD_DIM] = t[r * n:(r + 1) * n]


def _qkv_project(x, nw, w_in, cos2, sin2, rows):
    B, S, D = x.shape
    n_blocks = S // rows
    n_groups = len(DILATED_GROUPS)
    out_shapes, out_specs = [], []
    for _, dil in DILATED_GROUPS:
        for _ in range(3):
            out_shapes.append(jax.ShapeDtypeStruct((B, dil, S // dil, ATTN_WIDTH), BF16))
            out_specs.append(pl.BlockSpec((None, dil, rows // dil, ATTN_WIDTH),
                                          lambda i: (i // n_blocks, 0, i % n_blocks, 0)))
    cos_g = jnp.stack([_tile_residue_major(cos2, rows, dil) for _, dil in DILATED_GROUPS])
    sin_g = jnp.stack([_tile_residue_major(sin2, rows, dil) for _, dil in DILATED_GROUPS])
    tab_spec = pl.BlockSpec((n_groups, rows, HEAD_DIM), lambda i: (0, i % n_blocks, 0))
    const = lambda shape: pl.BlockSpec(shape, lambda i: (0, 0), pipeline_mode=pl.Buffered(1))
    xf = x.reshape(B * S, D)
    slab_specs = [pl.BlockSpec((rows, LANES), lambda i, c=c: (i, c)) for c in range(D // LANES)]
    return pl.pallas_call(
        _qkv_kernel,
        grid=(B * n_blocks,),
        in_specs=slab_specs + [const((1, D)), const((D, w_in.shape[1])), tab_spec, tab_spec],
        out_specs=out_specs,
        out_shape=out_shapes,
        compiler_params=_params("parallel"),
        name="attn_qkv",
    )(*([xf] * (D // LANES)), nw, w_in, cos_g, sin_g)


def _attn_kernel(q_ref, k_ref, v_ref, kh_ref, vh_ref, o_ref, lse_ref, *, n_sub):
    i = pl.program_id(2)
    qi = lax.broadcasted_iota(jnp.int32, (ATTN_BLOCK, 2 * ATTN_BLOCK), 0)
    kj = lax.broadcasted_iota(jnp.int32, (ATTN_BLOCK, 2 * ATTN_BLOCK), 1)
    rel = qi - kj + ATTN_BLOCK
    band = (rel >= 0) & (rel <= ATTN_BLOCK)
    band_first = band & (kj >= jnp.where(i > 0, 0, ATTN_BLOCK))
    lane = lax.broadcasted_iota(jnp.int32, (ATTN_BLOCK, LANES), 1)
    for c in range(n_sub):
        rows = slice(c * ATTN_BLOCK, (c + 1) * ATTN_BLOCK)
        mask = band_first if c == 0 else band
        lse_tile = jnp.zeros((ATTN_BLOCK, LANES), F32)
        for h in range(ATTN_HEADS):
            cols = slice(h * HEAD_DIM, (h + 1) * HEAD_DIM)
            q = q_ref[rows, cols]
            if c == 0:
                kk = jnp.concatenate([kh_ref[:, cols], k_ref[rows, cols]], axis=0)
                vv = jnp.concatenate([vh_ref[:, cols], v_ref[rows, cols]], axis=0)
            else:
                hist = slice((c - 1) * ATTN_BLOCK, (c + 1) * ATTN_BLOCK)
                kk = k_ref[hist, cols]
                vv = v_ref[hist, cols]
            s = jnp.where(mask, _dot_nt(q, kk), NEG_BIG)
            m = jnp.max(s, axis=-1, keepdims=True)
            p = jnp.exp(s - m)
            l = jnp.sum(p, axis=-1, keepdims=True)
            o = _dot(p.astype(BF16), vv) / l
            o_ref[rows, cols] = o
            lse_tile = jnp.where(lane == h, m + jnp.log(l), lse_tile)
        lse_ref[rows, :] = lse_tile


def _group_attention(q, k, v, rows):
    B, dil, L, W = q.shape
    rows = min(rows, L)
    n_sub = rows // ATTN_BLOCK
    blk = lambda width: pl.BlockSpec((None, None, rows, width), lambda b, r, i: (b, r, i, 0))
    halo = pl.BlockSpec((None, None, ATTN_BLOCK, W),
                        lambda b, r, i: (b, r, jnp.maximum(i * n_sub - 1, 0), 0))
    return pl.pallas_call(
        partial(_attn_kernel, n_sub=n_sub),
        grid=(B, dil, L // rows),
        in_specs=[blk(W), blk(W), blk(W), halo, halo],
        out_specs=[blk(W), blk(LANES)],
        out_shape=[
            jax.ShapeDtypeStruct((B, dil, L, W), F32),
            jax.ShapeDtypeStruct((B, dil, L, LANES), F32),
        ],
        compiler_params=_params("parallel", "parallel", "parallel"),
        name="attn_band",
    )(q, k, v, k, v)


def _merge_proj_kernel(*refs):
    n_groups = len(DILATED_GROUPS)
    o_refs = [refs[g * ATTN_HEADS:(g + 1) * ATTN_HEADS] for g in range(n_groups)]
    lse_refs = refs[n_groups * ATTN_HEADS:n_groups * (ATTN_HEADS + 1)]
    x_ref, w_ref, nw_ref, xo_ref, h_ref, merged_ref, nat_ref = refs[n_groups * (ATTN_HEADS + 1):]
    rows = x_ref.shape[0]

    def natural(src_ref, dil, slot):
        if dil == 1:
            return src_ref[0]
        for r in range(dil):
            nat_ref[slot, pl.ds(r, rows // dil, stride=dil), :] = src_ref[r]
        return nat_ref[slot]

    lses = [natural(lse_refs[g], dil, g) for g, (_, dil) in enumerate(DILATED_GROUPS)]
    m = jnp.maximum(jnp.maximum(lses[0], lses[1]), lses[2])
    es = [jnp.exp(l - m) for l in lses]
    inv = 1.0 / (es[0] + es[1] + es[2])
    alphas = [e * inv for e in es]
    for h in range(ATTN_HEADS):
        o = None
        for g, (_, dil) in enumerate(DILATED_GROUPS):
            slot = n_groups + g * ATTN_HEADS + h
            term = alphas[g][:, h:h + 1] * natural(o_refs[g][h], dil, slot)
            o = term if o is None else o + term
        merged_ref[:, h * HEAD_DIM:(h + 1) * HEAD_DIM] = o.astype(BF16)
    xo = x_ref[...] + _dot(merged_ref[...], w_ref[...])
    xo_ref[...] = xo
    h_ref[...] = _rmsnorm_rows(xo, nw_ref[...]).astype(BF16)


def _merge_project(o_list, lse_list, x, w_out, nw, rows):
    B, S, D = x.shape
    W = ATTN_WIDTH
    n_blocks = S // rows
    n_groups = len(DILATED_GROUPS)

    def residue_spec(dil, col):
        return pl.BlockSpec((None, dil, rows // dil, LANES), lambda b, i: (b, 0, i, col))

    o_specs = [residue_spec(dil, h) for _, dil in DILATED_GROUPS for h in range(ATTN_HEADS)]
    o_args = [o for o in o_list for _ in range(ATTN_HEADS)]
    lse_specs = [residue_spec(dil, 0) for _, dil in DILATED_GROUPS]
    row_spec = pl.BlockSpec((rows, D), lambda b, i: (b * n_blocks + i, 0))
    const = lambda shape: pl.BlockSpec(shape, lambda b, i: (0, 0), pipeline_mode=pl.Buffered(1))
    xo, h = pl.pallas_call(
        _merge_proj_kernel,
        grid=(B, n_blocks),
        in_specs=o_specs + lse_specs + [row_spec, const((W, D)), const((1, D))],
        out_specs=[row_spec, row_spec],
        out_shape=[jax.ShapeDtypeStruct((B * S, D), F32), jax.ShapeDtypeStruct((B * S, D), BF16)],
        scratch_shapes=[pltpu.VMEM((rows, W), BF16),
                        pltpu.VMEM((n_groups * (ATTN_HEADS + 1), rows, LANES), F32)],
        compiler_params=_params("parallel", "parallel"),
        name="attn_merge_proj",
    )(*o_args, *lse_list, x.reshape(B * S, D), w_out, nw)
    return xo.reshape(B, S, D), h.reshape(B, S, D)


def _proj_kernel(o_ref, x_ref, w_ref, nw_ref, xo_ref, h_ref):
    xo = x_ref[...] + _dot(o_ref[...], w_ref[...])
    xo_ref[...] = xo
    h_ref[...] = _rmsnorm_rows(xo, nw_ref[...]).astype(BF16)


def _project(o, x, w_out, nw, rows):
    B, S, D = x.shape
    T = B * S
    W = o.shape[-1]
    row_spec = lambda width: pl.BlockSpec((rows, width), lambda i: (i, 0))
    const = lambda shape: pl.BlockSpec(shape, lambda i: (0, 0))
    xo, h = pl.pallas_call(
        _proj_kernel,
        grid=(T // rows,),
        in_specs=[row_spec(W), row_spec(D), const((W, D)), const((1, D))],
        out_specs=[row_spec(D), row_spec(D)],
        out_shape=[jax.ShapeDtypeStruct((T, D), F32), jax.ShapeDtypeStruct((T, D), BF16)],
        compiler_params=_params("parallel"),
        name="out_proj",
    )(o.reshape(T, W), x.reshape(T, D), w_out, nw)
    return xo.reshape(B, S, D), h.reshape(B, S, D)


MLP_FF_CHUNK = 1024


def _mlp_kernel(h_ref, x_ref, wu_ref, wd_ref, nw_ref, o_ref, *, final_norm):
    h = h_ref[...]
    acc = x_ref[...]
    for f in range(D_FF // MLP_FF_CHUNK):
        cols = slice(f * MLP_FF_CHUNK, (f + 1) * MLP_FF_CHUNK)
        a = jnp.maximum(_dot(h, wu_ref[:, cols]), 0.0)
        acc = acc + _dot((a * a).astype(BF16), wd_ref[cols, :])
    if final_norm:
        acc = _rmsnorm_rows(acc, nw_ref[...])
    o_ref[...] = acc


def _mlp(h, x, w_up, w_down, nw, rows, final_norm):
    B, S, D = x.shape
    T = B * S
    row_spec = pl.BlockSpec((rows, D), lambda i: (i, 0))
    const = lambda shape: pl.BlockSpec(shape, lambda i: (0, 0), pipeline_mode=pl.Buffered(1))
    out = pl.pallas_call(
        partial(_mlp_kernel, final_norm=final_norm),
        grid=(T // rows,),
        in_specs=[row_spec, row_spec, const((D, D_FF)), const((D_FF, D)), const((1, D))],
        out_specs=row_spec,
        out_shape=jax.ShapeDtypeStruct((T, D), F32),
        compiler_params=_params("parallel"),
        name="mlp",
    )(h.reshape(T, D), x.reshape(T, D), w_up, w_down, nw)
    return out.reshape(B, S, D)


def _gla_in_kernel(x_ref, nw_ref, w_ref, wg_ref, wgu_ref, bg_ref,
                   q_ref, k_ref, v_ref, g_ref, la_ref):
    h = _rmsnorm_rows(x_ref[...], nw_ref[...]).astype(BF16)
    q_ref[...] = _dot(h, w_ref[:, :GLA_KEY_DIM])
    k_ref[...] = _dot(h, w_ref[:, GLA_KEY_DIM:2 * GLA_KEY_DIM])
    v_lo = 2 * GLA_KEY_DIM
    v_ref[...] = _dot(h, w_ref[:, v_lo:v_lo + GLA_VAL_DIM]).astype(BF16)
    g_lo = v_lo + GLA_VAL_DIM
    g_ref[...] = _dot(h, w_ref[:, g_lo:g_lo + GLA_VAL_DIM])
    gate_lr = _dot(h, wg_ref[...]).astype(BF16)
    gk = _dot(gate_lr, wgu_ref[...]) + bg_ref[...]
    log_sig = jnp.minimum(gk, 0.0) - jnp.log(1.0 + jnp.exp(-jnp.abs(gk)))
    la_ref[...] = log_sig * (1.0 / GLA_GATE_NORMALIZER)


def _gla_in_project(x, nw, w_main, w_gate, w_gate_up, b_gate, rows):
    B, S, D = x.shape
    T = B * S
    row_spec = lambda width: pl.BlockSpec((rows, width), lambda i: (i, 0))
    const = lambda shape: pl.BlockSpec(shape, lambda i: (0, 0))
    n_main = 2 * GLA_KEY_DIM + 2 * GLA_VAL_DIM
    return pl.pallas_call(
        _gla_in_kernel,
        grid=(T // rows,),
        in_specs=[row_spec(D), const((1, D)), const((D, n_main)), const((D, LANES)),
                  const((LANES, GLA_KEY_DIM)), const((1, GLA_KEY_DIM))],
        out_specs=[row_spec(GLA_KEY_DIM), row_spec(GLA_KEY_DIM), row_spec(GLA_VAL_DIM),
                   row_spec(GLA_VAL_DIM), row_spec(GLA_KEY_DIM)],
        out_shape=[
            jax.ShapeDtypeStruct((T, GLA_KEY_DIM), F32),
            jax.ShapeDtypeStruct((T, GLA_KEY_DIM), F32),
            jax.ShapeDtypeStruct((T, GLA_VAL_DIM), BF16),
            jax.ShapeDtypeStruct((T, GLA_VAL_DIM), F32),
            jax.ShapeDtypeStruct((T, GLA_KEY_DIM), F32),
        ],
        compiler_params=_params("parallel"),
        name="gla_in",
    )(x.reshape(T, D), nw, w_main, w_gate, w_gate_up, b_gate)


def _gla_kernel(q_ref, k_ref, v_ref, g_ref, la_ref, nw_ref, o_ref, state_ref, *, n_chunks):
    @pl.when(pl.program_id(1) == 0)
    def _():
        state_ref[...] = jnp.zeros_like(state_ref)

    ci = lax.broadcasted_iota(jnp.int32, (GLA_CHUNK, GLA_CHUNK), 0)
    cj = lax.broadcasted_iota(jnp.int32, (GLA_CHUNK, GLA_CHUNK), 1)
    causal = ci >= cj
    tril = causal.astype(BF16)
    ones = jnp.ones((GLA_CHUNK, LANES), BF16)
    nw = nw_ref[...]
    for c in range(n_chunks):
        rows = slice(c * GLA_CHUNK, (c + 1) * GLA_CHUNK)
        la = la_ref[rows, :]
        la_hi = la.astype(BF16)
        la_lo = (la - la_hi.astype(F32)).astype(BF16)
        b = _dot(tril, la_hi) + _dot(tril, la_lo)
        b_last = b[GLA_CHUNK - 1:GLA_CHUNK, :]
        eb = jnp.exp(b)
        enb = jnp.exp(-b)
        eu = jnp.exp(b_last - b)
        q_t = (q_ref[rows, :] * (GLA_DK ** -0.5) * eb).astype(BF16)
        k_all = k_ref[rows, :]
        k_t = (k_all * enb).astype(BF16)
        k_u = (k_all * eu).astype(BF16)
        for h in range(GLA_HEADS):
            kc = slice(h * GLA_DK, (h + 1) * GLA_DK)
            vc = slice(h * GLA_DV, (h + 1) * GLA_DV)
            v = v_ref[rows, vc]
            a = jnp.where(causal, _dot_nt(q_t[:, kc], k_t[:, kc]), 0.0).astype(BF16)
            state = state_ref[h]
            o = _dot(a, v) + _dot(q_t[:, kc], state.astype(BF16))
            bl_t = _dot_tn(la_hi[:, kc], ones) + _dot_tn(la_lo[:, kc], ones)
            decay = jnp.exp(bl_t)
            decay = jnp.concatenate([decay] * (GLA_DV // LANES), axis=1)
            state_ref[h] = decay * state + _dot_tn(k_u[:, kc], v)
            var = jnp.mean(o * o, axis=-1, keepdims=True)
            o = o * lax.rsqrt(var + NORM_EPS) * nw
            g = g_ref[rows, vc]
            o_ref[rows, vc] = (o * (g / (1.0 + jnp.exp(-g)))).astype(BF16)


def _gla_recurrence(q, k, v, g, la, nw, B, S, rows):
    T = B * S
    n_blocks = S // rows
    row_spec = lambda width: pl.BlockSpec((rows, width), lambda b, i: (b * n_blocks + i, 0))
    return pl.pallas_call(
        partial(_gla_kernel, n_chunks=rows // GLA_CHUNK),
        grid=(B, n_blocks),
        in_specs=[row_spec(GLA_KEY_DIM), row_spec(GLA_KEY_DIM), row_spec(GLA_VAL_DIM),
                  row_spec(GLA_VAL_DIM), row_spec(GLA_KEY_DIM),
                  pl.BlockSpec((1, GLA_DV), lambda b, i: (0, 0))],
        out_specs=row_spec(GLA_VAL_DIM),
        out_shape=jax.ShapeDtypeStruct((T, GLA_VAL_DIM), BF16),
        scratch_shapes=[pltpu.VMEM((GLA_HEADS, GLA_DK, GLA_DV), F32)],
        compiler_params=_params("parallel", "arbitrary"),
        name="gla_recurrence",
    )(q, k, v, g, la, nw)


def _rope_tables(seq_len):
    pos = jnp.arange(seq_len, dtype=F32)
    inv_freq = ROPE_THETA ** (-jnp.arange(0, HEAD_DIM, 2, dtype=F32) / HEAD_DIM)
    ang = pos[:, None] * inv_freq[None, :]
    cos, sin = jnp.cos(ang), jnp.sin(ang)
    return jnp.concatenate([cos, cos], axis=-1), jnp.concatenate([-sin, sin], axis=-1)


def kernel(x, norm_mix_w, norm_mlp_w, final_norm_w, attn_w_in, attn_w_out, gla_w_in,
           gla_w_gate_up, gla_b_gate, gla_norm_w, gla_w_out, mlp_w_up, mlp_w_down):
    B, S, D = x.shape
    row = lambda w: w.reshape(1, -1).astype(F32)

    cos2, sin2 = _rope_tables(S)
    qkv = _qkv_project(x, row(norm_mix_w[0]), attn_w_in[0].astype(BF16), cos2, sin2, rows=512)
    outs, lses = [], []
    for g, (window, dil) in enumerate(DILATED_GROUPS):
        assert window // dil == ATTN_BLOCK and (S // dil) % ATTN_BLOCK == 0
        q, k, v = qkv[3 * g:3 * g + 3]
        o, lse = _group_attention(q, k, v, rows=512)
        outs.append(o)
        lses.append(lse)
    x, h = _merge_project(outs, lses, x, attn_w_out[0].astype(BF16), row(norm_mlp_w[0]), rows=512)
    x = _mlp(h, x, mlp_w_up[0].astype(BF16), mlp_w_down[0].astype(BF16), row(final_norm_w),
             rows=512, final_norm=False)

    n_main = 2 * GLA_KEY_DIM + 2 * GLA_VAL_DIM
    w_gla = gla_w_in[0]
    w_gate = jnp.pad(w_gla[:, n_main:], ((0, 0), (0, LANES - GLA_GATE_RANK))).astype(BF16)
    w_gate_up = jnp.pad(gla_w_gate_up[0], ((0, LANES - GLA_GATE_RANK), (0, 0))).astype(BF16)
    q, k, v, g, la = _gla_in_project(x, row(norm_mix_w[1]), w_gla[:, :n_main].astype(BF16),
                                     w_gate, w_gate_up, row(gla_b_gate[0]), rows=512)
    o = _gla_recurrence(q, k, v, g, la, row(gla_norm_w[0]), B, S, rows=512)
    x, h = _project(o.reshape(B, S, GLA_VAL_DIM), x, gla_w_out[0].astype(BF16),
                    row(norm_mlp_w[1]), rows=512)
    x = _mlp(h, x, mlp_w_up[1].astype(BF16), mlp_w_down[1].astype(BF16), row(final_norm_w),
             rows=512, final_norm=True)
    return x
```

```python
from functools import partial

import jax
import jax.numpy as jnp
import numpy as np
from jax import lax
from jax.experimental import pallas as pl
from jax.experimental.pallas import tpu as pltpu

D_MODEL = 1024
NORM_EPS = 1e-5
DILATED_GROUPS = ((128, 1), (512, 4), (2048, 16))
ATTN_HEADS = 8
HEAD_DIM = 128
ATTN_WIDTH = ATTN_HEADS * HEAD_DIM
ATTN_BLOCK = 128
ROPE_THETA = 10000.0
GLA_HEADS = 4
GLA_DK = 128
GLA_DV = 256
GLA_KEY_DIM = GLA_HEADS * GLA_DK
GLA_VAL_DIM = GLA_HEADS * GLA_DV
GLA_GATE_RANK = 16
GLA_GATE_NORMALIZER = 16.0
GLA_CHUNK = 64
D_FF = 4 * D_MODEL

LANES = 128
VMEM_LIMIT_BYTES = 56 * 1024 * 1024
NEG_BIG = -1e30

BF16 = jnp.bfloat16
F32 = jnp.float32


def _params(*semantics):
    return pltpu.CompilerParams(dimension_semantics=semantics,
                                vmem_limit_bytes=VMEM_LIMIT_BYTES)


def _rmsnorm_rows(x, w):
    var = jnp.mean(x * x, axis=-1, keepdims=True)
    return x * lax.rsqrt(var + NORM_EPS) * w


def _dot(a, b):
    return jnp.dot(a, b, preferred_element_type=F32)


def _dot_nt(a, b):
    return lax.dot_general(a, b, (((1,), (1,)), ((), ())), preferred_element_type=F32)


def _dot_tn(a, b):
    return lax.dot_general(a, b, (((0,), (0,)), ((), ())), preferred_element_type=F32)


QKV_CHUNK = 256


def _qkv_kernel(*refs):
    n_slabs = D_MODEL // LANES
    x_refs = refs[:n_slabs]
    nw_ref, w_ref, cos_ref, sin_ref = refs[n_slabs:n_slabs + 4]
    out_refs = refs[n_slabs + 4:]
    rows = x_refs[0].shape[0]
    n_chunks = ATTN_WIDTH // QKV_CHUNK
    for g, (_, dil) in enumerate(DILATED_GROUPS):
        n = rows // dil

        def residue_major(x_ref):
            if dil == 1:
                return x_ref[...]
            return jnp.concatenate([x_ref[pl.ds(r, n, stride=dil), :] for r in range(dil)], axis=0)

        xg = jnp.concatenate([residue_major(x_ref) for x_ref in x_refs], axis=1)
        h = _rmsnorm_rows(xg, nw_ref[...]).astype(BF16)
        cos = residue_major(cos_ref)
        sin = residue_major(sin_ref)
        for part in range(3):
            o_ref = out_refs[3 * g + part]
            for j in range(n_chunks):
                col = (3 * g + part) * ATTN_WIDTH + j * QKV_CHUNK
                acc = _dot(h, w_ref[:, col:col + QKV_CHUNK])
                for half in range(QKV_CHUNK // HEAD_DIM):
                    t = acc[:, half * HEAD_DIM:(half + 1) * HEAD_DIM]
                    if part < 2:
                        t = t * cos + pltpu.roll(t, HEAD_DIM // 2, 1) * sin
                    if part == 0:
                        t = t * (HEAD_DIM ** -0.5)
                    t = t.astype(BF16)
                    lo = j * QKV_CHUNK + half * HEAD_DIM
                    for r in range(dil):
                        o_ref[r, :, lo:lo + HEAD_DIM] = t[r * n:(r + 1) * n]


def _qkv_project(x, nw, w_in, cos2, sin2, rows):
    B, S, D = x.shape
    n_blocks = S // rows
    out_shapes, out_specs = [], []
    for _, dil in DILATED_GROUPS:
        for _ in range(3):
            out_shapes.append(jax.ShapeDtypeStruct((B, dil, S // dil, ATTN_WIDTH), BF16))
            out_specs.append(pl.BlockSpec((None, dil, rows // dil, ATTN_WIDTH),
                                          lambda i: (i // n_blocks, 0, i % n_blocks, 0)))
    tab_spec = pl.BlockSpec((rows, HEAD_DIM), lambda i: (i % n_blocks, 0))
    const = lambda shape: pl.BlockSpec(shape, lambda i: (0, 0), pipeline_mode=pl.Buffered(1))
    xf = x.reshape(B * S, D)
    slab_specs = [pl.BlockSpec((rows, LANES), lambda i, c=c: (i, c)) for c in range(D // LANES)]
    return pl.pallas_call(
        _qkv_kernel,
        grid=(B * n_blocks,),
        in_specs=slab_specs + [const((1, D)), const((D, w_in.shape[1])), tab_spec, tab_spec],
        out_specs=out_specs,
        out_shape=out_shapes,
        compiler_params=_params("parallel"),
        name="attn_qkv",
    )(*([xf] * (D // LANES)), nw, w_in, cos2, sin2)


def _attn_kernel(q_ref, k_ref, v_ref, kh_ref, vh_ref, o_ref, lse_ref, *, n_sub):
    i = pl.program_id(2)
    qi = lax.broadcasted_iota(jnp.int32, (ATTN_BLOCK, 2 * ATTN_BLOCK), 0)
    kj = lax.broadcasted_iota(jnp.int32, (ATTN_BLOCK, 2 * ATTN_BLOCK), 1)
    rel = qi - kj + ATTN_BLOCK
    band = (rel >= 0) & (rel <= ATTN_BLOCK)
    band_first = band & (kj >= jnp.where(i > 0, 0, ATTN_BLOCK))
    lane = lax.broadcasted_iota(jnp.int32, (ATTN_BLOCK, LANES), 1)
    for c in range(n_sub):
        rows = slice(c * ATTN_BLOCK, (c + 1) * ATTN_BLOCK)
        mask = band_first if c == 0 else band
        lse_tile = jnp.zeros((ATTN_BLOCK, LANES), F32)
        for h in range(ATTN_HEADS):
            cols = slice(h * HEAD_DIM, (h + 1) * HEAD_DIM)
            q = q_ref[rows, cols]
            if c == 0:
                kk = jnp.concatenate([kh_ref[:, cols], k_ref[rows, cols]], axis=0)
                vv = jnp.concatenate([vh_ref[:, cols], v_ref[rows, cols]], axis=0)
            else:
                hist = slice((c - 1) * ATTN_BLOCK, (c + 1) * ATTN_BLOCK)
                kk = k_ref[hist, cols]
                vv = v_ref[hist, cols]
            s = jnp.where(mask, _dot_nt(q, kk), NEG_BIG)
            m = jnp.max(s, axis=-1, keepdims=True)
            p = jnp.exp(s - m)
            l = jnp.sum(p, axis=-1, keepdims=True)
            o = _dot(p.astype(BF16), vv) / l
            o_ref[rows, cols] = o
            lse_tile = jnp.where(lane == h, m + jnp.log(l), lse_tile)
        lse_ref[rows, :] = lse_tile


def _group_attention(q, k, v, rows):
    B, dil, L, W = q.shape
    rows = min(rows, L)
    n_sub = rows // ATTN_BLOCK
    blk = lambda width: pl.BlockSpec((None, None, rows, width), lambda b, r, i: (b, r, i, 0))
    halo = pl.BlockSpec((None, None, ATTN_BLOCK, W),
                        lambda b, r, i: (b, r, jnp.maximum(i * n_sub - 1, 0), 0))
    return pl.pallas_call(
        partial(_attn_kernel, n_sub=n_sub),
        grid=(B, dil, L // rows),
        in_specs=[blk(W), blk(W), blk(W), halo, halo],
        out_specs=[blk(W), blk(LANES)],
        out_shape=[
            jax.ShapeDtypeStruct((B, dil, L, W), F32),
            jax.ShapeDtypeStruct((B, dil, L, LANES), F32),
        ],
        compiler_params=_params("parallel", "parallel", "parallel"),
        name="attn_band",
    )(q, k, v, k, v)


MLP_FF_CHUNK = 1024


def _project_and_mlp(mixed, x_ref, wo_ref, nw_ref, wu_ref, wd_ref):
    x_mid = x_ref[...] + _dot(mixed, wo_ref[...])
    h = _rmsnorm_rows(x_mid, nw_ref[...]).astype(BF16)
    acc = x_mid
    for f in range(D_FF // MLP_FF_CHUNK):
        cols = slice(f * MLP_FF_CHUNK, (f + 1) * MLP_FF_CHUNK)
        a = jnp.maximum(_dot(h, wu_ref[:, cols]), 0.0)
        acc = acc + _dot((a * a).astype(BF16), wd_ref[cols, :])
    return acc


def _attn_tail_kernel(*refs):
    n_groups = len(DILATED_GROUPS)
    o_refs = [refs[g * ATTN_HEADS:(g + 1) * ATTN_HEADS] for g in range(n_groups)]
    lse_refs = refs[n_groups * ATTN_HEADS:n_groups * (ATTN_HEADS + 1)]
    (x_ref, wo_ref, nw_ref, wu_ref, wd_ref, out_ref,
     merged_ref, nat_ref) = refs[n_groups * (ATTN_HEADS + 1):]
    rows = x_ref.shape[0]

    def natural(src_ref, dil, slot):
        if dil == 1:
            return src_ref[0]
        for r in range(dil):
            nat_ref[slot, pl.ds(r, rows // dil, stride=dil), :] = src_ref[r]
        return nat_ref[slot]

    lses = [natural(lse_refs[g], dil, g) for g, (_, dil) in enumerate(DILATED_GROUPS)]
    m = jnp.maximum(jnp.maximum(lses[0], lses[1]), lses[2])
    es = [jnp.exp(l - m) for l in lses]
    inv = 1.0 / (es[0] + es[1] + es[2])
    alphas = [e * inv for e in es]
    for h in range(ATTN_HEADS):
        o = None
        for g, (_, dil) in enumerate(DILATED_GROUPS):
            slot = n_groups + 2 * g + h % 2
            term = alphas[g][:, h:h + 1] * natural(o_refs[g][h], dil, slot)
            o = term if o is None else o + term
        merged_ref[:, h * HEAD_DIM:(h + 1) * HEAD_DIM] = o.astype(BF16)
    out_ref[...] = _project_and_mlp(merged_ref[...], x_ref, wo_ref, nw_ref, wu_ref, wd_ref)


def _attn_tail(o_list, lse_list, x, w_out, nw, w_up, w_down, rows):
    B, S, D = x.shape
    W = ATTN_WIDTH
    n_blocks = S // rows
    n_groups = len(DILATED_GROUPS)

    def residue_spec(dil, col):
        return pl.BlockSpec((None, dil, rows // dil, LANES), lambda b, i: (b, 0, i, col))

    o_specs = [residue_spec(dil, h) for _, dil in DILATED_GROUPS for h in range(ATTN_HEADS)]
    o_args = [o for o in o_list for _ in range(ATTN_HEADS)]
    lse_specs = [residue_spec(dil, 0) for _, dil in DILATED_GROUPS]
    row_spec = pl.BlockSpec((rows, D), lambda b, i: (b * n_blocks + i, 0))
    const = lambda shape: pl.BlockSpec(shape, lambda b, i: (0, 0), pipeline_mode=pl.Buffered(1))
    out = pl.pallas_call(
        _attn_tail_kernel,
        grid=(B, n_blocks),
        in_specs=o_specs + lse_specs + [row_spec, const((W, D)), const((1, D)),
                                        const((D, D_FF)), const((D_FF, D))],
        out_specs=row_spec,
        out_shape=jax.ShapeDtypeStruct((B * S, D), F32),
        scratch_shapes=[pltpu.VMEM((rows, W), BF16),
                        pltpu.VMEM((3 * n_groups, rows, LANES), F32)],
        compiler_params=_params("parallel", "parallel"),
        name="attn_tail",
    )(*o_args, *lse_list, x.reshape(B * S, D), w_out, nw, w_up, w_down)
    return out.reshape(B, S, D)


def _gla_tail_kernel(o_ref, x_ref, wo_ref, nw_ref, wu_ref, wd_ref, nwf_ref, out_ref):
    acc = _project_and_mlp(o_ref[...], x_ref, wo_ref, nw_ref, wu_ref, wd_ref)
    out_ref[...] = _rmsnorm_rows(acc, nwf_ref[...])


def _gla_tail(o, x, w_out, nw, w_up, w_down, nw_final, rows):
    B, S, D = x.shape
    T = B * S
    row_spec = lambda width: pl.BlockSpec((rows, width), lambda i: (i, 0))
    const = lambda shape: pl.BlockSpec(shape, lambda i: (0, 0), pipeline_mode=pl.Buffered(1))
    out = pl.pallas_call(
        _gla_tail_kernel,
        grid=(T // rows,),
        in_specs=[row_spec(o.shape[-1]), row_spec(D), const((o.shape[-1], D)), const((1, D)),
                  const((D, D_FF)), const((D_FF, D)), const((1, D))],
        out_specs=row_spec(D),
        out_shape=jax.ShapeDtypeStruct((T, D), F32),
        compiler_params=_params("parallel"),
        name="gla_tail",
    )(o, x.reshape(T, D), w_out, nw, w_up, w_down, nw_final)
    return out.reshape(B, S, D)


def _gla_in_kernel(x_ref, nw_ref, w_ref, wg_ref, wgu_ref, bg_ref,
                   q_ref, k_ref, v_ref, g_ref, la_ref):
    h = _rmsnorm_rows(x_ref[...], nw_ref[...]).astype(BF16)
    q_ref[...] = _dot(h, w_ref[:, :GLA_KEY_DIM])
    k_ref[...] = _dot(h, w_ref[:, GLA_KEY_DIM:2 * GLA_KEY_DIM])
    v_lo = 2 * GLA_KEY_DIM
    v_ref[...] = _dot(h, w_ref[:, v_lo:v_lo + GLA_VAL_DIM]).astype(BF16)
    g_lo = v_lo + GLA_VAL_DIM
    g_ref[...] = _dot(h, w_ref[:, g_lo:g_lo + GLA_VAL_DIM])
    gate_lr = _dot(h, wg_ref[...]).astype(BF16)
    gk = _dot(gate_lr, wgu_ref[...]) + bg_ref[...]
    log_sig = jnp.minimum(gk, 0.0) - jnp.log(1.0 + jnp.exp(-jnp.abs(gk)))
    la_ref[...] = log_sig * (1.0 / GLA_GATE_NORMALIZER)


def _gla_in_project(x, nw, w_main, w_gate, w_gate_up, b_gate, rows):
    B, S, D = x.shape
    T = B * S
    row_spec = lambda width: pl.BlockSpec((rows, width), lambda i: (i, 0))
    const = lambda shape: pl.BlockSpec(shape, lambda i: (0, 0))
    n_main = 2 * GLA_KEY_DIM + 2 * GLA_VAL_DIM
    return pl.pallas_call(
        _gla_in_kernel,
        grid=(T // rows,),
        in_specs=[row_spec(D), const((1, D)), const((D, n_main)), const((D, LANES)),
                  const((LANES, GLA_KEY_DIM)), const((1, GLA_KEY_DIM))],
        out_specs=[row_spec(GLA_KEY_DIM), row_spec(GLA_KEY_DIM), row_spec(GLA_VAL_DIM),
                   row_spec(GLA_VAL_DIM), row_spec(GLA_KEY_DIM)],
        out_shape=[
            jax.ShapeDtypeStruct((T, GLA_KEY_DIM), F32),
            jax.ShapeDtypeStruct((T, GLA_KEY_DIM), F32),
            jax.ShapeDtypeStruct((T, GLA_VAL_DIM), BF16),
            jax.ShapeDtypeStruct((T, GLA_VAL_DIM), F32),
            jax.ShapeDtypeStruct((T, GLA_KEY_DIM), F32),
        ],
        compiler_params=_params("parallel"),
        name="gla_in",
    )(x.reshape(T, D), nw, w_main, w_gate, w_gate_up, b_gate)


def _gla_kernel(q_ref, k_ref, v_ref, g_ref, la_ref, nw_ref, o_ref, state_ref, *, n_chunks):
    @pl.when(pl.program_id(1) == 0)
    def _():
        state_ref[...] = jnp.zeros_like(state_ref)

    ci = lax.broadcasted_iota(jnp.int32, (GLA_CHUNK, GLA_CHUNK), 0)
    cj = lax.broadcasted_iota(jnp.int32, (GLA_CHUNK, GLA_CHUNK), 1)
    causal = ci >= cj
    tril = causal.astype(BF16)
    ones = jnp.ones((GLA_CHUNK, LANES), BF16)
    nw = nw_ref[...]
    for c in range(n_chunks):
        rows = slice(c * GLA_CHUNK, (c + 1) * GLA_CHUNK)
        la = la_ref[rows, :]
        la_hi = la.astype(BF16)
        la_lo = (la - la_hi.astype(F32)).astype(BF16)
        b = _dot(tril, la_hi) + _dot(tril, la_lo)
        b_last = b[GLA_CHUNK - 1:GLA_CHUNK, :]
        eb = jnp.exp(b)
        enb = jnp.exp(-b)
        eu = jnp.exp(b_last - b)
        q_t = (q_ref[rows, :] * (GLA_DK ** -0.5) * eb).astype(BF16)
        k_all = k_ref[rows, :]
        k_t = (k_all * enb).astype(BF16)
        k_u = (k_all * eu).astype(BF16)
        for h in range(GLA_HEADS):
            kc = slice(h * GLA_DK, (h + 1) * GLA_DK)
            vc = slice(h * GLA_DV, (h + 1) * GLA_DV)
            v = v_ref[rows, vc]
            a = jnp.where(causal, _dot_nt(q_t[:, kc], k_t[:, kc]), 0.0).astype(BF16)
            state = state_ref[h]
            o = _dot(a, v) + _dot(q_t[:, kc], state.astype(BF16))
            bl_t = _dot_tn(la_hi[:, kc], ones) + _dot_tn(la_lo[:, kc], ones)
            decay = jnp.exp(bl_t)
            decay = jnp.concatenate([decay] * (GLA_DV // LANES), axis=1)
            state_ref[h] = decay * state + _dot_tn(k_u[:, kc], v)
            var = jnp.mean(o * o, axis=-1, keepdims=True)
            o = o * lax.rsqrt(var + NORM_EPS) * nw
            g = g_ref[rows, vc]
            o_ref[rows, vc] = (o * (g / (1.0 + jnp.exp(-g)))).astype(BF16)


def _gla_recurrence(q, k, v, g, la, nw, B, S, rows):
    T = B * S
    n_blocks = S // rows
    row_spec = lambda width: pl.BlockSpec((rows, width), lambda b, i: (b * n_blocks + i, 0))
    return pl.pallas_call(
        partial(_gla_kernel, n_chunks=rows // GLA_CHUNK),
        grid=(B, n_blocks),
        in_specs=[row_spec(GLA_KEY_DIM), row_spec(GLA_KEY_DIM), row_spec(GLA_VAL_DIM),
                  row_spec(GLA_VAL_DIM), row_spec(GLA_KEY_DIM),
                  pl.BlockSpec((1, GLA_DV), lambda b, i: (0, 0))],
        out_specs=row_spec(GLA_VAL_DIM),
        out_shape=jax.ShapeDtypeStruct((T, GLA_VAL_DIM), BF16),
        scratch_shapes=[pltpu.VMEM((GLA_HEADS, GLA_DK, GLA_DV), F32)],
        compiler_params=_params("parallel", "arbitrary"),
        name="gla_recurrence",
    )(q, k, v, g, la, nw)


def _rope_tables(seq_len):
    pos = np.arange(seq_len, dtype=np.float64)
    inv_freq = ROPE_THETA ** (-np.arange(0, HEAD_DIM, 2, dtype=np.float64) / HEAD_DIM)
    ang = pos[:, None] * inv_freq[None, :]
    cos, sin = np.cos(ang), np.sin(ang)
    cos2 = np.concatenate([cos, cos], axis=-1).astype(np.float32)
    sin2 = np.concatenate([-sin, sin], axis=-1).astype(np.float32)
    return jnp.asarray(cos2), jnp.asarray(sin2)


def kernel(x, norm_mix_w, norm_mlp_w, final_norm_w, attn_w_in, attn_w_out, gla_w_in,
           gla_w_gate_up, gla_b_gate, gla_norm_w, gla_w_out, mlp_w_up, mlp_w_down):
    B, S, D = x.shape
    row = lambda w: w.reshape(1, -1).astype(F32)

    cos2, sin2 = _rope_tables(S)
    qkv = _qkv_project(x, row(norm_mix_w[0]), attn_w_in[0].astype(BF16), cos2, sin2, rows=512)
    outs, lses = [], []
    for g, (window, dil) in enumerate(DILATED_GROUPS):
        assert window // dil == ATTN_BLOCK and (S // dil) % ATTN_BLOCK == 0
        q, k, v = qkv[3 * g:3 * g + 3]
        o, lse = _group_attention(q, k, v, rows=512)
        outs.append(o)
        lses.append(lse)
    x = _attn_tail(outs, lses, x, attn_w_out[0].astype(BF16), row(norm_mlp_w[0]),
                   mlp_w_up[0].astype(BF16), mlp_w_down[0].astype(BF16), rows=512)

    n_main = 2 * GLA_KEY_DIM + 2 * GLA_VAL_DIM
    w_gla = gla_w_in[0]
    w_gate = jnp.pad(w_gla[:, n_main:], ((0, 0), (0, LANES - GLA_GATE_RANK))).astype(BF16)
    w_gate_up = jnp.pad(gla_w_gate_up[0], ((0, LANES - GLA_GATE_RANK), (0, 0))).astype(BF16)
    q, k, v, g, la = _gla_in_project(x, row(norm_mix_w[1]), w_gla[:, :n_main].astype(BF16),
                                     w_gate, w_gate_up, row(gla_b_gate[0]), rows=512)
    o = _gla_recurrence(q, k, v, g, la, row(gla_norm_w[0]), B, S, rows=512)
    return _gla_tail(o, x, gla_w_out[0].astype(BF16), row(norm_mlp_w[1]),
                     mlp_w_up[1].astype(BF16), mlp_w_down[1].astype(BF16), row(final_norm_w),
                     rows=512)
```

```python
from functools import partial

import jax
import jax.numpy as jnp
import numpy as np
from jax import lax
from jax.experimental import pallas as pl
from jax.experimental.pallas import tpu as pltpu

D_MODEL = 1024
NORM_EPS = 1e-5
DILATED_GROUPS = ((128, 1), (512, 4), (2048, 16))
ATTN_HEADS = 8
HEAD_DIM = 128
ATTN_WIDTH = ATTN_HEADS * HEAD_DIM
ATTN_BLOCK = 128
ROPE_THETA = 10000.0
GLA_HEADS = 4
GLA_DK = 128
GLA_DV = 256
GLA_KEY_DIM = GLA_HEADS * GLA_DK
GLA_VAL_DIM = GLA_HEADS * GLA_DV
GLA_GATE_RANK = 16
GLA_GATE_NORMALIZER = 16.0
GLA_CHUNK = 64
D_FF = 4 * D_MODEL

LANES = 128
VMEM_LIMIT_BYTES = 56 * 1024 * 1024
NEG_BIG = -1e30

BF16 = jnp.bfloat16
F32 = jnp.float32


def _params(*semantics):
    return pltpu.CompilerParams(dimension_semantics=semantics,
                                vmem_limit_bytes=VMEM_LIMIT_BYTES)


def _rmsnorm_rows(x, w):
    var = jnp.mean(x * x, axis=-1, keepdims=True)
    return x * lax.rsqrt(var + NORM_EPS) * w


def _dot(a, b):
    return jnp.dot(a, b, preferred_element_type=F32)


def _dot_nt(a, b):
    return lax.dot_general(a, b, (((1,), (1,)), ((), ())), preferred_element_type=F32)


def _dot_tn(a, b):
    return lax.dot_general(a, b, (((0,), (0,)), ((), ())), preferred_element_type=F32)


QKV_CHUNK = 256


def _qkv_kernel(*refs):
    n_slabs = D_MODEL // LANES
    x_refs = refs[:n_slabs]
    nw_ref, w_ref, cos_ref, sin_ref = refs[n_slabs:n_slabs + 4]
    out_refs, tab_ref = refs[n_slabs + 4:-1], refs[-1]
    rows = x_refs[0].shape[0]
    n_chunks = ATTN_WIDTH // QKV_CHUNK
    row_blocks = rows // ATTN_BLOCK
    for g, (_, dil) in enumerate(DILATED_GROUPS):
        n = rows // dil

        def residue_major(src_ref):
            if dil == 1:
                return src_ref[...]
            return jnp.concatenate([src_ref[pl.ds(r, n, stride=dil), :] for r in range(dil)], axis=0)

        xg = jnp.concatenate([residue_major(x_ref) for x_ref in x_refs], axis=1)
        h = _rmsnorm_rows(xg, nw_ref[...]).astype(BF16)
        if dil == 1:
            cos_src, sin_src = cos_ref, sin_ref
        else:
            tab_ref[0] = residue_major(cos_ref)
            tab_ref[1] = residue_major(sin_ref)
            cos_src, sin_src = tab_ref.at[0], tab_ref.at[1]
        for part in range(3):
            o_ref = out_refs[3 * g + part]
            for j in range(n_chunks):
                col = (3 * g + part) * ATTN_WIDTH + j * QKV_CHUNK
                acc = _dot(h, w_ref[:, col:col + QKV_CHUNK])
                for half in range(QKV_CHUNK // HEAD_DIM):
                    lo = j * QKV_CHUNK + half * HEAD_DIM
                    for rb in range(row_blocks):
                        rs = slice(rb * ATTN_BLOCK, (rb + 1) * ATTN_BLOCK)
                        t = acc[rs, half * HEAD_DIM:(half + 1) * HEAD_DIM]
                        if part < 2:
                            t = t * cos_src[rs, :] + pltpu.roll(t, HEAD_DIM // 2, 1) * sin_src[rs, :]
                        if part == 0:
                            t = t * (HEAD_DIM ** -0.5)
                        t = t.astype(BF16)
                        if n >= ATTN_BLOCK:
                            first = rb * ATTN_BLOCK
                            o_ref[first // n, pl.ds(first % n, ATTN_BLOCK), lo:lo + HEAD_DIM] = t
                        else:
                            for k in range(ATTN_BLOCK // n):
                                o_ref[rb * (ATTN_BLOCK // n) + k, :, lo:lo + HEAD_DIM] = (
                                    t[k * n:(k + 1) * n])


def _qkv_project(x, nw, w_in, cos2, sin2, rows):
    B, S, D = x.shape
    n_blocks = S // rows
    out_shapes, out_specs = [], []
    for _, dil in DILATED_GROUPS:
        for _ in range(3):
            out_shapes.append(jax.ShapeDtypeStruct((B, dil, S // dil, ATTN_WIDTH), BF16))
            out_specs.append(pl.BlockSpec((None, dil, rows // dil, ATTN_WIDTH),
                                          lambda i: (i // n_blocks, 0, i % n_blocks, 0)))
    tab_spec = pl.BlockSpec((rows, HEAD_DIM), lambda i: (i % n_blocks, 0))
    const = lambda shape: pl.BlockSpec(shape, lambda i: (0, 0), pipeline_mode=pl.Buffered(1))
    xf = x.reshape(B * S, D)
    slab_specs = [pl.BlockSpec((rows, LANES), lambda i, c=c: (i, c)) for c in range(D // LANES)]
    return pl.pallas_call(
        _qkv_kernel,
        grid=(B * n_blocks,),
        in_specs=slab_specs + [const((1, D)), const((D, w_in.shape[1])), tab_spec, tab_spec],
        out_specs=out_specs,
        out_shape=out_shapes,
        scratch_shapes=[pltpu.VMEM((2, rows, HEAD_DIM), F32)],
        compiler_params=_params("parallel"),
        name="attn_qkv",
    )(*([xf] * (D // LANES)), nw, w_in, cos2, sin2)


def _attn_kernel(q_ref, k_ref, v_ref, kh_ref, vh_ref, o_ref, lse_ref, *, n_sub):
    i = pl.program_id(2)
    qi = lax.broadcasted_iota(jnp.int32, (ATTN_BLOCK, 2 * ATTN_BLOCK), 0)
    kj = lax.broadcasted_iota(jnp.int32, (ATTN_BLOCK, 2 * ATTN_BLOCK), 1)
    rel = qi - kj + ATTN_BLOCK
    band = (rel >= 0) & (rel <= ATTN_BLOCK)
    band_first = band & (kj >= jnp.where(i > 0, 0, ATTN_BLOCK))
    lane = lax.broadcasted_iota(jnp.int32, (ATTN_BLOCK, LANES), 1)
    for c in range(n_sub):
        rows = slice(c * ATTN_BLOCK, (c + 1) * ATTN_BLOCK)
        mask = band_first if c == 0 else band
        lse_tile = jnp.zeros((ATTN_BLOCK, LANES), F32)
        for h in range(ATTN_HEADS):
            cols = slice(h * HEAD_DIM, (h + 1) * HEAD_DIM)
            q = q_ref[rows, cols]
            if c == 0:
                kk = jnp.concatenate([kh_ref[:, cols], k_ref[rows, cols]], axis=0)
                vv = jnp.concatenate([vh_ref[:, cols], v_ref[rows, cols]], axis=0)
            else:
                hist = slice((c - 1) * ATTN_BLOCK, (c + 1) * ATTN_BLOCK)
                kk = k_ref[hist, cols]
                vv = v_ref[hist, cols]
            s = jnp.where(mask, _dot_nt(q, kk), NEG_BIG)
            m = jnp.max(s, axis=-1, keepdims=True)
            p = jnp.exp(s - m)
            l = jnp.sum(p, axis=-1, keepdims=True)
            o = _dot(p.astype(BF16), vv) / l
            o_ref[rows, cols] = o
            lse_tile = jnp.where(lane == h, m + jnp.log(l), lse_tile)
        lse_ref[rows, :] = lse_tile


def _group_attention(q, k, v, rows):
    B, dil, L, W = q.shape
    rows = min(rows, L)
    n_sub = rows // ATTN_BLOCK
    blk = lambda width: pl.BlockSpec((None, None, rows, width), lambda b, r, i: (b, r, i, 0))
    halo = pl.BlockSpec((None, None, ATTN_BLOCK, W),
                        lambda b, r, i: (b, r, jnp.maximum(i * n_sub - 1, 0), 0))
    return pl.pallas_call(
        partial(_attn_kernel, n_sub=n_sub),
        grid=(B, dil, L // rows),
        in_specs=[blk(W), blk(W), blk(W), halo, halo],
        out_specs=[blk(W), blk(LANES)],
        out_shape=[
            jax.ShapeDtypeStruct((B, dil, L, W), F32),
            jax.ShapeDtypeStruct((B, dil, L, LANES), F32),
        ],
        compiler_params=_params("parallel", "parallel", "parallel"),
        name="attn_band",
    )(q, k, v, k, v)


MLP_FF_CHUNK = 1024


def _project_and_mlp(mixed, x_ref, wo_ref, nw_ref, wu_ref, wd_ref, side_work=None):
    x_mid = x_ref[...] + _dot(mixed, wo_ref[...])
    h = _rmsnorm_rows(x_mid, nw_ref[...]).astype(BF16)
    acc = x_mid
    for f in range(D_FF // MLP_FF_CHUNK):
        cols = slice(f * MLP_FF_CHUNK, (f + 1) * MLP_FF_CHUNK)
        a = jnp.maximum(_dot(h, wu_ref[:, cols]), 0.0)
        acc = acc + _dot((a * a).astype(BF16), wd_ref[cols, :])
        if side_work is not None:
            side_work(f)
    return acc


def _attn_tail_kernel(*refs):
    n_groups = len(DILATED_GROUPS)
    o_refs = [refs[g * ATTN_HEADS:(g + 1) * ATTN_HEADS] for g in range(n_groups)]
    lse_refs = refs[n_groups * ATTN_HEADS:n_groups * (ATTN_HEADS + 1)]
    (x_ref, wo_ref, nw_ref, wu_ref, wd_ref, out_ref,
     merged_ref, nat_ref) = refs[n_groups * (ATTN_HEADS + 1):]
    rows = x_ref.shape[0]
    step = pl.program_id(0)
    write_slot = step % 2
    read_slot = 1 - write_slot

    @pl.when(step == 0)
    def _():
        merged_ref[1] = jnp.zeros(merged_ref.shape[1:], BF16)

    def natural(src_ref, dil, slot):
        if dil == 1:
            return src_ref[0]
        for r in range(dil):
            nat_ref[slot, pl.ds(r, rows // dil, stride=dil), :] = src_ref[r]
        return nat_ref[slot]

    mixed = merged_ref[read_slot]
    lses = [natural(lse_refs[g], dil, g) for g, (_, dil) in enumerate(DILATED_GROUPS)]
    m = jnp.maximum(jnp.maximum(lses[0], lses[1]), lses[2])
    es = [jnp.exp(l - m) for l in lses]
    inv = 1.0 / (es[0] + es[1] + es[2])
    alphas = [e * inv for e in es]
    heads_per_chunk = ATTN_HEADS // (D_FF // MLP_FF_CHUNK)

    def merge_heads(f):
        for h in range(f * heads_per_chunk, (f + 1) * heads_per_chunk):
            o = None
            for g, (_, dil) in enumerate(DILATED_GROUPS):
                slot = n_groups + 2 * g + h % 2
                term = alphas[g][:, h:h + 1] * natural(o_refs[g][h], dil, slot)
                o = term if o is None else o + term
            merged_ref[write_slot, :, h * HEAD_DIM:(h + 1) * HEAD_DIM] = o.astype(BF16)

    out_ref[...] = _project_and_mlp(mixed, x_ref, wo_ref, nw_ref, wu_ref, wd_ref,
                                    side_work=merge_heads)


def _attn_tail(o_list, lse_list, x, w_out, nw, w_up, w_down, rows):
    B, S, D = x.shape
    W = ATTN_WIDTH
    n_blocks = S // rows
    n_groups = len(DILATED_GROUPS)

    n_tiles = B * n_blocks

    def residue_spec(dil, col):
        def index(s):
            t = jnp.minimum(s, n_tiles - 1)
            return (t // n_blocks, 0, t % n_blocks, col)
        return pl.BlockSpec((None, dil, rows // dil, LANES), index)

    o_specs = [residue_spec(dil, h) for _, dil in DILATED_GROUPS for h in range(ATTN_HEADS)]
    o_args = [o for o in o_list for _ in range(ATTN_HEADS)]
    lse_specs = [residue_spec(dil, 0) for _, dil in DILATED_GROUPS]
    row_spec = pl.BlockSpec((rows, D), lambda s: (jnp.maximum(s - 1, 0), 0))
    const = lambda shape: pl.BlockSpec(shape, lambda s: (0, 0), pipeline_mode=pl.Buffered(1))
    out = pl.pallas_call(
        _attn_tail_kernel,
        grid=(n_tiles + 1,),
        in_specs=o_specs + lse_specs + [row_spec, const((W, D)), const((1, D)),
                                        const((D, D_FF)), const((D_FF, D))],
        out_specs=row_spec,
        out_shape=jax.ShapeDtypeStruct((B * S, D), F32),
        scratch_shapes=[pltpu.VMEM((2, rows, W), BF16),
                        pltpu.VMEM((3 * n_groups, rows, LANES), F32)],
        compiler_params=_params("arbitrary"),
        name="attn_tail",
    )(*o_args, *lse_list, x.reshape(B * S, D), w_out, nw, w_up, w_down)
    return out.reshape(B, S, D)


def _gla_tail_kernel(o_ref, x_ref, wo_ref, nw_ref, wu_ref, wd_ref, nwf_ref, out_ref):
    acc = _project_and_mlp(o_ref[...], x_ref, wo_ref, nw_ref, wu_ref, wd_ref)
    out_ref[...] = _rmsnorm_rows(acc, nwf_ref[...])


def _gla_tail(o, x, w_out, nw, w_up, w_down, nw_final, rows):
    B, S, D = x.shape
    T = B * S
    row_spec = lambda width: pl.BlockSpec((rows, width), lambda i: (i, 0))
    const = lambda shape: pl.BlockSpec(shape, lambda i: (0, 0), pipeline_mode=pl.Buffered(1))
    out = pl.pallas_call(
        _gla_tail_kernel,
        grid=(T // rows,),
        in_specs=[row_spec(o.shape[-1]), row_spec(D), const((o.shape[-1], D)), const((1, D)),
                  const((D, D_FF)), const((D_FF, D)), const((1, D))],
        out_specs=row_spec(D),
        out_shape=jax.ShapeDtypeStruct((T, D), F32),
        compiler_params=_params("parallel"),
        name="gla_tail",
    )(o, x.reshape(T, D), w_out, nw, w_up, w_down, nw_final)
    return out.reshape(B, S, D)


def _gla_in_kernel(x_ref, nw_ref, w_ref, wg_ref, wgu_ref, bg_ref,
                   q_ref, k_ref, v_ref, g_ref, la_ref):
    h = _rmsnorm_rows(x_ref[...], nw_ref[...]).astype(BF16)
    q_ref[...] = _dot(h, w_ref[:, :GLA_KEY_DIM])
    k_ref[...] = _dot(h, w_ref[:, GLA_KEY_DIM:2 * GLA_KEY_DIM])
    v_lo = 2 * GLA_KEY_DIM
    v_ref[...] = _dot(h, w_ref[:, v_lo:v_lo + GLA_VAL_DIM]).astype(BF16)
    g_lo = v_lo + GLA_VAL_DIM
    g_ref[...] = _dot(h, w_ref[:, g_lo:g_lo + GLA_VAL_DIM])
    gate_lr = _dot(h, wg_ref[...]).astype(BF16)
    gk = _dot(gate_lr, wgu_ref[...]) + bg_ref[...]
    log_sig = jnp.minimum(gk, 0.0) - jnp.log(1.0 + jnp.exp(-jnp.abs(gk)))
    la_ref[...] = log_sig * (1.0 / GLA_GATE_NORMALIZER)


def _gla_in_project(x, nw, w_main, w_gate, w_gate_up, b_gate, rows):
    B, S, D = x.shape
    T = B * S
    row_spec = lambda width: pl.BlockSpec((rows, width), lambda i: (i, 0))
    const = lambda shape: pl.BlockSpec(shape, lambda i: (0, 0))
    n_main = 2 * GLA_KEY_DIM + 2 * GLA_VAL_DIM
    return pl.pallas_call(
        _gla_in_kernel,
        grid=(T // rows,),
        in_specs=[row_spec(D), const((1, D)), const((D, n_main)), const((D, LANES)),
                  const((LANES, GLA_KEY_DIM)), const((1, GLA_KEY_DIM))],
        out_specs=[row_spec(GLA_KEY_DIM), row_spec(GLA_KEY_DIM), row_spec(GLA_VAL_DIM),
                   row_spec(GLA_VAL_DIM), row_spec(GLA_KEY_DIM)],
        out_shape=[
            jax.ShapeDtypeStruct((T, GLA_KEY_DIM), F32),
            jax.ShapeDtypeStruct((T, GLA_KEY_DIM), F32),
            jax.ShapeDtypeStruct((T, GLA_VAL_DIM), BF16),
            jax.ShapeDtypeStruct((T, GLA_VAL_DIM), F32),
            jax.ShapeDtypeStruct((T, GLA_KEY_DIM), F32),
        ],
        compiler_params=_params("parallel"),
        name="gla_in",
    )(x.reshape(T, D), nw, w_main, w_gate, w_gate_up, b_gate)


def _gla_kernel(q_ref, k_ref, v_ref, g_ref, la_ref, nw_ref, o_ref, state_ref, *, n_chunks):
    @pl.when(pl.program_id(1) == 0)
    def _():
        state_ref[...] = jnp.zeros_like(state_ref)

    pair = 2 * GLA_CHUNK
    ri = lax.broadcasted_iota(jnp.int32, (pair, pair), 0)
    cj = lax.broadcasted_iota(jnp.int32, (pair, pair), 1)
    causal = ri >= cj
    cumsum_mat = (causal & ((ri >= GLA_CHUNK) == (cj >= GLA_CHUNK))).astype(BF16)
    first = lax.broadcasted_iota(jnp.int32, (pair, GLA_KEY_DIM), 0) < GLA_CHUNK
    ones = jnp.ones((pair, LANES), BF16)
    nw = nw_ref[...]
    for p in range(n_chunks // 2):
        rows = slice(p * pair, (p + 1) * pair)
        la = la_ref[rows, :]
        la_hi = la.astype(BF16)
        la_lo = (la - la_hi.astype(F32)).astype(BF16)
        b = _dot(cumsum_mat, la_hi) + _dot(cumsum_mat, la_lo)
        bl0 = b[GLA_CHUNK - 1:GLA_CHUNK, :]
        bl1 = b[pair - 1:pair, :]
        ref_b = jnp.where(first, b - bl0, b)
        q_e = q_ref[rows, :] * jnp.exp(ref_b)
        k_e = k_ref[rows, :] * jnp.exp(-ref_b)
        scale = GLA_DK ** -0.5
        q_sc = (q_e * scale).astype(BF16)
        k_sc = k_e.astype(BF16)
        q_st = (q_e * (jnp.exp(bl0) * scale)).astype(BF16)
        k_up = (k_e * jnp.exp(bl1)).astype(BF16)
        for h in range(GLA_HEADS):
            kc = slice(h * GLA_DK, (h + 1) * GLA_DK)
            vc = slice(h * GLA_DV, (h + 1) * GLA_DV)
            v = v_ref[rows, vc]
            a = jnp.where(causal, _dot_nt(q_sc[:, kc], k_sc[:, kc]), 0.0).astype(BF16)
            state = state_ref[h]
            o = _dot(jnp.concatenate([a, q_st[:, kc]], axis=1),
                     jnp.concatenate([v, state.astype(BF16)], axis=0))
            bl_t = _dot_tn(la_hi[:, kc], ones) + _dot_tn(la_lo[:, kc], ones)
            decay = jnp.exp(bl_t)
            decay = jnp.concatenate([decay] * (GLA_DV // LANES), axis=1)
            state_ref[h] = decay * state + _dot_tn(k_up[:, kc], v)
            var = jnp.mean(o * o, axis=-1, keepdims=True)
            o = o * lax.rsqrt(var + NORM_EPS) * nw
            g = g_ref[rows, vc]
            half_g = 0.5 * g
            o_ref[rows, vc] = (o * (half_g * (1.0 + jnp.tanh(half_g)))).astype(BF16)


def _gla_recurrence(q, k, v, g, la, nw, B, S, rows):
    T = B * S
    n_blocks = S // rows
    row_spec = lambda width: pl.BlockSpec((rows, width), lambda b, i: (b * n_blocks + i, 0))
    return pl.pallas_call(
        partial(_gla_kernel, n_chunks=rows // GLA_CHUNK),
        grid=(B, n_blocks),
        in_specs=[row_spec(GLA_KEY_DIM), row_spec(GLA_KEY_DIM), row_spec(GLA_VAL_DIM),
                  row_spec(GLA_VAL_DIM), row_spec(GLA_KEY_DIM),
                  pl.BlockSpec((1, GLA_DV), lambda b, i: (0, 0))],
        out_specs=row_spec(GLA_VAL_DIM),
        out_shape=jax.ShapeDtypeStruct((T, GLA_VAL_DIM), BF16),
        scratch_shapes=[pltpu.VMEM((GLA_HEADS, GLA_DK, GLA_DV), F32)],
        compiler_params=_params("parallel", "arbitrary"),
        name="gla_recurrence",
    )(q, k, v, g, la, nw)


def _rope_tables(seq_len):
    pos = np.arange(seq_len, dtype=np.float64)
    inv_freq = ROPE_THETA ** (-np.arange(0, HEAD_DIM, 2, dtype=np.float64) / HEAD_DIM)
    ang = pos[:, None] * inv_freq[None, :]
    cos, sin = np.cos(ang), np.sin(ang)
    cos2 = np.concatenate([cos, cos], axis=-1).astype(np.float32)
    sin2 = np.concatenate([-sin, sin], axis=-1).astype(np.float32)
    return jnp.asarray(cos2), jnp.asarray(sin2)


def kernel(x, norm_mix_w, norm_mlp_w, final_norm_w, attn_w_in, attn_w_out, gla_w_in,
           gla_w_gate_up, gla_b_gate, gla_norm_w, gla_w_out, mlp_w_up, mlp_w_down):
    B, S, D = x.shape
    row = lambda w: w.reshape(1, -1).astype(F32)

    cos2, sin2 = _rope_tables(S)
    qkv = _qkv_project(x, row(norm_mix_w[0]), attn_w_in[0].astype(BF16), cos2, sin2, rows=512)
    outs, lses = [], []
    for g, (window, dil) in enumerate(DILATED_GROUPS):
        assert window // dil == ATTN_BLOCK and (S // dil) % ATTN_BLOCK == 0
        q, k, v = qkv[3 * g:3 * g + 3]
        o, lse = _group_attention(q, k, v, rows=512)
        outs.append(o)
        lses.append(lse)
    x = _attn_tail(outs, lses, x, attn_w_out[0].astype(BF16), row(norm_mlp_w[0]),
                   mlp_w_up[0].astype(BF16), mlp_w_down[0].astype(BF16), rows=512)

    n_main = 2 * GLA_KEY_DIM + 2 * GLA_VAL_DIM
    w_gla = gla_w_in[0]
    w_gate = jnp.pad(w_gla[:, n_main:], ((0, 0), (0, LANES - GLA_GATE_RANK))).astype(BF16)
    w_gate_up = jnp.pad(gla_w_gate_up[0], ((0, LANES - GLA_GATE_RANK), (0, 0))).astype(BF16)
    q, k, v, g, la = _gla_in_project(x, row(norm_mix_w[1]), w_gla[:, :n_main].astype(BF16),
                                     w_gate, w_gate_up, row(gla_b_gate[0]), rows=512)
    o = _gla_recurrence(q, k, v, g, la, row(gla_norm_w[0]), B, S, rows=512)
    return _gla_tail(o, x, gla_w_out[0].astype(BF16), row(norm_mlp_w[1]),
                     mlp_w_up[1].astype(BF16), mlp_w_down[1].astype(BF16), row(final_norm_w),
                     rows=512)
```

```python
from functools import partial

import jax
import jax.numpy as jnp
import numpy as np
from jax import lax
from jax.experimental import pallas as pl
from jax.experimental.pallas import tpu as pltpu

D_MODEL = 1024
NORM_EPS = 1e-5
DILATED_GROUPS = ((128, 1), (512, 4), (2048, 16))
ATTN_HEADS = 8
HEAD_DIM = 128
ATTN_WIDTH = ATTN_HEADS * HEAD_DIM
ATTN_BLOCK = 128
ROPE_THETA = 10000.0
GLA_HEADS = 4
GLA_DK = 128
GLA_DV = 256
GLA_KEY_DIM = GLA_HEADS * GLA_DK
GLA_VAL_DIM = GLA_HEADS * GLA_DV
GLA_GATE_RANK = 16
GLA_GATE_NORMALIZER = 16.0
GLA_CHUNK = 64
D_FF = 4 * D_MODEL

LANES = 128
VMEM_LIMIT_BYTES = 56 * 1024 * 1024
NEG_BIG = -1e30

BF16 = jnp.bfloat16
F32 = jnp.float32


def _params(*semantics):
    return pltpu.CompilerParams(dimension_semantics=semantics,
                                vmem_limit_bytes=VMEM_LIMIT_BYTES)


def _rmsnorm_rows(x, w):
    var = jnp.mean(x * x, axis=-1, keepdims=True)
    return x * lax.rsqrt(var + NORM_EPS) * w


def _dot(a, b):
    return jnp.dot(a, b, preferred_element_type=F32)


def _dot_nt(a, b):
    return lax.dot_general(a, b, (((1,), (1,)), ((), ())), preferred_element_type=F32)


def _dot_tn(a, b):
    return lax.dot_general(a, b, (((0,), (0,)), ((), ())), preferred_element_type=F32)


QKV_CHUNK = 256
Q_SCALE = HEAD_DIM ** -0.5 * float(np.log2(np.e))


def _qkv_kernel(*refs):
    n_slabs = D_MODEL // LANES
    x_refs = refs[:n_slabs]
    nw_ref, w_ref, cos_ref, sin_ref = refs[n_slabs:n_slabs + 4]
    out_refs, tab_ref = refs[n_slabs + 4:-1], refs[-1]
    rows = x_refs[0].shape[0]
    n_chunks = ATTN_WIDTH // QKV_CHUNK
    row_blocks = rows // ATTN_BLOCK
    for g, (_, dil) in enumerate(DILATED_GROUPS):
        n = rows // dil

        def residue_major(src_ref):
            if dil == 1:
                return src_ref[...]
            return jnp.concatenate([src_ref[pl.ds(r, n, stride=dil), :] for r in range(dil)], axis=0)

        xg = jnp.concatenate([residue_major(x_ref) for x_ref in x_refs], axis=1)
        h = _rmsnorm_rows(xg, nw_ref[...]).astype(BF16)
        if dil == 1:
            cos_src, sin_src = cos_ref, sin_ref
        else:
            tab_ref[0] = residue_major(cos_ref)
            tab_ref[1] = residue_major(sin_ref)
            cos_src, sin_src = tab_ref.at[0], tab_ref.at[1]
        for part in range(3):
            o_ref = out_refs[3 * g + part]
            for j in range(n_chunks):
                col = (3 * g + part) * ATTN_WIDTH + j * QKV_CHUNK
                acc = _dot(h, w_ref[:, col:col + QKV_CHUNK])
                for half in range(QKV_CHUNK // HEAD_DIM):
                    lo = j * QKV_CHUNK + half * HEAD_DIM
                    for rb in range(row_blocks):
                        rs = slice(rb * ATTN_BLOCK, (rb + 1) * ATTN_BLOCK)
                        t = acc[rs, half * HEAD_DIM:(half + 1) * HEAD_DIM]
                        if part < 2:
                            t = t * cos_src[rs, :] + pltpu.roll(t, HEAD_DIM // 2, 1) * sin_src[rs, :]
                        if part == 0:
                            t = t * Q_SCALE
                        t = t.astype(BF16)
                        if n >= ATTN_BLOCK:
                            first = rb * ATTN_BLOCK
                            o_ref[first // n, pl.ds(first % n, ATTN_BLOCK), lo:lo + HEAD_DIM] = t
                        else:
                            for k in range(ATTN_BLOCK // n):
                                o_ref[rb * (ATTN_BLOCK // n) + k, :, lo:lo + HEAD_DIM] = (
                                    t[k * n:(k + 1) * n])


def _qkv_project(x, nw, w_in, cos2, sin2, rows):
    B, S, D = x.shape
    n_blocks = S // rows
    out_shapes, out_specs = [], []
    for _, dil in DILATED_GROUPS:
        for _ in range(3):
            out_shapes.append(jax.ShapeDtypeStruct((B, dil, S // dil, ATTN_WIDTH), BF16))
            out_specs.append(pl.BlockSpec((None, dil, rows // dil, ATTN_WIDTH),
                                          lambda i: (i // n_blocks, 0, i % n_blocks, 0)))
    tab_spec = pl.BlockSpec((rows, HEAD_DIM), lambda i: (i % n_blocks, 0))
    const = lambda shape: pl.BlockSpec(shape, lambda i: (0, 0), pipeline_mode=pl.Buffered(1))
    xf = x.reshape(B * S, D)
    slab_specs = [pl.BlockSpec((rows, LANES), lambda i, c=c: (i, c)) for c in range(D // LANES)]
    return pl.pallas_call(
        _qkv_kernel,
        grid=(B * n_blocks,),
        in_specs=slab_specs + [const((1, D)), const((D, w_in.shape[1])), tab_spec, tab_spec],
        out_specs=out_specs,
        out_shape=out_shapes,
        scratch_shapes=[pltpu.VMEM((2, rows, HEAD_DIM), F32)],
        compiler_params=_params("parallel"),
        name="attn_qkv",
    )(*([xf] * (D // LANES)), nw, w_in, cos2, sin2)


def _band_bias():
    qi = np.arange(ATTN_BLOCK)[:, None]
    kj = np.arange(2 * ATTN_BLOCK)[None, :]
    rel = qi - kj + ATTN_BLOCK
    band = (rel >= 0) & (rel <= ATTN_BLOCK)
    masks = np.stack([band & (kj >= ATTN_BLOCK), band])
    return jnp.asarray(np.where(masks, 0.0, NEG_BIG).astype(np.float32))


def _attn_kernel(q_ref, k_ref, v_ref, kh_ref, vh_ref, bias_ref, o_ref, lse_ref, *, n_sub):
    first_bias = jnp.where(pl.program_id(2) > 0, 1, 0)
    lane = lax.broadcasted_iota(jnp.int32, (ATTN_BLOCK, LANES), 1)
    for c in range(n_sub):
        rows = slice(c * ATTN_BLOCK, (c + 1) * ATTN_BLOCK)
        bias = bias_ref.at[first_bias if c == 0 else 1]
        lse_tile = jnp.zeros((ATTN_BLOCK, LANES), F32)
        for h in range(ATTN_HEADS):
            cols = slice(h * HEAD_DIM, (h + 1) * HEAD_DIM)
            q = q_ref[rows, cols]
            if c == 0:
                kk = jnp.concatenate([kh_ref[:, cols], k_ref[rows, cols]], axis=0)
                vv = jnp.concatenate([vh_ref[:, cols], v_ref[rows, cols]], axis=0)
            else:
                hist = slice((c - 1) * ATTN_BLOCK, (c + 1) * ATTN_BLOCK)
                kk = k_ref[hist, cols]
                vv = v_ref[hist, cols]
            s = _dot_nt(q, kk) + bias[...]
            m = jnp.max(s, axis=-1, keepdims=True)
            p = jnp.exp2(s - m)
            l = jnp.sum(p, axis=-1, keepdims=True)
            o = _dot(p.astype(BF16), vv) / l
            o_ref[rows, cols] = o
            lse_tile = jnp.where(lane == h, m + jnp.log2(l), lse_tile)
        lse_ref[rows, :] = lse_tile


def _group_attention(q, k, v, rows):
    B, dil, L, W = q.shape
    rows = min(rows, L)
    n_sub = rows // ATTN_BLOCK
    blk = lambda width: pl.BlockSpec((None, None, rows, width), lambda b, r, i: (b, r, i, 0))
    halo = pl.BlockSpec((None, None, ATTN_BLOCK, W),
                        lambda b, r, i: (b, r, jnp.maximum(i * n_sub - 1, 0), 0))
    return pl.pallas_call(
        partial(_attn_kernel, n_sub=n_sub),
        grid=(B, dil, L // rows),
        in_specs=[blk(W), blk(W), blk(W), halo, halo,
                  pl.BlockSpec((2, ATTN_BLOCK, 2 * ATTN_BLOCK), lambda b, r, i: (0, 0, 0))],
        out_specs=[blk(W), blk(LANES)],
        out_shape=[
            jax.ShapeDtypeStruct((B, dil, L, W), F32),
            jax.ShapeDtypeStruct((B, dil, L, LANES), F32),
        ],
        compiler_params=_params("parallel", "parallel", "parallel"),
        name="attn_band",
    )(q, k, v, k, v, _band_bias())


MLP_FF_CHUNK = 1024


def _project_and_mlp(mixed, x_ref, wo_ref, nw_ref, wu_ref, wd_ref, side_work=None):
    x_mid = x_ref[...] + _dot(mixed, wo_ref[...])
    h = _rmsnorm_rows(x_mid, nw_ref[...]).astype(BF16)
    acc = x_mid
    for f in range(D_FF // MLP_FF_CHUNK):
        cols = slice(f * MLP_FF_CHUNK, (f + 1) * MLP_FF_CHUNK)
        a = jnp.maximum(_dot(h, wu_ref[:, cols]), 0.0)
        acc = acc + _dot((a * a).astype(BF16), wd_ref[cols, :])
        if side_work is not None:
            side_work(f)
    return acc


def _attn_tail_kernel(*refs):
    n_groups = len(DILATED_GROUPS)
    o_refs = [refs[g * ATTN_HEADS:(g + 1) * ATTN_HEADS] for g in range(n_groups)]
    lse_refs = refs[n_groups * ATTN_HEADS:n_groups * (ATTN_HEADS + 1)]
    (x_ref, wo_ref, nw_ref, wu_ref, wd_ref, out_ref,
     merged_ref, nat_ref) = refs[n_groups * (ATTN_HEADS + 1):]
    rows = x_ref.shape[0]
    step = pl.program_id(0)
    write_slot = step % 2
    read_slot = 1 - write_slot

    @pl.when(step == 0)
    def _():
        merged_ref[1] = jnp.zeros(merged_ref.shape[1:], BF16)

    def natural(src_ref, dil, slot):
        if dil == 1:
            return src_ref[0]
        for r in range(dil):
            nat_ref[slot, pl.ds(r, rows // dil, stride=dil), :] = src_ref[r]
        return nat_ref[slot]

    mixed = merged_ref[read_slot]
    lses = [natural(lse_refs[g], dil, g) for g, (_, dil) in enumerate(DILATED_GROUPS)]
    m = jnp.maximum(jnp.maximum(lses[0], lses[1]), lses[2])
    es = [jnp.exp2(l - m) for l in lses]
    inv = 1.0 / (es[0] + es[1] + es[2])
    alphas = [e * inv for e in es]
    heads_per_chunk = ATTN_HEADS // (D_FF // MLP_FF_CHUNK)

    def merge_heads(f):
        for h in range(f * heads_per_chunk, (f + 1) * heads_per_chunk):
            o = None
            for g, (_, dil) in enumerate(DILATED_GROUPS):
                slot = n_groups + 2 * g + h % 2
                term = alphas[g][:, h:h + 1] * natural(o_refs[g][h], dil, slot)
                o = term if o is None else o + term
            merged_ref[write_slot, :, h * HEAD_DIM:(h + 1) * HEAD_DIM] = o.astype(BF16)

    out_ref[...] = _project_and_mlp(mixed, x_ref, wo_ref, nw_ref, wu_ref, wd_ref,
                                    side_work=merge_heads)


def _attn_tail(o_list, lse_list, x, w_out, nw, w_up, w_down, rows):
    B, S, D = x.shape
    W = ATTN_WIDTH
    n_blocks = S // rows
    n_groups = len(DILATED_GROUPS)

    n_tiles = B * n_blocks

    def residue_spec(dil, col):
        def index(s):
            t = jnp.minimum(s, n_tiles - 1)
            return (t // n_blocks, 0, t % n_blocks, col)
        return pl.BlockSpec((None, dil, rows // dil, LANES), index)

    o_specs = [residue_spec(dil, h) for _, dil in DILATED_GROUPS for h in range(ATTN_HEADS)]
    o_args = [o for o in o_list for _ in range(ATTN_HEADS)]
    lse_specs = [residue_spec(dil, 0) for _, dil in DILATED_GROUPS]
    row_spec = pl.BlockSpec((rows, D), lambda s: (jnp.maximum(s - 1, 0), 0))
    const = lambda shape: pl.BlockSpec(shape, lambda s: (0, 0), pipeline_mode=pl.Buffered(1))
    out = pl.pallas_call(
        _attn_tail_kernel,
        grid=(n_tiles + 1,),
        in_specs=o_specs + lse_specs + [row_spec, const((W, D)), const((1, D)),
                                        const((D, D_FF)), const((D_FF, D))],
        out_specs=row_spec,
        out_shape=jax.ShapeDtypeStruct((B * S, D), F32),
        scratch_shapes=[pltpu.VMEM((2, rows, W), BF16),
                        pltpu.VMEM((3 * n_groups, rows, LANES), F32)],
        compiler_params=_params("arbitrary"),
        name="attn_tail",
    )(*o_args, *lse_list, x.reshape(B * S, D), w_out, nw, w_up, w_down)
    return out.reshape(B, S, D)


def _gla_tail_kernel(o_ref, x_ref, wo_ref, nw_ref, wu_ref, wd_ref, nwf_ref, out_ref):
    acc = _project_and_mlp(o_ref[...], x_ref, wo_ref, nw_ref, wu_ref, wd_ref)
    out_ref[...] = _rmsnorm_rows(acc, nwf_ref[...])


def _gla_tail(o, x, w_out, nw, w_up, w_down, nw_final, rows):
    B, S, D = x.shape
    T = B * S
    row_spec = lambda width: pl.BlockSpec((rows, width), lambda i: (i, 0))
    const = lambda shape: pl.BlockSpec(shape, lambda i: (0, 0), pipeline_mode=pl.Buffered(1))
    out = pl.pallas_call(
        _gla_tail_kernel,
        grid=(T // rows,),
        in_specs=[row_spec(o.shape[-1]), row_spec(D), const((o.shape[-1], D)), const((1, D)),
                  const((D, D_FF)), const((D_FF, D)), const((1, D))],
        out_specs=row_spec(D),
        out_shape=jax.ShapeDtypeStruct((T, D), F32),
        compiler_params=_params("parallel"),
        name="gla_tail",
    )(o, x.reshape(T, D), w_out, nw, w_up, w_down, nw_final)
    return out.reshape(B, S, D)


def _gla_in_kernel(x_ref, nw_ref, w_ref, wg_ref, wgu_ref, bg_ref,
                   q_ref, k_ref, v_ref, g_ref, la_ref):
    h = _rmsnorm_rows(x_ref[...], nw_ref[...]).astype(BF16)
    q_ref[...] = _dot(h, w_ref[:, :GLA_KEY_DIM])
    k_ref[...] = _dot(h, w_ref[:, GLA_KEY_DIM:2 * GLA_KEY_DIM])
    v_lo = 2 * GLA_KEY_DIM
    v_ref[...] = _dot(h, w_ref[:, v_lo:v_lo + GLA_VAL_DIM]).astype(BF16)
    g_lo = v_lo + GLA_VAL_DIM
    g_ref[...] = _dot(h, w_ref[:, g_lo:g_lo + GLA_VAL_DIM])
    gate_lr = _dot(h, wg_ref[...]).astype(BF16)
    gk = _dot(gate_lr, wgu_ref[...]) + bg_ref[...]
    log_sig = jnp.minimum(gk, 0.0) - jnp.log(1.0 + jnp.exp(-jnp.abs(gk)))
    la_ref[...] = log_sig * (1.0 / GLA_GATE_NORMALIZER)


def _gla_in_project(x, nw, w_main, w_gate, w_gate_up, b_gate, rows):
    B, S, D = x.shape
    T = B * S
    row_spec = lambda width: pl.BlockSpec((rows, width), lambda i: (i, 0))
    const = lambda shape: pl.BlockSpec(shape, lambda i: (0, 0))
    n_main = 2 * GLA_KEY_DIM + 2 * GLA_VAL_DIM
    return pl.pallas_call(
        _gla_in_kernel,
        grid=(T // rows,),
        in_specs=[row_spec(D), const((1, D)), const((D, n_main)), const((D, LANES)),
                  const((LANES, GLA_KEY_DIM)), const((1, GLA_KEY_DIM))],
        out_specs=[row_spec(GLA_KEY_DIM), row_spec(GLA_KEY_DIM), row_spec(GLA_VAL_DIM),
                   row_spec(GLA_VAL_DIM), row_spec(GLA_KEY_DIM)],
        out_shape=[
            jax.ShapeDtypeStruct((T, GLA_KEY_DIM), F32),
            jax.ShapeDtypeStruct((T, GLA_KEY_DIM), F32),
            jax.ShapeDtypeStruct((T, GLA_VAL_DIM), BF16),
            jax.ShapeDtypeStruct((T, GLA_VAL_DIM), F32),
            jax.ShapeDtypeStruct((T, GLA_KEY_DIM), F32),
        ],
        compiler_params=_params("parallel"),
        name="gla_in",
    )(x.reshape(T, D), nw, w_main, w_gate, w_gate_up, b_gate)


def _gla_kernel(q_ref, k_ref, v_ref, g_ref, la_ref, nw_ref, o_ref, state_ref, *, n_chunks):
    @pl.when(pl.program_id(1) == 0)
    def _():
        state_ref[...] = jnp.zeros_like(state_ref)

    pair = 2 * GLA_CHUNK
    ri = lax.broadcasted_iota(jnp.int32, (pair, pair), 0)
    cj = lax.broadcasted_iota(jnp.int32, (pair, pair), 1)
    causal = ri >= cj
    cumsum_mat = (causal & ((ri >= GLA_CHUNK) == (cj >= GLA_CHUNK))).astype(BF16)
    first = lax.broadcasted_iota(jnp.int32, (pair, GLA_KEY_DIM), 0) < GLA_CHUNK
    ones = jnp.ones((pair, LANES), BF16)
    nw = nw_ref[...]
    for p in range(n_chunks // 2):
        rows = slice(p * pair, (p + 1) * pair)
        la = la_ref[rows, :]
        la_hi = la.astype(BF16)
        la_lo = (la - la_hi.astype(F32)).astype(BF16)
        b = _dot(cumsum_mat, la_hi) + _dot(cumsum_mat, la_lo)
        bl0 = b[GLA_CHUNK - 1:GLA_CHUNK, :]
        bl1 = b[pair - 1:pair, :]
        ref_b = jnp.where(first, b - bl0, b)
        q_e = q_ref[rows, :] * jnp.exp(ref_b)
        k_e = k_ref[rows, :] * jnp.exp(-ref_b)
        scale = GLA_DK ** -0.5
        q_sc = (q_e * scale).astype(BF16)
        k_sc = k_e.astype(BF16)
        q_st = (q_e * (jnp.exp(bl0) * scale)).astype(BF16)
        k_up = (k_e * jnp.exp(bl1)).astype(BF16)
        for h in range(GLA_HEADS):
            kc = slice(h * GLA_DK, (h + 1) * GLA_DK)
            vc = slice(h * GLA_DV, (h + 1) * GLA_DV)
            v = v_ref[rows, vc]
            a = jnp.where(causal, _dot_nt(q_sc[:, kc], k_sc[:, kc]), 0.0).astype(BF16)
            state = state_ref[h]
            o = _dot(jnp.concatenate([a, q_st[:, kc]], axis=1),
                     jnp.concatenate([v, state.astype(BF16)], axis=0))
            bl_t = _dot_tn(la_hi[:, kc], ones) + _dot_tn(la_lo[:, kc], ones)
            decay = jnp.exp(bl_t)
            decay = jnp.concatenate([decay] * (GLA_DV // LANES), axis=1)
            state_ref[h] = decay * state + _dot_tn(k_up[:, kc], v)
            var = jnp.mean(o * o, axis=-1, keepdims=True)
            o = o * lax.rsqrt(var + NORM_EPS) * nw
            g = g_ref[rows, vc]
            half_g = 0.5 * g
            o_ref[rows, vc] = (o * (half_g * (1.0 + jnp.tanh(half_g)))).astype(BF16)


def _gla_recurrence(q, k, v, g, la, nw, B, S, rows):
    T = B * S
    n_blocks = S // rows
    row_spec = lambda width: pl.BlockSpec((rows, width), lambda b, i: (b * n_blocks + i, 0))
    return pl.pallas_call(
        partial(_gla_kernel, n_chunks=rows // GLA_CHUNK),
        grid=(B, n_blocks),
        in_specs=[row_spec(GLA_KEY_DIM), row_spec(GLA_KEY_DIM), row_spec(GLA_VAL_DIM),
                  row_spec(GLA_VAL_DIM), row_spec(GLA_KEY_DIM),
                  pl.BlockSpec((1, GLA_DV), lambda b, i: (0, 0))],
        out_specs=row_spec(GLA_VAL_DIM),
        out_shape=jax.ShapeDtypeStruct((T, GLA_VAL_DIM), BF16),
        scratch_shapes=[pltpu.VMEM((GLA_HEADS, GLA_DK, GLA_DV), F32)],
        compiler_params=_params("parallel", "arbitrary"),
        name="gla_recurrence",
    )(q, k, v, g, la, nw)


def _rope_tables(seq_len):
    pos = np.arange(seq_len, dtype=np.float64)
    inv_freq = ROPE_THETA ** (-np.arange(0, HEAD_DIM, 2, dtype=np.float64) / HEAD_DIM)
    ang = pos[:, None] * inv_freq[None, :]
    cos, sin = np.cos(ang), np.sin(ang)
    cos2 = np.concatenate([cos, cos], axis=-1).astype(np.float32)
    sin2 = np.concatenate([-sin, sin], axis=-1).astype(np.float32)
    return jnp.asarray(cos2), jnp.asarray(sin2)


def kernel(x, norm_mix_w, norm_mlp_w, final_norm_w, attn_w_in, attn_w_out, gla_w_in,
           gla_w_gate_up, gla_b_gate, gla_norm_w, gla_w_out, mlp_w_up, mlp_w_down):
    B, S, D = x.shape
    row = lambda w: w.reshape(1, -1).astype(F32)

    cos2, sin2 = _rope_tables(S)
    qkv = _qkv_project(x, row(norm_mix_w[0]), attn_w_in[0].astype(BF16), cos2, sin2, rows=512)
    outs, lses = [], []
    for g, (window, dil) in enumerate(DILATED_GROUPS):
        assert window // dil == ATTN_BLOCK and (S // dil) % ATTN_BLOCK == 0
        q, k, v = qkv[3 * g:3 * g + 3]
        o, lse = _group_attention(q, k, v, rows=512)
        outs.append(o)
        lses.append(lse)
    x = _attn_tail(outs, lses, x, attn_w_out[0].astype(BF16), row(norm_mlp_w[0]),
                   mlp_w_up[0].astype(BF16), mlp_w_down[0].astype(BF16), rows=512)

    n_main = 2 * GLA_KEY_DIM + 2 * GLA_VAL_DIM
    w_gla = gla_w_in[0]
    w_gate = jnp.pad(w_gla[:, n_main:], ((0, 0), (0, LANES - GLA_GATE_RANK))).astype(BF16)
    w_gate_up = jnp.pad(gla_w_gate_up[0], ((0, LANES - GLA_GATE_RANK), (0, 0))).astype(BF16)
    q, k, v, g, la = _gla_in_project(x, row(norm_mix_w[1]), w_gla[:, :n_main].astype(BF16),
                                     w_gate, w_gate_up, row(gla_b_gate[0]), rows=512)
    o = _gla_recurrence(q, k, v, g, la, row(gla_norm_w[0]), B, S, rows=512)
    return _gla_tail(o, x, gla_w_out[0].astype(BF16), row(norm_mlp_w[1]),
                     mlp_w_up[1].astype(BF16), mlp_w_down[1].astype(BF16), row(final_norm_w),
                     rows=512)
```

```python
from functools import partial

import jax
import jax.numpy as jnp
import numpy as np
from jax import lax
from jax.experimental import pallas as pl
from jax.experimental.pallas import tpu as pltpu

D_MODEL = 1024
NORM_EPS = 1e-5
DILATED_GROUPS = ((128, 1), (512, 4), (2048, 16))
ATTN_HEADS = 8
HEAD_DIM = 128
ATTN_WIDTH = ATTN_HEADS * HEAD_DIM
ATTN_BLOCK = 128
ROPE_THETA = 10000.0
GLA_HEADS = 4
GLA_DK = 128
GLA_DV = 256
GLA_KEY_DIM = GLA_HEADS * GLA_DK
GLA_VAL_DIM = GLA_HEADS * GLA_DV
GLA_GATE_RANK = 16
GLA_GATE_NORMALIZER = 16.0
GLA_CHUNK = 64
D_FF = 4 * D_MODEL

LANES = 128
VMEM_LIMIT_BYTES = 56 * 1024 * 1024
NEG_BIG = -1e30

BF16 = jnp.bfloat16
F32 = jnp.float32


def _params(*semantics):
    return pltpu.CompilerParams(dimension_semantics=semantics,
                                vmem_limit_bytes=VMEM_LIMIT_BYTES)


def _rmsnorm_rows(x, w):
    var = jnp.mean(x * x, axis=-1, keepdims=True)
    return x * lax.rsqrt(var + NORM_EPS) * w


def _dot(a, b):
    return jnp.dot(a, b, preferred_element_type=F32)


def _dot_nt(a, b):
    return lax.dot_general(a, b, (((1,), (1,)), ((), ())), preferred_element_type=F32)


def _dot_tn(a, b):
    return lax.dot_general(a, b, (((0,), (0,)), ((), ())), preferred_element_type=F32)


QKV_CHUNK = 256
Q_SCALE = HEAD_DIM ** -0.5 * float(np.log2(np.e))


def _qkv_kernel(*refs):
    n_slabs = D_MODEL // LANES
    x_refs = refs[:n_slabs]
    nw_ref, w_ref, cos_ref, sin_ref = refs[n_slabs:n_slabs + 4]
    out_refs, tab_ref = refs[n_slabs + 4:-1], refs[-1]
    rows = x_refs[0].shape[0]
    n_chunks = ATTN_WIDTH // QKV_CHUNK
    row_blocks = rows // ATTN_BLOCK
    for g, (_, dil) in enumerate(DILATED_GROUPS):
        n = rows // dil

        def residue_major(src_ref):
            if dil == 1:
                return src_ref[...]
            return jnp.concatenate([src_ref[pl.ds(r, n, stride=dil), :] for r in range(dil)], axis=0)

        xg = jnp.concatenate([residue_major(x_ref) for x_ref in x_refs], axis=1)
        h = _rmsnorm_rows(xg, nw_ref[...]).astype(BF16)
        if dil == 1:
            cos_src, sin_src = cos_ref, sin_ref
        else:
            tab_ref[0] = residue_major(cos_ref)
            tab_ref[1] = residue_major(sin_ref)
            cos_src, sin_src = tab_ref.at[0], tab_ref.at[1]
        for part in range(3):
            o_ref = out_refs[3 * g + part]
            for j in range(n_chunks):
                col = (3 * g + part) * ATTN_WIDTH + j * QKV_CHUNK
                acc = _dot(h, w_ref[:, col:col + QKV_CHUNK])
                for half in range(QKV_CHUNK // HEAD_DIM):
                    lo = j * QKV_CHUNK + half * HEAD_DIM
                    for rb in range(row_blocks):
                        rs = slice(rb * ATTN_BLOCK, (rb + 1) * ATTN_BLOCK)
                        t = acc[rs, half * HEAD_DIM:(half + 1) * HEAD_DIM]
                        if part < 2:
                            t = t * cos_src[rs, :] + pltpu.roll(t, HEAD_DIM // 2, 1) * sin_src[rs, :]
                        if part == 0:
                            t = t * Q_SCALE
                        t = t.astype(BF16)
                        if n >= ATTN_BLOCK:
                            first = rb * ATTN_BLOCK
                            o_ref[first // n, pl.ds(first % n, ATTN_BLOCK), lo:lo + HEAD_DIM] = t
                        else:
                            for k in range(ATTN_BLOCK // n):
                                o_ref[rb * (ATTN_BLOCK // n) + k, :, lo:lo + HEAD_DIM] = (
                                    t[k * n:(k + 1) * n])


def _qkv_project(x, nw, w_in, cos2, sin2, rows):
    B, S, D = x.shape
    n_blocks = S // rows
    out_shapes, out_specs = [], []
    for _, dil in DILATED_GROUPS:
        for _ in range(3):
            out_shapes.append(jax.ShapeDtypeStruct((B, dil, S // dil, ATTN_WIDTH), BF16))
            out_specs.append(pl.BlockSpec((None, dil, rows // dil, ATTN_WIDTH),
                                          lambda i: (i // n_blocks, 0, i % n_blocks, 0)))
    tab_spec = pl.BlockSpec((rows, HEAD_DIM), lambda i: (i % n_blocks, 0))
    const = lambda shape: pl.BlockSpec(shape, lambda i: (0, 0), pipeline_mode=pl.Buffered(1))
    xf = x.reshape(B * S, D)
    slab_specs = [pl.BlockSpec((rows, LANES), lambda i, c=c: (i, c)) for c in range(D // LANES)]
    return pl.pallas_call(
        _qkv_kernel,
        grid=(B * n_blocks,),
        in_specs=slab_specs + [const((1, D)), const((D, w_in.shape[1])), tab_spec, tab_spec],
        out_specs=out_specs,
        out_shape=out_shapes,
        scratch_shapes=[pltpu.VMEM((2, rows, HEAD_DIM), F32)],
        compiler_params=_params("parallel"),
        name="attn_qkv",
    )(*([xf] * (D // LANES)), nw, w_in, cos2, sin2)


def _band_bias():
    qi = np.arange(ATTN_BLOCK)[:, None]
    kj = np.arange(2 * ATTN_BLOCK)[None, :]
    rel = qi - kj + ATTN_BLOCK
    band = (rel >= 0) & (rel <= ATTN_BLOCK)
    masks = np.stack([band & (kj >= ATTN_BLOCK), band])
    return jnp.asarray(np.where(masks, 0.0, NEG_BIG).astype(np.float32))


def _attn_kernel(q_ref, k_ref, v_ref, kh_ref, vh_ref, bias_ref, o_ref, lse_ref, *, n_sub):
    first_bias = jnp.where(pl.program_id(2) > 0, 1, 0)
    lane = lax.broadcasted_iota(jnp.int32, (ATTN_BLOCK, LANES), 1)
    for c in range(n_sub):
        rows = slice(c * ATTN_BLOCK, (c + 1) * ATTN_BLOCK)
        bias = bias_ref.at[first_bias if c == 0 else 1]
        lse_tile = jnp.zeros((ATTN_BLOCK, LANES), F32)
        for h in range(ATTN_HEADS):
            cols = slice(h * HEAD_DIM, (h + 1) * HEAD_DIM)
            q = q_ref[rows, cols]
            if c == 0:
                kk = jnp.concatenate([kh_ref[:, cols], k_ref[rows, cols]], axis=0)
                vv = jnp.concatenate([vh_ref[:, cols], v_ref[rows, cols]], axis=0)
            else:
                hist = slice((c - 1) * ATTN_BLOCK, (c + 1) * ATTN_BLOCK)
                kk = k_ref[hist, cols]
                vv = v_ref[hist, cols]
            s = _dot_nt(q, kk) + bias[...]
            m = jnp.max(s, axis=-1, keepdims=True)
            p = jnp.exp2(s - m)
            l = jnp.sum(p, axis=-1, keepdims=True)
            o = _dot(p.astype(BF16), vv) / l
            o_ref[rows, cols] = o
            lse_tile = jnp.where(lane == h, m + jnp.log2(l), lse_tile)
        lse_ref[rows, :] = lse_tile


def _group_attention(q, k, v, rows):
    B, dil, L, W = q.shape
    rows = min(rows, L)
    n_sub = rows // ATTN_BLOCK
    blk = lambda width: pl.BlockSpec((None, None, rows, width), lambda b, r, i: (b, r, i, 0))
    halo = pl.BlockSpec((None, None, ATTN_BLOCK, W),
                        lambda b, r, i: (b, r, jnp.maximum(i * n_sub - 1, 0), 0))
    return pl.pallas_call(
        partial(_attn_kernel, n_sub=n_sub),
        grid=(B, dil, L // rows),
        in_specs=[blk(W), blk(W), blk(W), halo, halo,
                  pl.BlockSpec((2, ATTN_BLOCK, 2 * ATTN_BLOCK), lambda b, r, i: (0, 0, 0))],
        out_specs=[blk(W), blk(LANES)],
        out_shape=[
            jax.ShapeDtypeStruct((B, dil, L, W), F32),
            jax.ShapeDtypeStruct((B, dil, L, LANES), F32),
        ],
        compiler_params=_params("parallel", "parallel", "parallel"),
        name="attn_band",
    )(q, k, v, k, v, _band_bias())


MLP_FF_CHUNK = 1024


def _project_and_mlp(mixed, x_ref, wo_ref, nw_ref, wu_ref, wd_ref, side_work=None):
    x_mid = x_ref[...] + _dot(mixed, wo_ref[...])
    h = _rmsnorm_rows(x_mid, nw_ref[...]).astype(BF16)
    acc = x_mid
    for f in range(D_FF // MLP_FF_CHUNK):
        cols = slice(f * MLP_FF_CHUNK, (f + 1) * MLP_FF_CHUNK)
        a = jnp.maximum(_dot(h, wu_ref[:, cols]), 0.0)
        acc = acc + _dot((a * a).astype(BF16), wd_ref[cols, :])
        if side_work is not None:
            side_work(f)
    return acc


def _attn_tail_kernel(*refs):
    n_groups = len(DILATED_GROUPS)
    o_refs = [refs[g * ATTN_HEADS:(g + 1) * ATTN_HEADS] for g in range(n_groups)]
    lse_refs = refs[n_groups * ATTN_HEADS:n_groups * (ATTN_HEADS + 1)]
    (x_ref, wo_ref, nw_ref, wu_ref, wd_ref, out_ref,
     merged_ref, nat_ref) = refs[n_groups * (ATTN_HEADS + 1):]
    rows = x_ref.shape[0]
    step = pl.program_id(0)
    write_slot = step % 2
    read_slot = 1 - write_slot

    @pl.when(step == 0)
    def _():
        merged_ref[1] = jnp.zeros(merged_ref.shape[1:], BF16)

    def natural(src_ref, dil, slot):
        if dil == 1:
            return src_ref[0]
        for r in range(dil):
            nat_ref[slot, pl.ds(r, rows // dil, stride=dil), :] = src_ref[r]
        return nat_ref[slot]

    mixed = merged_ref[read_slot]
    lses = [natural(lse_refs[g], dil, g) for g, (_, dil) in enumerate(DILATED_GROUPS)]
    m = jnp.maximum(jnp.maximum(lses[0], lses[1]), lses[2])
    es = [jnp.exp2(l - m) for l in lses]
    inv = 1.0 / (es[0] + es[1] + es[2])
    alphas = [e * inv for e in es]
    heads_per_chunk = ATTN_HEADS // (D_FF // MLP_FF_CHUNK)

    def merge_heads(f):
        for h in range(f * heads_per_chunk, (f + 1) * heads_per_chunk):
            o = None
            for g, (_, dil) in enumerate(DILATED_GROUPS):
                slot = n_groups + 2 * g + h % 2
                term = alphas[g][:, h:h + 1] * natural(o_refs[g][h], dil, slot)
                o = term if o is None else o + term
            merged_ref[write_slot, :, h * HEAD_DIM:(h + 1) * HEAD_DIM] = o.astype(BF16)

    out_ref[...] = _project_and_mlp(mixed, x_ref, wo_ref, nw_ref, wu_ref, wd_ref,
                                    side_work=merge_heads)


def _attn_tail(o_list, lse_list, x, w_out, nw, w_up, w_down, rows):
    B, S, D = x.shape
    W = ATTN_WIDTH
    n_blocks = S // rows
    n_groups = len(DILATED_GROUPS)

    n_tiles = B * n_blocks

    def residue_spec(dil, col):
        def index(s):
            t = jnp.minimum(s, n_tiles - 1)
            return (t // n_blocks, 0, t % n_blocks, col)
        return pl.BlockSpec((None, dil, rows // dil, LANES), index)

    o_specs = [residue_spec(dil, h) for _, dil in DILATED_GROUPS for h in range(ATTN_HEADS)]
    o_args = [o for o in o_list for _ in range(ATTN_HEADS)]
    lse_specs = [residue_spec(dil, 0) for _, dil in DILATED_GROUPS]
    row_spec = pl.BlockSpec((rows, D), lambda s: (jnp.maximum(s - 1, 0), 0))
    const = lambda shape: pl.BlockSpec(shape, lambda s: (0, 0), pipeline_mode=pl.Buffered(1))
    out = pl.pallas_call(
        _attn_tail_kernel,
        grid=(n_tiles + 1,),
        in_specs=o_specs + lse_specs + [row_spec, const((W, D)), const((1, D)),
                                        const((D, D_FF)), const((D_FF, D))],
        out_specs=row_spec,
        out_shape=jax.ShapeDtypeStruct((B * S, D), F32),
        scratch_shapes=[pltpu.VMEM((2, rows, W), BF16),
                        pltpu.VMEM((3 * n_groups, rows, LANES), F32)],
        compiler_params=_params("arbitrary"),
        name="attn_tail",
    )(*o_args, *lse_list, x.reshape(B * S, D), w_out, nw, w_up, w_down)
    return out.reshape(B, S, D)


def _gla_tail_kernel(o_ref, x_ref, wo_ref, nw_ref, wu_ref, wd_ref, nwf_ref, out_ref):
    acc = _project_and_mlp(o_ref[...], x_ref, wo_ref, nw_ref, wu_ref, wd_ref)
    out_ref[...] = _rmsnorm_rows(acc, nwf_ref[...])


def _gla_tail(o, x, w_out, nw, w_up, w_down, nw_final, rows):
    B, S, D = x.shape
    T = B * S
    row_spec = lambda width: pl.BlockSpec((rows, width), lambda i: (i, 0))
    const = lambda shape: pl.BlockSpec(shape, lambda i: (0, 0), pipeline_mode=pl.Buffered(1))
    out = pl.pallas_call(
        _gla_tail_kernel,
        grid=(T // rows,),
        in_specs=[row_spec(o.shape[-1]), row_spec(D), const((o.shape[-1], D)), const((1, D)),
                  const((D, D_FF)), const((D_FF, D)), const((1, D))],
        out_specs=row_spec(D),
        out_shape=jax.ShapeDtypeStruct((T, D), F32),
        compiler_params=_params("parallel"),
        name="gla_tail",
    )(o, x.reshape(T, D), w_out, nw, w_up, w_down, nw_final)
    return out.reshape(B, S, D)


GLA_PROJ_CHUNK = 256


def _gla_projection_steps(x_ref, nw_ref, w_ref, wg_ref, wgu_ref, bg_ref, dst):
    q_ref, k_ref, v_ref, g_ref, la_ref = dst
    cell = {}

    def hidden():
        if "h" not in cell:
            cell["h"] = _rmsnorm_rows(x_ref[...], nw_ref[...]).astype(BF16)
        return cell["h"]

    def gate():
        gate_lr = _dot(hidden(), wg_ref[...]).astype(BF16)
        gk = _dot(gate_lr, wgu_ref[...]) + bg_ref[...]
        log_sig = jnp.minimum(gk, 0.0) - jnp.log(1.0 + jnp.exp(-jnp.abs(gk)))
        la_ref[...] = log_sig * (1.0 / GLA_GATE_NORMALIZER)

    def plain(dst_ref, col, lo):
        def run():
            acc = _dot(hidden(), w_ref[:, col:col + GLA_PROJ_CHUNK])
            dst_ref[:, lo:lo + GLA_PROJ_CHUNK] = acc.astype(dst_ref.dtype)
        return run

    steps = [gate]
    col = 0
    for dst_ref in (q_ref, k_ref, v_ref, g_ref):
        for lo in range(0, dst_ref.shape[1], GLA_PROJ_CHUNK):
            steps.append(plain(dst_ref, col, lo))
            col += GLA_PROJ_CHUNK
    return steps


def _gla_recurrence_steps(src, nw_ref, o_ref, state_ref):
    q_ref, k_ref, v_ref, g_ref, la_ref = src
    n_chunks = q_ref.shape[0] // GLA_CHUNK
    pair = 2 * GLA_CHUNK
    ri = lax.broadcasted_iota(jnp.int32, (pair, pair), 0)
    cj = lax.broadcasted_iota(jnp.int32, (pair, pair), 1)
    causal = ri >= cj
    cumsum_mat = (causal & ((ri >= GLA_CHUNK) == (cj >= GLA_CHUNK))).astype(BF16)
    first = lax.broadcasted_iota(jnp.int32, (pair, GLA_KEY_DIM), 0) < GLA_CHUNK
    ones = jnp.ones((pair, LANES), BF16)
    nw = nw_ref[...]

    def prepare(p):
        rows = slice(p * pair, (p + 1) * pair)
        la = la_ref[rows, :]
        la_hi = la.astype(BF16)
        la_lo = (la - la_hi.astype(F32)).astype(BF16)
        b = _dot(cumsum_mat, la_hi) + _dot(cumsum_mat, la_lo)
        bl0 = b[GLA_CHUNK - 1:GLA_CHUNK, :]
        bl1 = b[pair - 1:pair, :]
        ref_b = jnp.where(first, b - bl0, b)
        q_e = q_ref[rows, :] * jnp.exp(ref_b)
        k_e = k_ref[rows, :] * jnp.exp(-ref_b)
        scale = GLA_DK ** -0.5
        return dict(
            la_hi=la_hi, la_lo=la_lo,
            q_sc=(q_e * scale).astype(BF16),
            k_sc=k_e.astype(BF16),
            q_st=(q_e * (jnp.exp(bl0) * scale)).astype(BF16),
            k_up=(k_e * jnp.exp(bl1)).astype(BF16),
        )

    def head_step(p, h, shared):
        def run():
            if not shared:
                shared.update(prepare(p))
            rows = slice(p * pair, (p + 1) * pair)
            kc = slice(h * GLA_DK, (h + 1) * GLA_DK)
            vc = slice(h * GLA_DV, (h + 1) * GLA_DV)
            v = v_ref[rows, vc]
            a = jnp.where(causal, _dot_nt(shared["q_sc"][:, kc], shared["k_sc"][:, kc]), 0.0)
            state = state_ref[h]
            o = _dot(jnp.concatenate([a.astype(BF16), shared["q_st"][:, kc]], axis=1),
                     jnp.concatenate([v, state.astype(BF16)], axis=0))
            bl_t = (_dot_tn(shared["la_hi"][:, kc], ones)
                    + _dot_tn(shared["la_lo"][:, kc], ones))
            decay = jnp.exp(bl_t)
            decay = jnp.concatenate([decay] * (GLA_DV // LANES), axis=1)
            state_ref[h] = decay * state + _dot_tn(shared["k_up"][:, kc], v)
            var = jnp.mean(o * o, axis=-1, keepdims=True)
            o = o * lax.rsqrt(var + NORM_EPS) * nw
            half_g = 0.5 * g_ref[rows, vc]
            o_ref[rows, vc] = (o * (half_g * (1.0 + jnp.tanh(half_g)))).astype(BF16)
        return run

    steps = []
    for p in range(n_chunks // 2):
        shared = {}
        steps.extend(head_step(p, h, shared) for h in range(GLA_HEADS))
    return steps


def _gla_mixer_kernel(x_ref, nwm_ref, w_ref, wg_ref, wgu_ref, bg_ref, nwh_ref, o_ref,
                      state_ref, *slots, tiles_per_seq):
    n_fields = len(slots) // 2
    slot_a, slot_b = slots[:n_fields], slots[n_fields:]
    step = pl.program_id(0)

    @pl.when(step == 0)
    def _():
        for ref in slot_b:
            ref[...] = jnp.zeros(ref.shape, ref.dtype)

    @pl.when((step == 0) | ((step - 1) % tiles_per_seq == 0))
    def _():
        state_ref[...] = jnp.zeros_like(state_ref)

    def body(read, write):
        project = _gla_projection_steps(x_ref, nwm_ref, w_ref, wg_ref, wgu_ref, bg_ref, write)
        recur = _gla_recurrence_steps(read, nwh_ref, o_ref, state_ref)
        project[0]()
        pending = project[1:]
        for run in recur:
            run()
            if pending:
                pending.pop(0)()
        for run in pending:
            run()

    @pl.when(step % 2 == 0)
    def _():
        body(read=slot_b, write=slot_a)

    @pl.when(step % 2 == 1)
    def _():
        body(read=slot_a, write=slot_b)


def _gla_mixer(x, nw_mix, w_main, w_gate, w_gate_up, b_gate, nw_head, rows):
    B, S, D = x.shape
    T = B * S
    n_tiles = T // rows
    const = lambda shape: pl.BlockSpec(shape, lambda s: (0, 0), pipeline_mode=pl.Buffered(1))
    slot = [pltpu.VMEM((rows, GLA_KEY_DIM), F32), pltpu.VMEM((rows, GLA_KEY_DIM), F32),
            pltpu.VMEM((rows, GLA_VAL_DIM), BF16), pltpu.VMEM((rows, GLA_VAL_DIM), F32),
            pltpu.VMEM((rows, GLA_KEY_DIM), F32)]
    return pl.pallas_call(
        partial(_gla_mixer_kernel, tiles_per_seq=S // rows),
        grid=(n_tiles + 1,),
        in_specs=[pl.BlockSpec((rows, D), lambda s: (jnp.minimum(s, n_tiles - 1), 0)),
                  const((1, D)), const(w_main.shape), const((D, LANES)),
                  const((LANES, GLA_KEY_DIM)), const((1, GLA_KEY_DIM)), const((1, GLA_DV))],
        out_specs=pl.BlockSpec((rows, GLA_VAL_DIM), lambda s: (jnp.maximum(s - 1, 0), 0)),
        out_shape=jax.ShapeDtypeStruct((T, GLA_VAL_DIM), BF16),
        scratch_shapes=[pltpu.VMEM((GLA_HEADS, GLA_DK, GLA_DV), F32)] + slot + slot,
        compiler_params=_params("arbitrary"),
        name="gla_mixer",
    )(x.reshape(T, D), nw_mix, w_main, w_gate, w_gate_up, b_gate, nw_head)


def _rope_tables(seq_len):
    pos = np.arange(seq_len, dtype=np.float64)
    inv_freq = ROPE_THETA ** (-np.arange(0, HEAD_DIM, 2, dtype=np.float64) / HEAD_DIM)
    ang = pos[:, None] * inv_freq[None, :]
    cos, sin = np.cos(ang), np.sin(ang)
    cos2 = np.concatenate([cos, cos], axis=-1).astype(np.float32)
    sin2 = np.concatenate([-sin, sin], axis=-1).astype(np.float32)
    return jnp.asarray(cos2), jnp.asarray(sin2)


def kernel(x, norm_mix_w, norm_mlp_w, final_norm_w, attn_w_in, attn_w_out, gla_w_in,
           gla_w_gate_up, gla_b_gate, gla_norm_w, gla_w_out, mlp_w_up, mlp_w_down):
    B, S, D = x.shape
    row = lambda w: w.reshape(1, -1).astype(F32)

    cos2, sin2 = _rope_tables(S)
    qkv = _qkv_project(x, row(norm_mix_w[0]), attn_w_in[0].astype(BF16), cos2, sin2, rows=512)
    outs, lses = [], []
    for g, (window, dil) in enumerate(DILATED_GROUPS):
        assert window // dil == ATTN_BLOCK and (S // dil) % ATTN_BLOCK == 0
        q, k, v = qkv[3 * g:3 * g + 3]
        o, lse = _group_attention(q, k, v, rows=512)
        outs.append(o)
        lses.append(lse)
    x = _attn_tail(outs, lses, x, attn_w_out[0].astype(BF16), row(norm_mlp_w[0]),
                   mlp_w_up[0].astype(BF16), mlp_w_down[0].astype(BF16), rows=512)

    n_main = 2 * GLA_KEY_DIM + 2 * GLA_VAL_DIM
    w_gla = gla_w_in[0]
    w_gate = jnp.pad(w_gla[:, n_main:], ((0, 0), (0, LANES - GLA_GATE_RANK))).astype(BF16)
    w_gate_up = jnp.pad(gla_w_gate_up[0], ((0, LANES - GLA_GATE_RANK), (0, 0))).astype(BF16)
    o = _gla_mixer(x, row(norm_mix_w[1]), w_gla[:, :n_main].astype(BF16), w_gate, w_gate_up,
                   row(gla_b_gate[0]), row(gla_norm_w[0]), rows=512)
    return _gla_tail(o, x, gla_w_out[0].astype(BF16), row(norm_mlp_w[1]),
                     mlp_w_up[1].astype(BF16), mlp_w_down[1].astype(BF16), row(final_norm_w),
                     rows=512)
```

```python
from functools import partial

import jax
import jax.numpy as jnp
import numpy as np
from jax import lax
from jax.experimental import pallas as pl
from jax.experimental.pallas import tpu as pltpu

D_MODEL = 1024
NORM_EPS = 1e-5
DILATED_GROUPS = ((128, 1), (512, 4), (2048, 16))
ATTN_HEADS = 8
HEAD_DIM = 128
ATTN_WIDTH = ATTN_HEADS * HEAD_DIM
ATTN_BLOCK = 128
ROPE_THETA = 10000.0
GLA_HEADS = 4
GLA_DK = 128
GLA_DV = 256
GLA_KEY_DIM = GLA_HEADS * GLA_DK
GLA_VAL_DIM = GLA_HEADS * GLA_DV
GLA_GATE_RANK = 16
GLA_GATE_NORMALIZER = 16.0
GLA_CHUNK = 64
D_FF = 4 * D_MODEL

LANES = 128
VMEM_LIMIT_BYTES = 56 * 1024 * 1024
NEG_BIG = -1e30

BF16 = jnp.bfloat16
F32 = jnp.float32


def _params(*semantics):
    return pltpu.CompilerParams(dimension_semantics=semantics,
                                vmem_limit_bytes=VMEM_LIMIT_BYTES)


def _rmsnorm_rows(x, w):
    var = jnp.mean(x * x, axis=-1, keepdims=True)
    return x * lax.rsqrt(var + NORM_EPS) * w


def _dot(a, b):
    return jnp.dot(a, b, preferred_element_type=F32)


def _dot_nt(a, b):
    return lax.dot_general(a, b, (((1,), (1,)), ((), ())), preferred_element_type=F32)


def _dot_tn(a, b):
    return lax.dot_general(a, b, (((0,), (0,)), ((), ())), preferred_element_type=F32)


def _cast_blocks(src_refs, dst_refs):
    for src_ref, dst_ref in zip(src_refs, dst_refs):
        dst_ref[...] = src_ref[:, :dst_ref.shape[1]].astype(BF16)


def _rider_specs(riders, n_steps):
    in_specs, out_specs, out_shapes = [], [], []
    for w, layer, n_keep in riders:
        _, k, n = w.shape
        assert k % n_steps == 0 and (k // n_steps) % 16 == 0
        blk = k // n_steps
        index = lambda s, layer=layer: (layer, jnp.minimum(s, n_steps - 1), 0)
        in_specs.append(pl.BlockSpec((None, blk, n), index))
        out_specs.append(pl.BlockSpec((blk, n_keep), lambda s: (jnp.minimum(s, n_steps - 1), 0)))
        out_shapes.append(jax.ShapeDtypeStruct((k, n_keep), BF16))
    return in_specs, out_specs, out_shapes


QKV_CHUNK = 256
Q_SCALE = HEAD_DIM ** -0.5 * float(np.log2(np.e))


def _qkv_kernel(*refs, n_riders):
    n_slabs = D_MODEL // LANES
    n_out = 3 * len(DILATED_GROUPS)
    x_refs = refs[:n_slabs]
    nw_ref, w_ref, cos_ref, sin_ref = refs[n_slabs:n_slabs + 4]
    rider_in = refs[n_slabs + 4:n_slabs + 4 + n_riders]
    out_refs = refs[n_slabs + 4 + n_riders:n_slabs + 4 + n_riders + n_out]
    rider_out = refs[n_slabs + 4 + n_riders + n_out:-1]
    tab_ref = refs[-1]
    rows = x_refs[0].shape[0]
    n_chunks = ATTN_WIDTH // QKV_CHUNK
    row_blocks = rows // ATTN_BLOCK
    _cast_blocks(rider_in, rider_out)

    def residue_major(src_ref, dil):
        if dil == 1:
            return src_ref[...]
        n = rows // dil
        return jnp.concatenate([src_ref[pl.ds(r, n, stride=dil), :] for r in range(dil)], axis=0)

    for g, (_, dil) in enumerate(DILATED_GROUPS):
        n = rows // dil
        xg = jnp.concatenate([residue_major(x_ref, dil) for x_ref in x_refs], axis=1)
        h = _rmsnorm_rows(xg, nw_ref[...]).astype(BF16)
        if dil == 1:
            cos_src, sin_src = cos_ref, sin_ref
        else:
            tab_ref[0] = residue_major(cos_ref, dil)
            tab_ref[1] = residue_major(sin_ref, dil)
            cos_src, sin_src = tab_ref.at[0], tab_ref.at[1]
        for part in range(3):
            o_ref = out_refs[3 * g + part]
            for j in range(n_chunks):
                col = (3 * g + part) * ATTN_WIDTH + j * QKV_CHUNK
                acc = _dot(h, w_ref[:, col:col + QKV_CHUNK])
                for half in range(QKV_CHUNK // HEAD_DIM):
                    lo = j * QKV_CHUNK + half * HEAD_DIM
                    for rb in range(row_blocks):
                        rs = slice(rb * ATTN_BLOCK, (rb + 1) * ATTN_BLOCK)
                        t = acc[rs, half * HEAD_DIM:(half + 1) * HEAD_DIM]
                        if part < 2:
                            t = t * cos_src[rs, :] + pltpu.roll(t, HEAD_DIM // 2, 1) * sin_src[rs, :]
                        if part == 0:
                            t = t * Q_SCALE
                        t = t.astype(BF16)
                        if n >= ATTN_BLOCK:
                            first = rb * ATTN_BLOCK
                            o_ref[first // n, pl.ds(first % n, ATTN_BLOCK), lo:lo + HEAD_DIM] = t
                        else:
                            for k in range(ATTN_BLOCK // n):
                                o_ref[rb * (ATTN_BLOCK // n) + k, :, lo:lo + HEAD_DIM] = (
                                    t[k * n:(k + 1) * n])


def _qkv_project(x, nw, w_in, cos2, sin2, riders, rows):
    B, S, D = x.shape
    n_blocks = S // rows
    out_shapes, out_specs = [], []
    for _, dil in DILATED_GROUPS:
        for _ in range(3):
            out_shapes.append(jax.ShapeDtypeStruct((B, dil, S // dil, ATTN_WIDTH), BF16))
            out_specs.append(pl.BlockSpec((None, dil, rows // dil, ATTN_WIDTH),
                                          lambda i: (i // n_blocks, 0, i % n_blocks, 0)))
    rider_in, rider_out, rider_shapes = _rider_specs(riders, B * n_blocks)
    tab_spec = pl.BlockSpec((rows, HEAD_DIM), lambda i: (i % n_blocks, 0))
    const = lambda shape: pl.BlockSpec(shape, lambda i: (0, 0), pipeline_mode=pl.Buffered(1))
    xf = x.reshape(B * S, D)
    slab_specs = [pl.BlockSpec((rows, LANES), lambda i, c=c: (i, c)) for c in range(D // LANES)]
    outs = pl.pallas_call(
        partial(_qkv_kernel, n_riders=len(riders)),
        grid=(B * n_blocks,),
        in_specs=slab_specs + [const((1, D)), const((D, w_in.shape[1])), tab_spec, tab_spec]
        + rider_in,
        out_specs=out_specs + rider_out,
        out_shape=out_shapes + rider_shapes,
        scratch_shapes=[pltpu.VMEM((2, rows, HEAD_DIM), F32)],
        compiler_params=_params("parallel"),
        name="attn_qkv",
    )(*([xf] * (D // LANES)), nw, w_in, cos2, sin2, *[w for w, _, _ in riders])
    return outs[:len(out_shapes)], outs[len(out_shapes):]


def _band_bias():
    qi = np.arange(ATTN_BLOCK)[:, None]
    kj = np.arange(2 * ATTN_BLOCK)[None, :]
    rel = qi - kj + ATTN_BLOCK
    band = (rel >= 0) & (rel <= ATTN_BLOCK)
    masks = np.stack([band & (kj >= ATTN_BLOCK), band])
    return jnp.asarray(np.where(masks, 0.0, NEG_BIG).astype(np.float32))


def _attn_kernel(q_ref, k_ref, v_ref, kh_ref, vh_ref, bias_ref, o_ref, lse_ref, *, n_sub):
    first_bias = jnp.where(pl.program_id(2) > 0, 1, 0)
    lane = lax.broadcasted_iota(jnp.int32, (ATTN_BLOCK, LANES), 1)
    for c in range(n_sub):
        rows = slice(c * ATTN_BLOCK, (c + 1) * ATTN_BLOCK)
        bias = bias_ref.at[first_bias if c == 0 else 1]
        lse_tile = jnp.zeros((ATTN_BLOCK, LANES), F32)
        for h in range(ATTN_HEADS):
            cols = slice(h * HEAD_DIM, (h + 1) * HEAD_DIM)
            q = q_ref[rows, cols]
            if c == 0:
                kk = jnp.concatenate([kh_ref[:, cols], k_ref[rows, cols]], axis=0)
                vv = jnp.concatenate([vh_ref[:, cols], v_ref[rows, cols]], axis=0)
            else:
                hist = slice((c - 1) * ATTN_BLOCK, (c + 1) * ATTN_BLOCK)
                kk = k_ref[hist, cols]
                vv = v_ref[hist, cols]
            s = _dot_nt(q, kk) + bias[...]
            m = jnp.max(s, axis=-1, keepdims=True)
            p = jnp.exp2(s - m)
            l = jnp.sum(p, axis=-1, keepdims=True)
            o = _dot(p.astype(BF16), vv) / l
            o_ref[rows, cols] = o.astype(BF16)
            lse_tile = jnp.where(lane == h, m + jnp.log2(l), lse_tile)
        lse_ref[rows, :] = lse_tile


def _group_attention(q, k, v, rows):
    B, dil, L, W = q.shape
    rows = min(rows, L)
    n_sub = rows // ATTN_BLOCK
    blk = lambda width: pl.BlockSpec((None, None, rows, width), lambda b, r, i: (b, r, i, 0))
    halo = pl.BlockSpec((None, None, ATTN_BLOCK, W),
                        lambda b, r, i: (b, r, jnp.maximum(i * n_sub - 1, 0), 0))
    return pl.pallas_call(
        partial(_attn_kernel, n_sub=n_sub),
        grid=(B, dil, L // rows),
        in_specs=[blk(W), blk(W), blk(W), halo, halo,
                  pl.BlockSpec((2, ATTN_BLOCK, 2 * ATTN_BLOCK), lambda b, r, i: (0, 0, 0))],
        out_specs=[blk(W), blk(LANES)],
        out_shape=[
            jax.ShapeDtypeStruct((B, dil, L, W), BF16),
            jax.ShapeDtypeStruct((B, dil, L, LANES), F32),
        ],
        compiler_params=_params("parallel", "parallel", "parallel"),
        name="attn_band",
    )(q, k, v, k, v, _band_bias())


MLP_FF_CHUNK = 1024


def _project_and_mlp(mixed, x_ref, wo_ref, nw_ref, wu_ref, wd_ref, side_work=None):
    x_mid = x_ref[...] + _dot(mixed, wo_ref[...])
    h = _rmsnorm_rows(x_mid, nw_ref[...]).astype(BF16)
    acc = x_mid
    for f in range(D_FF // MLP_FF_CHUNK):
        cols = slice(f * MLP_FF_CHUNK, (f + 1) * MLP_FF_CHUNK)
        a = jnp.maximum(_dot(h, wu_ref[:, cols]), 0.0)
        acc = acc + _dot((a * a).astype(BF16), wd_ref[cols, :])
        if side_work is not None:
            side_work(f)
    return acc


def _attn_tail_kernel(*refs, n_riders):
    n_groups = len(DILATED_GROUPS)
    o_refs = [refs[g * ATTN_HEADS:(g + 1) * ATTN_HEADS] for g in range(n_groups)]
    lse_refs = refs[n_groups * ATTN_HEADS:n_groups * (ATTN_HEADS + 1)]
    rest = refs[n_groups * (ATTN_HEADS + 1):]
    x_ref, wo_ref, nw_ref, wu_ref, wd_ref = rest[:5]
    rider_in = rest[5:5 + n_riders]
    out_ref = rest[5 + n_riders]
    rider_out = rest[6 + n_riders:6 + 2 * n_riders]
    merged_ref, nat_ref = rest[6 + 2 * n_riders:]
    rows = x_ref.shape[0]
    _cast_blocks(rider_in, rider_out)
    step = pl.program_id(0)
    write_slot = step % 2
    read_slot = 1 - write_slot

    @pl.when(step == 0)
    def _():
        merged_ref[1] = jnp.zeros(merged_ref.shape[1:], BF16)

    def natural(src_ref, dil, slot):
        if dil == 1:
            return src_ref[0].astype(F32)
        for r in range(dil):
            nat_ref[slot, pl.ds(r, rows // dil, stride=dil), :] = src_ref[r].astype(F32)
        return nat_ref[slot]

    mixed = merged_ref[read_slot]
    heads_per_chunk = ATTN_HEADS // (D_FF // MLP_FF_CHUNK)
    alphas = []

    def merge_heads(f):
        if not alphas:
            lses = [natural(lse_refs[g], dil, g) for g, (_, dil) in enumerate(DILATED_GROUPS)]
            m = jnp.maximum(jnp.maximum(lses[0], lses[1]), lses[2])
            es = [jnp.exp2(l - m) for l in lses]
            inv = 1.0 / (es[0] + es[1] + es[2])
            alphas.extend(e * inv for e in es)
        for h in range(f * heads_per_chunk, (f + 1) * heads_per_chunk):
            o = None
            for g, (_, dil) in enumerate(DILATED_GROUPS):
                slot = n_groups + 2 * g + h % 2
                term = alphas[g][:, h:h + 1] * natural(o_refs[g][h], dil, slot)
                o = term if o is None else o + term
            merged_ref[write_slot, :, h * HEAD_DIM:(h + 1) * HEAD_DIM] = o.astype(BF16)

    out_ref[...] = _project_and_mlp(mixed, x_ref, wo_ref, nw_ref, wu_ref, wd_ref,
                                    side_work=merge_heads)


def _attn_tail(o_list, lse_list, x, w_out, nw, w_up, w_down, riders, rows):
    B, S, D = x.shape
    W = ATTN_WIDTH
    n_blocks = S // rows
    n_groups = len(DILATED_GROUPS)
    n_tiles = B * n_blocks

    def residue_spec(dil, col):
        def index(s):
            t = jnp.minimum(s, n_tiles - 1)
            return (t // n_blocks, 0, t % n_blocks, col)
        return pl.BlockSpec((None, dil, rows // dil, LANES), index)

    o_specs = [residue_spec(dil, h) for _, dil in DILATED_GROUPS for h in range(ATTN_HEADS)]
    o_args = [o for o in o_list for _ in range(ATTN_HEADS)]
    lse_specs = [residue_spec(dil, 0) for _, dil in DILATED_GROUPS]
    rider_in, rider_out, rider_shapes = _rider_specs(riders, n_tiles)
    row_spec = pl.BlockSpec((rows, D), lambda s: (jnp.maximum(s - 1, 0), 0))
    const = lambda shape: pl.BlockSpec(shape, lambda s: (0, 0), pipeline_mode=pl.Buffered(1))
    outs = pl.pallas_call(
        partial(_attn_tail_kernel, n_riders=len(riders)),
        grid=(n_tiles + 1,),
        in_specs=o_specs + lse_specs + [row_spec, const((W, D)), const((1, D)),
                                        const((D, D_FF)), const((D_FF, D))] + rider_in,
        out_specs=[row_spec] + rider_out,
        out_shape=[jax.ShapeDtypeStruct((B * S, D), F32)] + rider_shapes,
        scratch_shapes=[pltpu.VMEM((2, rows, W), BF16),
                        pltpu.VMEM((3 * n_groups, rows, LANES), F32)],
        compiler_params=_params("arbitrary"),
        name="attn_tail",
    )(*o_args, *lse_list, x.reshape(B * S, D), w_out, nw, w_up, w_down,
      *[w for w, _, _ in riders])
    return outs[0].reshape(B, S, D), outs[1:]


def _gla_tail_kernel(o_ref, x_ref, wo_ref, nw_ref, wu_ref, wd_ref, nwf_ref, out_ref):
    acc = _project_and_mlp(o_ref[...], x_ref, wo_ref, nw_ref, wu_ref, wd_ref)
    out_ref[...] = _rmsnorm_rows(acc, nwf_ref[...])


def _gla_tail(o, x, w_out, nw, w_up, w_down, nw_final, rows):
    B, S, D = x.shape
    T = B * S
    row_spec = lambda width: pl.BlockSpec((rows, width), lambda i: (i, 0))
    const = lambda shape: pl.BlockSpec(shape, lambda i: (0, 0), pipeline_mode=pl.Buffered(1))
    out = pl.pallas_call(
        _gla_tail_kernel,
        grid=(T // rows,),
        in_specs=[row_spec(o.shape[-1]), row_spec(D), const((o.shape[-1], D)), const((1, D)),
                  const((D, D_FF)), const((D_FF, D)), const((1, D))],
        out_specs=row_spec(D),
        out_shape=jax.ShapeDtypeStruct((T, D), F32),
        compiler_params=_params("parallel"),
        name="gla_tail",
    )(o, x.reshape(T, D), w_out, nw, w_up, w_down, nw_final)
    return out.reshape(B, S, D)


def _gla_in_kernel(x_ref, nw_ref, w_ref, wg_ref, wgu_ref, bg_ref,
                   q_ref, k_ref, v_ref, g_ref, la_ref):
    h = _rmsnorm_rows(x_ref[...], nw_ref[...]).astype(BF16)
    q_ref[...] = _dot(h, w_ref[:, :GLA_KEY_DIM])
    k_ref[...] = _dot(h, w_ref[:, GLA_KEY_DIM:2 * GLA_KEY_DIM])
    v_lo = 2 * GLA_KEY_DIM
    v_ref[...] = _dot(h, w_ref[:, v_lo:v_lo + GLA_VAL_DIM]).astype(BF16)
    g_lo = v_lo + GLA_VAL_DIM
    g_ref[...] = _dot(h, w_ref[:, g_lo:g_lo + GLA_VAL_DIM])
    gate_lr = _dot(h, wg_ref[...]).astype(BF16)
    gk = _dot(gate_lr, wgu_ref[...]) + bg_ref[...]
    log_sig = jnp.minimum(gk, 0.0) - jnp.log(1.0 + jnp.exp(-jnp.abs(gk)))
    la_ref[...] = log_sig * (1.0 / GLA_GATE_NORMALIZER)


def _gla_in_project(x, nw, w_main, w_gate, w_gate_up, b_gate, rows):
    B, S, D = x.shape
    T = B * S
    row_spec = lambda width: pl.BlockSpec((rows, width), lambda i: (i, 0))
    const = lambda shape: pl.BlockSpec(shape, lambda i: (0, 0))
    n_main = 2 * GLA_KEY_DIM + 2 * GLA_VAL_DIM
    return pl.pallas_call(
        _gla_in_kernel,
        grid=(T // rows,),
        in_specs=[row_spec(D), const((1, D)), const((D, n_main)), const((D, LANES)),
                  const((LANES, GLA_KEY_DIM)), const((1, GLA_KEY_DIM))],
        out_specs=[row_spec(GLA_KEY_DIM), row_spec(GLA_KEY_DIM), row_spec(GLA_VAL_DIM),
                   row_spec(GLA_VAL_DIM), row_spec(GLA_KEY_DIM)],
        out_shape=[
            jax.ShapeDtypeStruct((T, GLA_KEY_DIM), F32),
            jax.ShapeDtypeStruct((T, GLA_KEY_DIM), F32),
            jax.ShapeDtypeStruct((T, GLA_VAL_DIM), BF16),
            jax.ShapeDtypeStruct((T, GLA_VAL_DIM), F32),
            jax.ShapeDtypeStruct((T, GLA_KEY_DIM), F32),
        ],
        compiler_params=_params("parallel"),
        name="gla_in",
    )(x.reshape(T, D), nw, w_main, w_gate, w_gate_up, b_gate)


def _gla_kernel(q_ref, k_ref, v_ref, g_ref, la_ref, nw_ref, o_ref, state_ref, *, n_chunks):
    @pl.when(pl.program_id(1) == 0)
    def _():
        state_ref[...] = jnp.zeros_like(state_ref)

    pair = 2 * GLA_CHUNK
    ri = lax.broadcasted_iota(jnp.int32, (pair, pair), 0)
    cj = lax.broadcasted_iota(jnp.int32, (pair, pair), 1)
    causal = ri >= cj
    cumsum_mat = (causal & ((ri >= GLA_CHUNK) == (cj >= GLA_CHUNK))).astype(BF16)
    first = lax.broadcasted_iota(jnp.int32, (pair, GLA_KEY_DIM), 0) < GLA_CHUNK
    ones = jnp.ones((pair, LANES), BF16)
    nw = nw_ref[...]
    for p in range(n_chunks // 2):
        rows = slice(p * pair, (p + 1) * pair)
        la = la_ref[rows, :]
        la_hi = la.astype(BF16)
        la_lo = (la - la_hi.astype(F32)).astype(BF16)
        b = _dot(cumsum_mat, la_hi) + _dot(cumsum_mat, la_lo)
        bl0 = b[GLA_CHUNK - 1:GLA_CHUNK, :]
        bl1 = b[pair - 1:pair, :]
        ref_b = jnp.where(first, b - bl0, b)
        q_e = q_ref[rows, :] * jnp.exp(ref_b)
        k_e = k_ref[rows, :] * jnp.exp(-ref_b)
        scale = GLA_DK ** -0.5
        q_sc = (q_e * scale).astype(BF16)
        k_sc = k_e.astype(BF16)
        q_st = (q_e * (jnp.exp(bl0) * scale)).astype(BF16)
        k_up = (k_e * jnp.exp(bl1)).astype(BF16)
        for h in range(GLA_HEADS):
            kc = slice(h * GLA_DK, (h + 1) * GLA_DK)
            vc = slice(h * GLA_DV, (h + 1) * GLA_DV)
            v = v_ref[rows, vc]
            a = jnp.where(causal, _dot_nt(q_sc[:, kc], k_sc[:, kc]), 0.0).astype(BF16)
            state = state_ref[h]
            o = _dot(jnp.concatenate([a, q_st[:, kc]], axis=1),
                     jnp.concatenate([v, state.astype(BF16)], axis=0))
            bl_t = _dot_tn(la_hi[:, kc], ones) + _dot_tn(la_lo[:, kc], ones)
            decay = jnp.exp(bl_t)
            decay = jnp.concatenate([decay] * (GLA_DV // LANES), axis=1)
            state_ref[h] = decay * state + _dot_tn(k_up[:, kc], v)
            var = jnp.mean(o * o, axis=-1, keepdims=True)
            o = o * lax.rsqrt(var + NORM_EPS) * nw
            g = g_ref[rows, vc]
            half_g = 0.5 * g
            o_ref[rows, vc] = (o * (half_g * (1.0 + jnp.tanh(half_g)))).astype(BF16)


def _gla_recurrence(q, k, v, g, la, nw, B, S, rows):
    T = B * S
    n_blocks = S // rows
    row_spec = lambda width: pl.BlockSpec((rows, width), lambda b, i: (b * n_blocks + i, 0))
    return pl.pallas_call(
        partial(_gla_kernel, n_chunks=rows // GLA_CHUNK),
        grid=(B, n_blocks),
        in_specs=[row_spec(GLA_KEY_DIM), row_spec(GLA_KEY_DIM), row_spec(GLA_VAL_DIM),
                  row_spec(GLA_VAL_DIM), row_spec(GLA_KEY_DIM),
                  pl.BlockSpec((1, GLA_DV), lambda b, i: (0, 0))],
        out_specs=row_spec(GLA_VAL_DIM),
        out_shape=jax.ShapeDtypeStruct((T, GLA_VAL_DIM), BF16),
        scratch_shapes=[pltpu.VMEM((GLA_HEADS, GLA_DK, GLA_DV), F32)],
        compiler_params=_params("parallel", "arbitrary"),
        name="gla_recurrence",
    )(q, k, v, g, la, nw)


def _rope_tables(seq_len):
    pos = np.arange(seq_len, dtype=np.float64)
    inv_freq = ROPE_THETA ** (-np.arange(0, HEAD_DIM, 2, dtype=np.float64) / HEAD_DIM)
    ang = pos[:, None] * inv_freq[None, :]
    cos, sin = np.cos(ang), np.sin(ang)
    cos2 = np.concatenate([cos, cos], axis=-1).astype(np.float32)
    sin2 = np.concatenate([-sin, sin], axis=-1).astype(np.float32)
    return jnp.asarray(cos2), jnp.asarray(sin2)


def kernel(x, norm_mix_w, norm_mlp_w, final_norm_w, attn_w_in, attn_w_out, gla_w_in,
           gla_w_gate_up, gla_b_gate, gla_norm_w, gla_w_out, mlp_w_up, mlp_w_down):
    B, S, D = x.shape
    row = lambda w: w.reshape(1, -1).astype(F32)

    cos2, sin2 = _rope_tables(S)
    qkv, (w_out0, w_up0, w_down0) = _qkv_project(
        x, row(norm_mix_w[0]), attn_w_in[0].astype(BF16), cos2, sin2,
        riders=[(attn_w_out, 0, D), (mlp_w_up, 0, D_FF), (mlp_w_down, 0, D)], rows=512)
    outs, lses = [], []
    for g, (window, dil) in enumerate(DILATED_GROUPS):
        assert window // dil == ATTN_BLOCK and (S // dil) % ATTN_BLOCK == 0
        q, k, v = qkv[3 * g:3 * g + 3]
        o, lse = _group_attention(q, k, v, rows=512)
        outs.append(o)
        lses.append(lse)
    n_main = 2 * GLA_KEY_DIM + 2 * GLA_VAL_DIM
    x, (w_gla, w_out1, w_up1, w_down1) = _attn_tail(
        outs, lses, x, w_out0, row(norm_mlp_w[0]), w_up0, w_down0,
        riders=[(gla_w_in, 0, n_main), (gla_w_out, 0, D), (mlp_w_up, 1, D_FF), (mlp_w_down, 1, D)],
        rows=512)

    w_gate = jnp.pad(gla_w_in[0][:, n_main:], ((0, 0), (0, LANES - GLA_GATE_RANK))).astype(BF16)
    w_gate_up = jnp.pad(gla_w_gate_up[0], ((0, LANES - GLA_GATE_RANK), (0, 0))).astype(BF16)
    q, k, v, g, la = _gla_in_project(x, row(norm_mix_w[1]), w_gla, w_gate, w_gate_up,
                                     row(gla_b_gate[0]), rows=512)
    o = _gla_recurrence(q, k, v, g, la, row(gla_norm_w[0]), B, S, rows=512)
    return _gla_tail(o, x, w_out1, row(norm_mlp_w[1]), w_up1, w_down1, row(final_norm_w), rows=512)
```

```python
from functools import partial

import jax
import jax.numpy as jnp
import numpy as np
from jax import lax
from jax.experimental import pallas as pl
from jax.experimental.pallas import tpu as pltpu

D_MODEL = 1024
NORM_EPS = 1e-5
DILATED_GROUPS = ((128, 1), (512, 4), (2048, 16))
ATTN_HEADS = 8
HEAD_DIM = 128
ATTN_WIDTH = ATTN_HEADS * HEAD_DIM
ATTN_BLOCK = 128
ROPE_THETA = 10000.0
GLA_HEADS = 4
GLA_DK = 128
GLA_DV = 256
GLA_KEY_DIM = GLA_HEADS * GLA_DK
GLA_VAL_DIM = GLA_HEADS * GLA_DV
GLA_GATE_RANK = 16
GLA_GATE_NORMALIZER = 16.0
GLA_CHUNK = 64
D_FF = 4 * D_MODEL

LANES = 128
VMEM_LIMIT_BYTES = 56 * 1024 * 1024
NEG_BIG = -1e30

BF16 = jnp.bfloat16
F32 = jnp.float32


def _params(*semantics):
    return pltpu.CompilerParams(dimension_semantics=semantics,
                                vmem_limit_bytes=VMEM_LIMIT_BYTES)


def _rmsnorm_rows(x, w):
    var = jnp.mean(x * x, axis=-1, keepdims=True)
    return x * lax.rsqrt(var + NORM_EPS) * w


def _dot(a, b):
    return jnp.dot(a, b, preferred_element_type=F32)


def _dot_nt(a, b):
    return lax.dot_general(a, b, (((1,), (1,)), ((), ())), preferred_element_type=F32)


def _dot_tn(a, b):
    return lax.dot_general(a, b, (((0,), (0,)), ((), ())), preferred_element_type=F32)


def _cast_blocks(src_refs, dst_refs):
    for src_ref, dst_ref in zip(src_refs, dst_refs):
        dst_ref[...] = src_ref[:, :dst_ref.shape[1]].astype(BF16)


def _rider_specs(riders, n_steps):
    in_specs, out_specs, out_shapes = [], [], []
    for w, layer, n_keep in riders:
        _, k, n = w.shape
        assert k % n_steps == 0 and (k // n_steps) % 16 == 0
        blk = k // n_steps
        index = lambda s, layer=layer: (layer, jnp.minimum(s, n_steps - 1), 0)
        in_specs.append(pl.BlockSpec((None, blk, n), index))
        out_specs.append(pl.BlockSpec((blk, n_keep), lambda s: (jnp.minimum(s, n_steps - 1), 0)))
        out_shapes.append(jax.ShapeDtypeStruct((k, n_keep), BF16))
    return in_specs, out_specs, out_shapes


QKV_CHUNK = 256
Q_SCALE = HEAD_DIM ** -0.5 * float(np.log2(np.e))


def _qkv_kernel(*refs, n_riders):
    n_slabs = D_MODEL // LANES
    n_out = 3 * len(DILATED_GROUPS)
    x_refs = refs[:n_slabs]
    nw_ref, w_ref, cos_ref, sin_ref = refs[n_slabs:n_slabs + 4]
    rider_in = refs[n_slabs + 4:n_slabs + 4 + n_riders]
    out_refs = refs[n_slabs + 4 + n_riders:n_slabs + 4 + n_riders + n_out]
    rider_out = refs[n_slabs + 4 + n_riders + n_out:-1]
    tab_ref = refs[-1]
    rows = x_refs[0].shape[0]
    n_chunks = ATTN_WIDTH // QKV_CHUNK
    row_blocks = rows // ATTN_BLOCK
    _cast_blocks(rider_in, rider_out)

    def residue_major(src_ref, dil):
        if dil == 1:
            return src_ref[...]
        n = rows // dil
        return jnp.concatenate([src_ref[pl.ds(r, n, stride=dil), :] for r in range(dil)], axis=0)

    for g, (_, dil) in enumerate(DILATED_GROUPS):
        n = rows // dil
        xg = jnp.concatenate([residue_major(x_ref, dil) for x_ref in x_refs], axis=1)
        h = _rmsnorm_rows(xg, nw_ref[...]).astype(BF16)
        if dil == 1:
            cos_src, sin_src = cos_ref, sin_ref
        else:
            tab_ref[0] = residue_major(cos_ref, dil)
            tab_ref[1] = residue_major(sin_ref, dil)
            cos_src, sin_src = tab_ref.at[0], tab_ref.at[1]
        for part in range(3):
            o_ref = out_refs[3 * g + part]
            for j in range(n_chunks):
                col = (3 * g + part) * ATTN_WIDTH + j * QKV_CHUNK
                acc = _dot(h, w_ref[:, col:col + QKV_CHUNK])
                for half in range(QKV_CHUNK // HEAD_DIM):
                    lo = j * QKV_CHUNK + half * HEAD_DIM
                    for rb in range(row_blocks):
                        rs = slice(rb * ATTN_BLOCK, (rb + 1) * ATTN_BLOCK)
                        t = acc[rs, half * HEAD_DIM:(half + 1) * HEAD_DIM]
                        if part < 2:
                            t = t * cos_src[rs, :] + pltpu.roll(t, HEAD_DIM // 2, 1) * sin_src[rs, :]
                        if part == 0:
                            t = t * Q_SCALE
                        t = t.astype(BF16)
                        if n >= ATTN_BLOCK:
                            first = rb * ATTN_BLOCK
                            o_ref[first // n, pl.ds(first % n, ATTN_BLOCK), lo:lo + HEAD_DIM] = t
                        else:
                            for k in range(ATTN_BLOCK // n):
                                o_ref[rb * (ATTN_BLOCK // n) + k, :, lo:lo + HEAD_DIM] = (
                                    t[k * n:(k + 1) * n])


def _qkv_project(x, nw, w_in, cos2, sin2, riders, rows):
    B, S, D = x.shape
    n_blocks = S // rows
    out_shapes, out_specs = [], []
    for _, dil in DILATED_GROUPS:
        for _ in range(3):
            out_shapes.append(jax.ShapeDtypeStruct((B, dil, S // dil, ATTN_WIDTH), BF16))
            out_specs.append(pl.BlockSpec((None, dil, rows // dil, ATTN_WIDTH),
                                          lambda i: (i // n_blocks, 0, i % n_blocks, 0)))
    rider_in, rider_out, rider_shapes = _rider_specs(riders, B * n_blocks)
    tab_spec = pl.BlockSpec((rows, HEAD_DIM), lambda i: (i % n_blocks, 0))
    const = lambda shape: pl.BlockSpec(shape, lambda i: (0, 0), pipeline_mode=pl.Buffered(1))
    xf = x.reshape(B * S, D)
    slab_specs = [pl.BlockSpec((rows, LANES), lambda i, c=c: (i, c)) for c in range(D // LANES)]
    outs = pl.pallas_call(
        partial(_qkv_kernel, n_riders=len(riders)),
        grid=(B * n_blocks,),
        in_specs=slab_specs + [const((1, D)), const((D, w_in.shape[1])), tab_spec, tab_spec]
        + rider_in,
        out_specs=out_specs + rider_out,
        out_shape=out_shapes + rider_shapes,
        scratch_shapes=[pltpu.VMEM((2, rows, HEAD_DIM), F32)],
        compiler_params=_params("parallel"),
        name="attn_qkv",
    )(*([xf] * (D // LANES)), nw, w_in, cos2, sin2, *[w for w, _, _ in riders])
    return outs[:len(out_shapes)], outs[len(out_shapes):]


def _band_bias():
    qi = np.arange(ATTN_BLOCK)[:, None]
    kj = np.arange(2 * ATTN_BLOCK)[None, :]
    rel = qi - kj + ATTN_BLOCK
    band = (rel >= 0) & (rel <= ATTN_BLOCK)
    masks = np.stack([band & (kj >= ATTN_BLOCK), band])
    return jnp.asarray(np.where(masks, 0.0, NEG_BIG).astype(np.float32))


def _attn_kernel(q_ref, k_ref, v_ref, kh_ref, vh_ref, bias_ref, o_ref, lse_ref, *, n_sub):
    first_bias = jnp.where(pl.program_id(2) > 0, 1, 0)
    lane = lax.broadcasted_iota(jnp.int32, (ATTN_BLOCK, LANES), 1)
    for c in range(n_sub):
        rows = slice(c * ATTN_BLOCK, (c + 1) * ATTN_BLOCK)
        bias = bias_ref.at[first_bias if c == 0 else 1]
        lse_tile = jnp.zeros((ATTN_BLOCK, LANES), F32)
        for h in range(ATTN_HEADS):
            cols = slice(h * HEAD_DIM, (h + 1) * HEAD_DIM)
            q = q_ref[rows, cols]
            if c == 0:
                kk = jnp.concatenate([kh_ref[:, cols], k_ref[rows, cols]], axis=0)
                vv = jnp.concatenate([vh_ref[:, cols], v_ref[rows, cols]], axis=0)
            else:
                hist = slice((c - 1) * ATTN_BLOCK, (c + 1) * ATTN_BLOCK)
                kk = k_ref[hist, cols]
                vv = v_ref[hist, cols]
            s = _dot_nt(q, kk) + bias[...]
            m = jnp.max(s, axis=-1, keepdims=True)
            p = jnp.exp2(s - m)
            l = jnp.sum(p, axis=-1, keepdims=True)
            o = _dot(p.astype(BF16), vv) / l
            o_ref[rows, cols] = o.astype(BF16)
            lse_tile = jnp.where(lane == h, m + jnp.log2(l), lse_tile)
        lse_ref[rows, :] = lse_tile


def _group_attention(q, k, v, rows):
    B, dil, L, W = q.shape
    rows = min(rows, L)
    n_sub = rows // ATTN_BLOCK
    blk = lambda width: pl.BlockSpec((None, None, rows, width), lambda b, r, i: (b, r, i, 0))
    halo = pl.BlockSpec((None, None, ATTN_BLOCK, W),
                        lambda b, r, i: (b, r, jnp.maximum(i * n_sub - 1, 0), 0))
    return pl.pallas_call(
        partial(_attn_kernel, n_sub=n_sub),
        grid=(B, dil, L // rows),
        in_specs=[blk(W), blk(W), blk(W), halo, halo,
                  pl.BlockSpec((2, ATTN_BLOCK, 2 * ATTN_BLOCK), lambda b, r, i: (0, 0, 0))],
        out_specs=[blk(W), blk(LANES)],
        out_shape=[
            jax.ShapeDtypeStruct((B, dil, L, W), BF16),
            jax.ShapeDtypeStruct((B, dil, L, LANES), F32),
        ],
        compiler_params=_params("parallel", "parallel", "parallel"),
        name="attn_band",
    )(q, k, v, k, v, _band_bias())


MLP_FF_CHUNK = 1024


def _project_and_mlp(mixed, x_ref, wo_ref, nw_ref, wu_ref, wd_ref, side_work=None):
    x_mid = x_ref[...] + _dot(mixed, wo_ref[...])
    h = _rmsnorm_rows(x_mid, nw_ref[...]).astype(BF16)
    acc = x_mid
    for f in range(D_FF // MLP_FF_CHUNK):
        cols = slice(f * MLP_FF_CHUNK, (f + 1) * MLP_FF_CHUNK)
        a = jnp.maximum(_dot(h, wu_ref[:, cols]), 0.0)
        acc = acc + _dot((a * a).astype(BF16), wd_ref[cols, :])
        if side_work is not None:
            side_work(f)
    return acc


def _attn_tail_kernel(*refs, n_riders):
    n_groups = len(DILATED_GROUPS)
    o_refs = [refs[g * ATTN_HEADS:(g + 1) * ATTN_HEADS] for g in range(n_groups)]
    lse_refs = refs[n_groups * ATTN_HEADS:n_groups * (ATTN_HEADS + 1)]
    rest = refs[n_groups * (ATTN_HEADS + 1):]
    x_ref, wo_ref, nw_ref, wu_ref, wd_ref = rest[:5]
    rider_in = rest[5:5 + n_riders]
    out_ref = rest[5 + n_riders]
    rider_out = rest[6 + n_riders:6 + 2 * n_riders]
    merged_ref, nat_ref = rest[6 + 2 * n_riders:]
    rows = x_ref.shape[0]
    _cast_blocks(rider_in, rider_out)
    step = pl.program_id(0)
    write_slot = step % 2
    read_slot = 1 - write_slot

    @pl.when(step == 0)
    def _():
        merged_ref[1] = jnp.zeros(merged_ref.shape[1:], BF16)

    def natural(src_ref, dil, slot):
        if dil == 1:
            return src_ref[0].astype(F32)
        for r in range(dil):
            nat_ref[slot, pl.ds(r, rows // dil, stride=dil), :] = src_ref[r].astype(F32)
        return nat_ref[slot]

    mixed = merged_ref[read_slot]
    heads_per_chunk = ATTN_HEADS // (D_FF // MLP_FF_CHUNK)
    alphas = []

    def merge_heads(f):
        if not alphas:
            lses = [natural(lse_refs[g], dil, g) for g, (_, dil) in enumerate(DILATED_GROUPS)]
            m = jnp.maximum(jnp.maximum(lses[0], lses[1]), lses[2])
            es = [jnp.exp2(l - m) for l in lses]
            inv = 1.0 / (es[0] + es[1] + es[2])
            alphas.extend(e * inv for e in es)
        for h in range(f * heads_per_chunk, (f + 1) * heads_per_chunk):
            o = None
            for g, (_, dil) in enumerate(DILATED_GROUPS):
                slot = n_groups + 2 * g + h % 2
                term = alphas[g][:, h:h + 1] * natural(o_refs[g][h], dil, slot)
                o = term if o is None else o + term
            merged_ref[write_slot, :, h * HEAD_DIM:(h + 1) * HEAD_DIM] = o.astype(BF16)

    out_ref[...] = _project_and_mlp(mixed, x_ref, wo_ref, nw_ref, wu_ref, wd_ref,
                                    side_work=merge_heads)


def _attn_tail(o_list, lse_list, x, w_out, nw, w_up, w_down, riders, rows):
    B, S, D = x.shape
    W = ATTN_WIDTH
    n_blocks = S // rows
    n_groups = len(DILATED_GROUPS)
    n_tiles = B * n_blocks

    def residue_spec(dil, col):
        def index(s):
            t = jnp.minimum(s, n_tiles - 1)
            return (t // n_blocks, 0, t % n_blocks, col)
        return pl.BlockSpec((None, dil, rows // dil, LANES), index)

    o_specs = [residue_spec(dil, h) for _, dil in DILATED_GROUPS for h in range(ATTN_HEADS)]
    o_args = [o for o in o_list for _ in range(ATTN_HEADS)]
    lse_specs = [residue_spec(dil, 0) for _, dil in DILATED_GROUPS]
    rider_in, rider_out, rider_shapes = _rider_specs(riders, n_tiles)
    row_spec = pl.BlockSpec((rows, D), lambda s: (jnp.maximum(s - 1, 0), 0))
    const = lambda shape: pl.BlockSpec(shape, lambda s: (0, 0), pipeline_mode=pl.Buffered(1))
    outs = pl.pallas_call(
        partial(_attn_tail_kernel, n_riders=len(riders)),
        grid=(n_tiles + 1,),
        in_specs=o_specs + lse_specs + [row_spec, const((W, D)), const((1, D)),
                                        const((D, D_FF)), const((D_FF, D))] + rider_in,
        out_specs=[row_spec] + rider_out,
        out_shape=[jax.ShapeDtypeStruct((B * S, D), F32)] + rider_shapes,
        scratch_shapes=[pltpu.VMEM((2, rows, W), BF16),
                        pltpu.VMEM((3 * n_groups, rows, LANES), F32)],
        compiler_params=_params("arbitrary"),
        name="attn_tail",
    )(*o_args, *lse_list, x.reshape(B * S, D), w_out, nw, w_up, w_down,
      *[w for w, _, _ in riders])
    return outs[0].reshape(B, S, D), outs[1:]


def _gla_tail_kernel(o_ref, x_ref, wo_ref, nw_ref, wu_ref, wd_ref, nwf_ref, out_ref):
    acc = _project_and_mlp(o_ref[...], x_ref, wo_ref, nw_ref, wu_ref, wd_ref)
    out_ref[...] = _rmsnorm_rows(acc, nwf_ref[...])


def _gla_tail(o, x, w_out, nw, w_up, w_down, nw_final, rows):
    B, S, D = x.shape
    T = B * S
    row_spec = lambda width: pl.BlockSpec((rows, width), lambda i: (i, 0))
    const = lambda shape: pl.BlockSpec(shape, lambda i: (0, 0), pipeline_mode=pl.Buffered(1))
    out = pl.pallas_call(
        _gla_tail_kernel,
        grid=(T // rows,),
        in_specs=[row_spec(o.shape[-1]), row_spec(D), const((o.shape[-1], D)), const((1, D)),
                  const((D, D_FF)), const((D_FF, D)), const((1, D))],
        out_specs=row_spec(D),
        out_shape=jax.ShapeDtypeStruct((T, D), F32),
        compiler_params=_params("parallel"),
        name="gla_tail",
    )(o, x.reshape(T, D), w_out, nw, w_up, w_down, nw_final)
    return out.reshape(B, S, D)


def _gla_in_kernel(x_ref, nw_ref, w_ref, wg_ref, wgu_ref, bg_ref,
                   q_ref, k_ref, v_ref, g_ref, la_ref):
    h = _rmsnorm_rows(x_ref[...], nw_ref[...]).astype(BF16)
    q_ref[...] = _dot(h, w_ref[:, :GLA_KEY_DIM])
    k_ref[...] = _dot(h, w_ref[:, GLA_KEY_DIM:2 * GLA_KEY_DIM])
    v_lo = 2 * GLA_KEY_DIM
    v_ref[...] = _dot(h, w_ref[:, v_lo:v_lo + GLA_VAL_DIM]).astype(BF16)
    g_lo = v_lo + GLA_VAL_DIM
    g_ref[...] = _dot(h, w_ref[:, g_lo:g_lo + GLA_VAL_DIM])
    gate_lr = _dot(h, wg_ref[...].astype(BF16)).astype(BF16)
    gk = _dot(gate_lr, wgu_ref[...].astype(BF16)) + bg_ref[...]
    log_sig = jnp.minimum(gk, 0.0) - jnp.log(1.0 + jnp.exp(-jnp.abs(gk)))
    la_ref[...] = log_sig * (1.0 / GLA_GATE_NORMALIZER)


def _gla_in_project(x, nw, w_main, w_gate, w_gate_up, b_gate, rows):
    B, S, D = x.shape
    T = B * S
    row_spec = lambda width: pl.BlockSpec((rows, width), lambda i: (i, 0))
    const = lambda shape: pl.BlockSpec(shape, lambda i: (0, 0))
    n_main = 2 * GLA_KEY_DIM + 2 * GLA_VAL_DIM
    return pl.pallas_call(
        _gla_in_kernel,
        grid=(T // rows,),
        in_specs=[row_spec(D), const((1, D)), const((D, n_main)), const((D, GLA_GATE_RANK)),
                  const((GLA_GATE_RANK, GLA_KEY_DIM)), const((1, GLA_KEY_DIM))],
        out_specs=[row_spec(GLA_KEY_DIM), row_spec(GLA_KEY_DIM), row_spec(GLA_VAL_DIM),
                   row_spec(GLA_VAL_DIM), row_spec(GLA_KEY_DIM)],
        out_shape=[
            jax.ShapeDtypeStruct((T, GLA_KEY_DIM), F32),
            jax.ShapeDtypeStruct((T, GLA_KEY_DIM), F32),
            jax.ShapeDtypeStruct((T, GLA_VAL_DIM), BF16),
            jax.ShapeDtypeStruct((T, GLA_VAL_DIM), F32),
            jax.ShapeDtypeStruct((T, GLA_KEY_DIM), F32),
        ],
        compiler_params=_params("parallel"),
        name="gla_in",
    )(x.reshape(T, D), nw, w_main, w_gate, w_gate_up, b_gate)


def _gla_kernel(q_ref, k_ref, v_ref, g_ref, la_ref, nw_ref, o_ref, state_ref, *, n_chunks):
    @pl.when(pl.program_id(1) == 0)
    def _():
        state_ref[...] = jnp.zeros_like(state_ref)

    pair = 2 * GLA_CHUNK
    ri = lax.broadcasted_iota(jnp.int32, (pair, pair), 0)
    cj = lax.broadcasted_iota(jnp.int32, (pair, pair), 1)
    causal = ri >= cj
    cumsum_mat = (causal & ((ri >= GLA_CHUNK) == (cj >= GLA_CHUNK))).astype(BF16)
    first = lax.broadcasted_iota(jnp.int32, (pair, GLA_KEY_DIM), 0) < GLA_CHUNK
    ones = jnp.ones((pair, LANES), BF16)
    nw = nw_ref[...]
    for p in range(n_chunks // 2):
        rows = slice(p * pair, (p + 1) * pair)
        la = la_ref[rows, :]
        la_hi = la.astype(BF16)
        la_lo = (la - la_hi.astype(F32)).astype(BF16)
        b = _dot(cumsum_mat, la_hi) + _dot(cumsum_mat, la_lo)
        bl0 = b[GLA_CHUNK - 1:GLA_CHUNK, :]
        bl1 = b[pair - 1:pair, :]
        ref_b = jnp.where(first, b - bl0, b)
        q_e = q_ref[rows, :] * jnp.exp(ref_b)
        k_e = k_ref[rows, :] * jnp.exp(-ref_b)
        scale = GLA_DK ** -0.5
        q_sc = (q_e * scale).astype(BF16)
        k_sc = k_e.astype(BF16)
        q_st = (q_e * (jnp.exp(bl0) * scale)).astype(BF16)
        k_up = (k_e * jnp.exp(bl1)).astype(BF16)
        for h in range(GLA_HEADS):
            kc = slice(h * GLA_DK, (h + 1) * GLA_DK)
            vc = slice(h * GLA_DV, (h + 1) * GLA_DV)
            v = v_ref[rows, vc]
            a = jnp.where(causal, _dot_nt(q_sc[:, kc], k_sc[:, kc]), 0.0).astype(BF16)
            state = state_ref[h]
            o = _dot(jnp.concatenate([a, q_st[:, kc]], axis=1),
                     jnp.concatenate([v, state.astype(BF16)], axis=0))
            bl_t = _dot_tn(la_hi[:, kc], ones) + _dot_tn(la_lo[:, kc], ones)
            decay = jnp.exp(bl_t)
            decay = jnp.concatenate([decay] * (GLA_DV // LANES), axis=1)
            state_ref[h] = decay * state + _dot_tn(k_up[:, kc], v)
            var = jnp.mean(o * o, axis=-1, keepdims=True)
            o = o * lax.rsqrt(var + NORM_EPS) * nw
            g = g_ref[rows, vc]
            half_g = 0.5 * g
            o_ref[rows, vc] = (o * (half_g * (1.0 + jnp.tanh(half_g)))).astype(BF16)


def _gla_recurrence(q, k, v, g, la, nw, B, S, rows):
    T = B * S
    n_blocks = S // rows
    row_spec = lambda width: pl.BlockSpec((rows, width), lambda b, i: (b * n_blocks + i, 0))
    return pl.pallas_call(
        partial(_gla_kernel, n_chunks=rows // GLA_CHUNK),
        grid=(B, n_blocks),
        in_specs=[row_spec(GLA_KEY_DIM), row_spec(GLA_KEY_DIM), row_spec(GLA_VAL_DIM),
                  row_spec(GLA_VAL_DIM), row_spec(GLA_KEY_DIM),
                  pl.BlockSpec((1, GLA_DV), lambda b, i: (0, 0))],
        out_specs=row_spec(GLA_VAL_DIM),
        out_shape=jax.ShapeDtypeStruct((T, GLA_VAL_DIM), BF16),
        scratch_shapes=[pltpu.VMEM((GLA_HEADS, GLA_DK, GLA_DV), F32)],
        compiler_params=_params("parallel", "arbitrary"),
        name="gla_recurrence",
    )(q, k, v, g, la, nw)


def _rope_tables(seq_len):
    pos = np.arange(seq_len, dtype=np.float64)
    inv_freq = ROPE_THETA ** (-np.arange(0, HEAD_DIM, 2, dtype=np.float64) / HEAD_DIM)
    ang = pos[:, None] * inv_freq[None, :]
    cos, sin = np.cos(ang), np.sin(ang)
    cos2 = np.concatenate([cos, cos], axis=-1).astype(np.float32)
    sin2 = np.concatenate([-sin, sin], axis=-1).astype(np.float32)
    return jnp.asarray(cos2), jnp.asarray(sin2)


def kernel(x, norm_mix_w, norm_mlp_w, final_norm_w, attn_w_in, attn_w_out, gla_w_in,
           gla_w_gate_up, gla_b_gate, gla_norm_w, gla_w_out, mlp_w_up, mlp_w_down):
    B, S, D = x.shape
    row = lambda w: w.reshape(1, -1).astype(F32)

    cos2, sin2 = _rope_tables(S)
    qkv, (w_out0, w_up0, w_down0) = _qkv_project(
        x, row(norm_mix_w[0]), attn_w_in[0].astype(BF16), cos2, sin2,
        riders=[(attn_w_out, 0, D), (mlp_w_up, 0, D_FF), (mlp_w_down, 0, D)], rows=512)
    outs, lses = [], []
    for g, (window, dil) in enumerate(DILATED_GROUPS):
        assert window // dil == ATTN_BLOCK and (S // dil) % ATTN_BLOCK == 0
        q, k, v = qkv[3 * g:3 * g + 3]
        o, lse = _group_attention(q, k, v, rows=512)
        outs.append(o)
        lses.append(lse)
    n_main = 2 * GLA_KEY_DIM + 2 * GLA_VAL_DIM
    x, (w_gla, w_out1, w_up1, w_down1) = _attn_tail(
        outs, lses, x, w_out0, row(norm_mlp_w[0]), w_up0, w_down0,
        riders=[(gla_w_in, 0, n_main), (gla_w_out, 0, D), (mlp_w_up, 1, D_FF), (mlp_w_down, 1, D)],
        rows=512)

    q, k, v, g, la = _gla_in_project(x, row(norm_mix_w[1]), w_gla, gla_w_in[0][:, n_main:],
                                     gla_w_gate_up[0], row(gla_b_gate[0]), rows=512)
    o = _gla_recurrence(q, k, v, g, la, row(gla_norm_w[0]), B, S, rows=512)
    return _gla_tail(o, x, w_out1, row(norm_mlp_w[1]), w_up1, w_down1, row(final_norm_w), rows=512)
```

```python
from functools import partial

import jax
import jax.numpy as jnp
import numpy as np
from jax import lax
from jax.experimental import pallas as pl
from jax.experimental.pallas import tpu as pltpu

D_MODEL = 1024
NORM_EPS = 1e-5
DILATED_GROUPS = ((128, 1), (512, 4), (2048, 16))
ATTN_HEADS = 8
HEAD_DIM = 128
ATTN_WIDTH = ATTN_HEADS * HEAD_DIM
ATTN_BLOCK = 128
ROPE_THETA = 10000.0
GLA_HEADS = 4
GLA_DK = 128
GLA_DV = 256
GLA_KEY_DIM = GLA_HEADS * GLA_DK
GLA_VAL_DIM = GLA_HEADS * GLA_DV
GLA_GATE_RANK = 16
GLA_GATE_NORMALIZER = 16.0
GLA_CHUNK = 64
D_FF = 4 * D_MODEL

LANES = 128
VMEM_LIMIT_BYTES = 56 * 1024 * 1024
NEG_BIG = -1e30

BF16 = jnp.bfloat16
F32 = jnp.float32


def _params(*semantics):
    return pltpu.CompilerParams(dimension_semantics=semantics,
                                vmem_limit_bytes=VMEM_LIMIT_BYTES)


def _rmsnorm_rows(x, w):
    var = jnp.mean(x * x, axis=-1, keepdims=True)
    return x * lax.rsqrt(var + NORM_EPS) * w


def _dot(a, b):
    return jnp.dot(a, b, preferred_element_type=F32)


def _dot_nt(a, b):
    return lax.dot_general(a, b, (((1,), (1,)), ((), ())), preferred_element_type=F32)


def _dot_tn(a, b):
    return lax.dot_general(a, b, (((0,), (0,)), ((), ())), preferred_element_type=F32)


def _cast_blocks(src_refs, dst_refs):
    for src_ref, dst_ref in zip(src_refs, dst_refs):
        dst_ref[...] = src_ref[:, :dst_ref.shape[1]].astype(BF16)


def _rider_specs(riders, n_steps):
    in_specs, out_specs, out_shapes = [], [], []
    for w, layer, n_keep in riders:
        _, k, n = w.shape
        assert k % n_steps == 0 and (k // n_steps) % 16 == 0
        blk = k // n_steps
        index = lambda s, layer=layer: (layer, jnp.minimum(s, n_steps - 1), 0)
        in_specs.append(pl.BlockSpec((None, blk, n), index))
        out_specs.append(pl.BlockSpec((blk, n_keep), lambda s: (jnp.minimum(s, n_steps - 1), 0)))
        out_shapes.append(jax.ShapeDtypeStruct((k, n_keep), BF16))
    return in_specs, out_specs, out_shapes


QKV_CHUNK = 256
Q_SCALE = HEAD_DIM ** -0.5 * float(np.log2(np.e))


def _qkv_kernel(*refs, n_riders):
    n_slabs = D_MODEL // LANES
    n_out = 3 * len(DILATED_GROUPS)
    x_refs = refs[:n_slabs]
    nw_ref, w_ref, cos_ref, sin_ref = refs[n_slabs:n_slabs + 4]
    rider_in = refs[n_slabs + 4:n_slabs + 4 + n_riders]
    out_refs = refs[n_slabs + 4 + n_riders:n_slabs + 4 + n_riders + n_out]
    rider_out = refs[n_slabs + 4 + n_riders + n_out:-1]
    tab_ref = refs[-1]
    rows = x_refs[0].shape[0]
    n_chunks = ATTN_WIDTH // QKV_CHUNK
    row_blocks = rows // ATTN_BLOCK
    _cast_blocks(rider_in, rider_out)

    def residue_major(src_ref, dil):
        if dil == 1:
            return src_ref[...]
        n = rows // dil
        return jnp.concatenate([src_ref[pl.ds(r, n, stride=dil), :] for r in range(dil)], axis=0)

    for g, (_, dil) in enumerate(DILATED_GROUPS):
        n = rows // dil
        xg = jnp.concatenate([residue_major(x_ref, dil) for x_ref in x_refs], axis=1)
        h = _rmsnorm_rows(xg, nw_ref[...]).astype(BF16)
        if dil == 1:
            cos_src, sin_src = cos_ref, sin_ref
        else:
            tab_ref[0] = residue_major(cos_ref, dil)
            tab_ref[1] = residue_major(sin_ref, dil)
            cos_src, sin_src = tab_ref.at[0], tab_ref.at[1]
        for part in range(3):
            o_ref = out_refs[3 * g + part]
            for j in range(n_chunks):
                col = (3 * g + part) * ATTN_WIDTH + j * QKV_CHUNK
                acc = _dot(h, w_ref[:, col:col + QKV_CHUNK])
                for half in range(QKV_CHUNK // HEAD_DIM):
                    lo = j * QKV_CHUNK + half * HEAD_DIM
                    for rb in range(row_blocks):
                        rs = slice(rb * ATTN_BLOCK, (rb + 1) * ATTN_BLOCK)
                        t = acc[rs, half * HEAD_DIM:(half + 1) * HEAD_DIM]
                        if part < 2:
                            t = t * cos_src[rs, :] + pltpu.roll(t, HEAD_DIM // 2, 1) * sin_src[rs, :]
                        if part == 0:
                            t = t * Q_SCALE
                        t = t.astype(BF16)
                        if n >= ATTN_BLOCK:
                            first = rb * ATTN_BLOCK
                            o_ref[first // n, pl.ds(first % n, ATTN_BLOCK), lo:lo + HEAD_DIM] = t
                        else:
                            for k in range(ATTN_BLOCK // n):
                                o_ref[rb * (ATTN_BLOCK // n) + k, :, lo:lo + HEAD_DIM] = (
                                    t[k * n:(k + 1) * n])


def _qkv_project(x, nw, w_in, cos2, sin2, riders, rows):
    B, S, D = x.shape
    n_blocks = S // rows
    out_shapes, out_specs = [], []
    for _, dil in DILATED_GROUPS:
        for _ in range(3):
            out_shapes.append(jax.ShapeDtypeStruct((B, dil, S // dil, ATTN_WIDTH), BF16))
            out_specs.append(pl.BlockSpec((None, dil, rows // dil, ATTN_WIDTH),
                                          lambda i: (i // n_blocks, 0, i % n_blocks, 0)))
    rider_in, rider_out, rider_shapes = _rider_specs(riders, B * n_blocks)
    tab_spec = pl.BlockSpec((rows, HEAD_DIM), lambda i: (i % n_blocks, 0))
    const = lambda shape: pl.BlockSpec(shape, lambda i: (0, 0), pipeline_mode=pl.Buffered(1))
    xf = x.reshape(B * S, D)
    slab_specs = [pl.BlockSpec((rows, LANES), lambda i, c=c: (i, c)) for c in range(D // LANES)]
    outs = pl.pallas_call(
        partial(_qkv_kernel, n_riders=len(riders)),
        grid=(B * n_blocks,),
        in_specs=slab_specs + [const((1, D)), const((D, w_in.shape[1])), tab_spec, tab_spec]
        + rider_in,
        out_specs=out_specs + rider_out,
        out_shape=out_shapes + rider_shapes,
        scratch_shapes=[pltpu.VMEM((2, rows, HEAD_DIM), F32)],
        compiler_params=_params("parallel"),
        name="attn_qkv",
    )(*([xf] * (D // LANES)), nw, w_in, cos2, sin2, *[w for w, _, _ in riders])
    return outs[:len(out_shapes)], outs[len(out_shapes):]


def _band_bias():
    qi = np.arange(ATTN_BLOCK)[:, None]
    kj = np.arange(2 * ATTN_BLOCK)[None, :]
    rel = qi - kj + ATTN_BLOCK
    band = (rel >= 0) & (rel <= ATTN_BLOCK)
    masks = np.stack([band & (kj >= ATTN_BLOCK), band])
    return jnp.asarray(np.where(masks, 0.0, NEG_BIG).astype(np.float32))


def _attn_kernel(q_ref, k_ref, v_ref, kh_ref, vh_ref, bias_ref, o_ref, lse_ref, *, n_sub):
    first_bias = jnp.where(pl.program_id(2) > 0, 1, 0)
    lane = lax.broadcasted_iota(jnp.int32, (ATTN_BLOCK, LANES), 1)
    for c in range(n_sub):
        rows = slice(c * ATTN_BLOCK, (c + 1) * ATTN_BLOCK)
        bias = bias_ref.at[first_bias if c == 0 else 1]
        lse_tile = jnp.zeros((ATTN_BLOCK, LANES), F32)
        for h in range(ATTN_HEADS):
            cols = slice(h * HEAD_DIM, (h + 1) * HEAD_DIM)
            q = q_ref[rows, cols]
            if c == 0:
                kk = jnp.concatenate([kh_ref[:, cols], k_ref[rows, cols]], axis=0)
                vv = jnp.concatenate([vh_ref[:, cols], v_ref[rows, cols]], axis=0)
            else:
                hist = slice((c - 1) * ATTN_BLOCK, (c + 1) * ATTN_BLOCK)
                kk = k_ref[hist, cols]
                vv = v_ref[hist, cols]
            s = _dot_nt(q, kk) + bias[...]
            m = jnp.max(s, axis=-1, keepdims=True)
            p = jnp.exp2(s - m)
            l = jnp.sum(p, axis=-1, keepdims=True)
            o = _dot(p.astype(BF16), vv) / l
            o_ref[rows, cols] = o.astype(BF16)
            lse_tile = jnp.where(lane == h, m + jnp.log2(l), lse_tile)
        lse_ref[rows, :] = lse_tile


def _group_attention(q, k, v, rows):
    B, dil, L, W = q.shape
    rows = min(rows, L)
    n_sub = rows // ATTN_BLOCK
    blk = lambda width: pl.BlockSpec((None, None, rows, width), lambda b, r, i: (b, r, i, 0))
    halo = pl.BlockSpec((None, None, ATTN_BLOCK, W),
                        lambda b, r, i: (b, r, jnp.maximum(i * n_sub - 1, 0), 0))
    return pl.pallas_call(
        partial(_attn_kernel, n_sub=n_sub),
        grid=(B, dil, L // rows),
        in_specs=[blk(W), blk(W), blk(W), halo, halo,
                  pl.BlockSpec((2, ATTN_BLOCK, 2 * ATTN_BLOCK), lambda b, r, i: (0, 0, 0))],
        out_specs=[blk(W), blk(LANES)],
        out_shape=[
            jax.ShapeDtypeStruct((B, dil, L, W), BF16),
            jax.ShapeDtypeStruct((B, dil, L, LANES), F32),
        ],
        compiler_params=_params("parallel", "parallel", "parallel"),
        name="attn_band",
    )(q, k, v, k, v, _band_bias())


MLP_FF_CHUNK = 1024


def _project_and_mlp(mixed, x_ref, wo_ref, nw_ref, wu_ref, wd_ref, side_work=None):
    x_mid = x_ref[...] + _dot(mixed, wo_ref[...])
    h = _rmsnorm_rows(x_mid, nw_ref[...]).astype(BF16)
    acc = x_mid
    for f in range(D_FF // MLP_FF_CHUNK):
        cols = slice(f * MLP_FF_CHUNK, (f + 1) * MLP_FF_CHUNK)
        a = jnp.maximum(_dot(h, wu_ref[:, cols]), 0.0)
        acc = acc + _dot((a * a).astype(BF16), wd_ref[cols, :])
        if side_work is not None:
            side_work(f)
    return acc


def _attn_tail_kernel(*refs, n_riders):
    n_groups = len(DILATED_GROUPS)
    o_refs = [refs[g * ATTN_HEADS:(g + 1) * ATTN_HEADS] for g in range(n_groups)]
    lse_refs = refs[n_groups * ATTN_HEADS:n_groups * (ATTN_HEADS + 1)]
    rest = refs[n_groups * (ATTN_HEADS + 1):]
    x_ref, wo_ref, nw_ref, wu_ref, wd_ref = rest[:5]
    rider_in = rest[5:5 + n_riders]
    out_ref = rest[5 + n_riders]
    rider_out = rest[6 + n_riders:6 + 2 * n_riders]
    merged_ref, nat_ref = rest[6 + 2 * n_riders:]
    rows = x_ref.shape[0]
    _cast_blocks(rider_in, rider_out)
    step = pl.program_id(0)
    write_slot = step % 2
    read_slot = 1 - write_slot

    @pl.when(step == 0)
    def _():
        merged_ref[1] = jnp.zeros(merged_ref.shape[1:], BF16)

    def natural(src_ref, dil, slot):
        if dil == 1:
            return src_ref[0].astype(F32)
        for r in range(dil):
            nat_ref[slot, pl.ds(r, rows // dil, stride=dil), :] = src_ref[r].astype(F32)
        return nat_ref[slot]

    mixed = merged_ref[read_slot]
    heads_per_chunk = ATTN_HEADS // (D_FF // MLP_FF_CHUNK)
    alphas = []

    def merge_heads(f):
        if not alphas:
            lses = [natural(lse_refs[g], dil, g) for g, (_, dil) in enumerate(DILATED_GROUPS)]
            m = jnp.maximum(jnp.maximum(lses[0], lses[1]), lses[2])
            es = [jnp.exp2(l - m) for l in lses]
            inv = 1.0 / (es[0] + es[1] + es[2])
            alphas.extend(e * inv for e in es)
        for h in range(f * heads_per_chunk, (f + 1) * heads_per_chunk):
            o = None
            for g, (_, dil) in enumerate(DILATED_GROUPS):
                slot = n_groups + 2 * g + h % 2
                term = alphas[g][:, h:h + 1] * natural(o_refs[g][h], dil, slot)
                o = term if o is None else o + term
            merged_ref[write_slot, :, h * HEAD_DIM:(h + 1) * HEAD_DIM] = o.astype(BF16)

    out_ref[...] = _project_and_mlp(mixed, x_ref, wo_ref, nw_ref, wu_ref, wd_ref,
                                    side_work=merge_heads)


def _attn_tail(o_list, lse_list, x, w_out, nw, w_up, w_down, riders, rows):
    B, S, D = x.shape
    W = ATTN_WIDTH
    n_blocks = S // rows
    n_groups = len(DILATED_GROUPS)
    n_tiles = B * n_blocks

    def residue_spec(dil, col):
        def index(s):
            t = jnp.minimum(s, n_tiles - 1)
            return (t // n_blocks, 0, t % n_blocks, col)
        return pl.BlockSpec((None, dil, rows // dil, LANES), index)

    o_specs = [residue_spec(dil, h) for _, dil in DILATED_GROUPS for h in range(ATTN_HEADS)]
    o_args = [o for o in o_list for _ in range(ATTN_HEADS)]
    lse_specs = [residue_spec(dil, 0) for _, dil in DILATED_GROUPS]
    rider_in, rider_out, rider_shapes = _rider_specs(riders, n_tiles)
    row_spec = pl.BlockSpec((rows, D), lambda s: (jnp.maximum(s - 1, 0), 0))
    const = lambda shape: pl.BlockSpec(shape, lambda s: (0, 0), pipeline_mode=pl.Buffered(1))
    outs = pl.pallas_call(
        partial(_attn_tail_kernel, n_riders=len(riders)),
        grid=(n_tiles + 1,),
        in_specs=o_specs + lse_specs + [row_spec, const((W, D)), const((1, D)),
                                        const((D, D_FF)), const((D_FF, D))] + rider_in,
        out_specs=[row_spec] + rider_out,
        out_shape=[jax.ShapeDtypeStruct((B * S, D), F32)] + rider_shapes,
        scratch_shapes=[pltpu.VMEM((2, rows, W), BF16),
                        pltpu.VMEM((3 * n_groups, rows, LANES), F32)],
        compiler_params=_params("arbitrary"),
        name="attn_tail",
    )(*o_args, *lse_list, x.reshape(B * S, D), w_out, nw, w_up, w_down,
      *[w for w, _, _ in riders])
    return outs[0].reshape(B, S, D), outs[1:]


def _gla_tail_kernel(o_ref, x_ref, wo_ref, nw_ref, wu_ref, wd_ref, nwf_ref, out_ref):
    acc = _project_and_mlp(o_ref[...], x_ref, wo_ref, nw_ref, wu_ref, wd_ref)
    out_ref[...] = _rmsnorm_rows(acc, nwf_ref[...])


def _gla_tail(o, x, w_out, nw, w_up, w_down, nw_final, rows):
    B, S, D = x.shape
    T = B * S
    row_spec = lambda width: pl.BlockSpec((rows, width), lambda i: (i, 0))
    const = lambda shape: pl.BlockSpec(shape, lambda i: (0, 0), pipeline_mode=pl.Buffered(1))
    out = pl.pallas_call(
        _gla_tail_kernel,
        grid=(T // rows,),
        in_specs=[row_spec(o.shape[-1]), row_spec(D), const((o.shape[-1], D)), const((1, D)),
                  const((D, D_FF)), const((D_FF, D)), const((1, D))],
        out_specs=row_spec(D),
        out_shape=jax.ShapeDtypeStruct((T, D), F32),
        compiler_params=_params("parallel"),
        name="gla_tail",
    )(o, x.reshape(T, D), w_out, nw, w_up, w_down, nw_final)
    return out.reshape(B, S, D)


def _gla_in_kernel(x_ref, nw_ref, w_ref, wg_ref, wgu_ref, bg_ref,
                   q_ref, k_ref, v_ref, g_ref, la_ref):
    h = _rmsnorm_rows(x_ref[...], nw_ref[...]).astype(BF16)
    q_ref[...] = _dot(h, w_ref[:, :GLA_KEY_DIM])
    k_ref[...] = _dot(h, w_ref[:, GLA_KEY_DIM:2 * GLA_KEY_DIM])
    v_lo = 2 * GLA_KEY_DIM
    v_ref[...] = _dot(h, w_ref[:, v_lo:v_lo + GLA_VAL_DIM]).astype(BF16)
    g_lo = v_lo + GLA_VAL_DIM
    g_ref[...] = _dot(h, w_ref[:, g_lo:g_lo + GLA_VAL_DIM])
    gate_lr = _dot(h, wg_ref[...].astype(BF16)).astype(BF16)
    gk = _dot(gate_lr, wgu_ref[...].astype(BF16)) + bg_ref[...]
    log_sig = jnp.minimum(gk, 0.0) - jnp.log(1.0 + jnp.exp(-jnp.abs(gk)))
    la_ref[...] = log_sig * (1.0 / GLA_GATE_NORMALIZER)


def _gla_in_project(x, nw, w_main, w_gate, w_gate_up, b_gate, rows):
    B, S, D = x.shape
    T = B * S
    row_spec = lambda width: pl.BlockSpec((rows, width), lambda i: (i, 0))
    const = lambda shape: pl.BlockSpec(shape, lambda i: (0, 0))
    n_main = 2 * GLA_KEY_DIM + 2 * GLA_VAL_DIM
    return pl.pallas_call(
        _gla_in_kernel,
        grid=(T // rows,),
        in_specs=[row_spec(D), const((1, D)), const((D, n_main)), const((D, GLA_GATE_RANK)),
                  const((GLA_GATE_RANK, GLA_KEY_DIM)), const((1, GLA_KEY_DIM))],
        out_specs=[row_spec(GLA_KEY_DIM), row_spec(GLA_KEY_DIM), row_spec(GLA_VAL_DIM),
                   row_spec(GLA_VAL_DIM), row_spec(GLA_KEY_DIM)],
        out_shape=[
            jax.ShapeDtypeStruct((T, GLA_KEY_DIM), F32),
            jax.ShapeDtypeStruct((T, GLA_KEY_DIM), F32),
            jax.ShapeDtypeStruct((T, GLA_VAL_DIM), BF16),
            jax.ShapeDtypeStruct((T, GLA_VAL_DIM), F32),
            jax.ShapeDtypeStruct((T, GLA_KEY_DIM), F32),
        ],
        compiler_params=_params("parallel"),
        name="gla_in",
    )(x.reshape(T, D), nw, w_main, w_gate, w_gate_up, b_gate)


def _gla_kernel(q_ref, k_ref, v_ref, g_ref, la_ref, nw_ref, o_ref, state_ref, *, n_chunks):
    @pl.when(pl.program_id(1) == 0)
    def _():
        state_ref[...] = jnp.zeros_like(state_ref)

    pair = 2 * GLA_CHUNK
    ri = lax.broadcasted_iota(jnp.int32, (pair, pair), 0)
    cj = lax.broadcasted_iota(jnp.int32, (pair, pair), 1)
    causal = ri >= cj
    cumsum_mat = (causal & ((ri >= GLA_CHUNK) == (cj >= GLA_CHUNK))).astype(BF16)
    first = lax.broadcasted_iota(jnp.int32, (pair, GLA_KEY_DIM), 0) < GLA_CHUNK
    ones = jnp.ones((pair, LANES), BF16)
    nw = nw_ref[...]
    for p in range(n_chunks // 2):
        rows = slice(p * pair, (p + 1) * pair)
        la = la_ref[rows, :]
        la_hi = la.astype(BF16)
        la_lo = (la - la_hi.astype(F32)).astype(BF16)
        b = _dot(cumsum_mat, la_hi) + _dot(cumsum_mat, la_lo)
        bl0 = b[GLA_CHUNK - 1:GLA_CHUNK, :]
        bl1 = b[pair - 1:pair, :]
        ref_b = jnp.where(first, b - bl0, b)
        q_e = q_ref[rows, :] * jnp.exp(ref_b)
        k_e = k_ref[rows, :] * jnp.exp(-ref_b)
        scale = GLA_DK ** -0.5
        q_sc = (q_e * scale).astype(BF16)
        k_sc = k_e.astype(BF16)
        q_st = (q_e * (jnp.exp(bl0) * scale)).astype(BF16)
        k_up = (k_e * jnp.exp(bl1)).astype(BF16)
        for h in range(GLA_HEADS):
            kc = slice(h * GLA_DK, (h + 1) * GLA_DK)
            vc = slice(h * GLA_DV, (h + 1) * GLA_DV)
            v = v_ref[rows, vc]
            a = jnp.where(causal, _dot_nt(q_sc[:, kc], k_sc[:, kc]), 0.0).astype(BF16)
            state = state_ref[h]
            o = _dot(jnp.concatenate([a, q_st[:, kc]], axis=1),
                     jnp.concatenate([v, state.astype(BF16)], axis=0))
            bl_t = _dot_tn(la_hi[:, kc], ones) + _dot_tn(la_lo[:, kc], ones)
            decay = jnp.exp(bl_t)
            decay = jnp.concatenate([decay] * (GLA_DV // LANES), axis=1)
            state_ref[h] = decay * state + _dot_tn(k_up[:, kc], v)
            var = jnp.mean(o * o, axis=-1, keepdims=True)
            o = o * lax.rsqrt(var + NORM_EPS) * nw
            g = g_ref[rows, vc]
            half_g = 0.5 * g
            o_ref[rows, vc] = (o * (half_g * (1.0 + jnp.tanh(half_g)))).astype(BF16)


def _gla_recurrence(q, k, v, g, la, nw, B, S, rows):
    T = B * S
    n_blocks = S // rows
    row_spec = lambda width: pl.BlockSpec((rows, width), lambda b, i: (b * n_blocks + i, 0))
    return pl.pallas_call(
        partial(_gla_kernel, n_chunks=rows // GLA_CHUNK),
        grid=(B, n_blocks),
        in_specs=[row_spec(GLA_KEY_DIM), row_spec(GLA_KEY_DIM), row_spec(GLA_VAL_DIM),
                  row_spec(GLA_VAL_DIM), row_spec(GLA_KEY_DIM),
                  pl.BlockSpec((1, GLA_DV), lambda b, i: (0, 0))],
        out_specs=row_spec(GLA_VAL_DIM),
        out_shape=jax.ShapeDtypeStruct((T, GLA_VAL_DIM), BF16),
        scratch_shapes=[pltpu.VMEM((GLA_HEADS, GLA_DK, GLA_DV), F32)],
        compiler_params=_params("parallel", "arbitrary"),
        name="gla_recurrence",
    )(q, k, v, g, la, nw)


def _rope_tables(seq_len):
    pos = np.arange(seq_len, dtype=np.float64)
    inv_freq = ROPE_THETA ** (-np.arange(0, HEAD_DIM, 2, dtype=np.float64) / HEAD_DIM)
    ang = pos[:, None] * inv_freq[None, :]
    cos, sin = np.cos(ang), np.sin(ang)
    cos2 = np.concatenate([cos, cos], axis=-1).astype(np.float32)
    sin2 = np.concatenate([-sin, sin], axis=-1).astype(np.float32)
    return jnp.asarray(cos2), jnp.asarray(sin2)


def kernel(x, norm_mix_w, norm_mlp_w, final_norm_w, attn_w_in, attn_w_out, gla_w_in,
           gla_w_gate_up, gla_b_gate, gla_norm_w, gla_w_out, mlp_w_up, mlp_w_down):
    B, S, D = x.shape
    row = lambda w: w.reshape(1, -1).astype(F32)

    cos2, sin2 = _rope_tables(S)
    qkv, (w_out0, w_up0, w_down0) = _qkv_project(
        x, row(norm_mix_w[0]), attn_w_in[0].astype(BF16), cos2, sin2,
        riders=[(attn_w_out, 0, D), (mlp_w_up, 0, D_FF), (mlp_w_down, 0, D)], rows=512)
    outs, lses = [], []
    for g, (window, dil) in enumerate(DILATED_GROUPS):
        assert window // dil == ATTN_BLOCK and (S // dil) % ATTN_BLOCK == 0
        q, k, v = qkv[3 * g:3 * g + 3]
        o, lse = _group_attention(q, k, v, rows=1024)
        outs.append(o)
        lses.append(lse)
    n_main = 2 * GLA_KEY_DIM + 2 * GLA_VAL_DIM
    x, (w_gla, w_out1, w_up1, w_down1) = _attn_tail(
        outs, lses, x, w_out0, row(norm_mlp_w[0]), w_up0, w_down0,
        riders=[(gla_w_in, 0, n_main), (gla_w_out, 0, D), (mlp_w_up, 1, D_FF), (mlp_w_down, 1, D)],
        rows=512)

    q, k, v, g, la = _gla_in_project(x, row(norm_mix_w[1]), w_gla, gla_w_in[0][:, n_main:],
                                     gla_w_gate_up[0], row(gla_b_gate[0]), rows=1024)
    o = _gla_recurrence(q, k, v, g, la, row(gla_norm_w[0]), B, S, rows=512)
    return _gla_tail(o, x, w_out1, row(norm_mlp_w[1]), w_up1, w_down1, row(final_norm_w), rows=512)
```

```python
from functools import partial

import jax
import jax.numpy as jnp
import numpy as np
from jax import lax
from jax.experimental import pallas as pl
from jax.experimental.pallas import tpu as pltpu

D_MODEL = 1024
NORM_EPS = 1e-5
DILATED_GROUPS = ((128, 1), (512, 4), (2048, 16))
ATTN_HEADS = 8
HEAD_DIM = 128
ATTN_WIDTH = ATTN_HEADS * HEAD_DIM
ATTN_BLOCK = 128
ROPE_THETA = 10000.0
GLA_HEADS = 4
GLA_DK = 128
GLA_DV = 256
GLA_KEY_DIM = GLA_HEADS * GLA_DK
GLA_VAL_DIM = GLA_HEADS * GLA_DV
GLA_GATE_RANK = 16
GLA_GATE_NORMALIZER = 16.0
GLA_CHUNK = 64
D_FF = 4 * D_MODEL

LANES = 128
VMEM_LIMIT_BYTES = 56 * 1024 * 1024
RESIDENT_ROWS = 512
STREAM_ROWS = 1024
NEG_BIG = -1e30

BF16 = jnp.bfloat16
F32 = jnp.float32


def _params(*semantics):
    return pltpu.CompilerParams(dimension_semantics=semantics,
                                vmem_limit_bytes=VMEM_LIMIT_BYTES)


def _rmsnorm_rows(x, w):
    var = jnp.mean(x * x, axis=-1, keepdims=True)
    return x * lax.rsqrt(var + NORM_EPS) * w


def _dot(a, b):
    return jnp.dot(a, b, preferred_element_type=F32)


def _dot_nt(a, b):
    return lax.dot_general(a, b, (((1,), (1,)), ((), ())), preferred_element_type=F32)


def _dot_tn(a, b):
    return lax.dot_general(a, b, (((0,), (0,)), ((), ())), preferred_element_type=F32)


def _cast_blocks(src_refs, dst_refs):
    for src_ref, dst_ref in zip(src_refs, dst_refs):
        dst_ref[...] = src_ref[:, :dst_ref.shape[1]].astype(BF16)


def _rider_specs(riders, n_steps):
    in_specs, out_specs, out_shapes = [], [], []
    for w, layer, n_keep in riders:
        _, k, n = w.shape
        assert k % n_steps == 0 and (k // n_steps) % 16 == 0
        blk = k // n_steps
        index = lambda s, layer=layer: (layer, jnp.minimum(s, n_steps - 1), 0)
        in_specs.append(pl.BlockSpec((None, blk, n), index))
        out_specs.append(pl.BlockSpec((blk, n_keep), lambda s: (jnp.minimum(s, n_steps - 1), 0)))
        out_shapes.append(jax.ShapeDtypeStruct((k, n_keep), BF16))
    return in_specs, out_specs, out_shapes


QKV_CHUNK = 256
Q_SCALE = HEAD_DIM ** -0.5 * float(np.log2(np.e))


def _qkv_kernel(*refs, n_riders):
    n_slabs = D_MODEL // LANES
    n_out = 3 * len(DILATED_GROUPS)
    x_refs = refs[:n_slabs]
    nw_ref, w_ref, cos_ref, sin_ref = refs[n_slabs:n_slabs + 4]
    rider_in = refs[n_slabs + 4:n_slabs + 4 + n_riders]
    out_refs = refs[n_slabs + 4 + n_riders:n_slabs + 4 + n_riders + n_out]
    rider_out = refs[n_slabs + 4 + n_riders + n_out:-1]
    tab_ref = refs[-1]
    rows = x_refs[0].shape[0]
    n_chunks = ATTN_WIDTH // QKV_CHUNK
    row_blocks = rows // ATTN_BLOCK
    _cast_blocks(rider_in, rider_out)

    def residue_major(src_ref, dil):
        if dil == 1:
            return src_ref[...]
        n = rows // dil
        return jnp.concatenate([src_ref[pl.ds(r, n, stride=dil), :] for r in range(dil)], axis=0)

    for g, (_, dil) in enumerate(DILATED_GROUPS):
        n = rows // dil
        xg = jnp.concatenate([residue_major(x_ref, dil) for x_ref in x_refs], axis=1)
        h = _rmsnorm_rows(xg, nw_ref[...]).astype(BF16)
        if dil == 1:
            cos_src, sin_src = cos_ref, sin_ref
        else:
            tab_ref[0] = residue_major(cos_ref, dil)
            tab_ref[1] = residue_major(sin_ref, dil)
            cos_src, sin_src = tab_ref.at[0], tab_ref.at[1]
        for part in range(3):
            o_ref = out_refs[3 * g + part]
            for j in range(n_chunks):
                col = (3 * g + part) * ATTN_WIDTH + j * QKV_CHUNK
                acc = _dot(h, w_ref[:, col:col + QKV_CHUNK])
                for half in range(QKV_CHUNK // HEAD_DIM):
                    lo = j * QKV_CHUNK + half * HEAD_DIM
                    for rb in range(row_blocks):
                        rs = slice(rb * ATTN_BLOCK, (rb + 1) * ATTN_BLOCK)
                        t = acc[rs, half * HEAD_DIM:(half + 1) * HEAD_DIM]
                        if part < 2:
                            t = t * cos_src[rs, :] + pltpu.roll(t, HEAD_DIM // 2, 1) * sin_src[rs, :]
                        if part == 0:
                            t = t * Q_SCALE
                        t = t.astype(BF16)
                        if n >= ATTN_BLOCK:
                            first = rb * ATTN_BLOCK
                            o_ref[first // n, pl.ds(first % n, ATTN_BLOCK), lo:lo + HEAD_DIM] = t
                        else:
                            for k in range(ATTN_BLOCK // n):
                                o_ref[rb * (ATTN_BLOCK // n) + k, :, lo:lo + HEAD_DIM] = (
                                    t[k * n:(k + 1) * n])


def _qkv_project(x, nw, w_in, cos2, sin2, riders, rows):
    B, S, D = x.shape
    n_blocks = S // rows
    out_shapes, out_specs = [], []
    for _, dil in DILATED_GROUPS:
        for _ in range(3):
            out_shapes.append(jax.ShapeDtypeStruct((B, dil, S // dil, ATTN_WIDTH), BF16))
            out_specs.append(pl.BlockSpec((None, dil, rows // dil, ATTN_WIDTH),
                                          lambda i: (i // n_blocks, 0, i % n_blocks, 0)))
    rider_in, rider_out, rider_shapes = _rider_specs(riders, B * n_blocks)
    tab_spec = pl.BlockSpec((rows, HEAD_DIM), lambda i: (i % n_blocks, 0))
    const = lambda shape: pl.BlockSpec(shape, lambda i: (0, 0), pipeline_mode=pl.Buffered(1))
    xf = x.reshape(B * S, D)
    slab_specs = [pl.BlockSpec((rows, LANES), lambda i, c=c: (i, c)) for c in range(D // LANES)]
    outs = pl.pallas_call(
        partial(_qkv_kernel, n_riders=len(riders)),
        grid=(B * n_blocks,),
        in_specs=slab_specs + [const((1, D)), const((D, w_in.shape[1])), tab_spec, tab_spec]
        + rider_in,
        out_specs=out_specs + rider_out,
        out_shape=out_shapes + rider_shapes,
        scratch_shapes=[pltpu.VMEM((2, rows, HEAD_DIM), F32)],
        compiler_params=_params("parallel"),
        name="attn_qkv",
    )(*([xf] * (D // LANES)), nw, w_in, cos2, sin2, *[w for w, _, _ in riders])
    return outs[:len(out_shapes)], outs[len(out_shapes):]


def _band_bias():
    qi = np.arange(ATTN_BLOCK)[:, None]
    kj = np.arange(2 * ATTN_BLOCK)[None, :]
    rel = qi - kj + ATTN_BLOCK
    band = (rel >= 0) & (rel <= ATTN_BLOCK)
    masks = np.stack([band & (kj >= ATTN_BLOCK), band])
    return jnp.asarray(np.where(masks, 0.0, NEG_BIG).astype(np.float32))


def _attn_kernel(q_ref, k_ref, v_ref, kh_ref, vh_ref, bias_ref, o_ref, lse_ref):
    first_bias = jnp.where(pl.program_id(2) > 0, 1, 0)
    lane = lax.broadcasted_iota(jnp.int32, (ATTN_BLOCK, LANES), 1)
    n_res, n_rows, _ = q_ref.shape
    for r in range(n_res):
        for c in range(n_rows // ATTN_BLOCK):
            rows = slice(c * ATTN_BLOCK, (c + 1) * ATTN_BLOCK)
            bias = bias_ref.at[first_bias if c == 0 else 1]
            lse_tile = jnp.zeros((ATTN_BLOCK, LANES), F32)
            for h in range(ATTN_HEADS):
                cols = slice(h * HEAD_DIM, (h + 1) * HEAD_DIM)
                q = q_ref[r, rows, cols]
                if c == 0:
                    kk = jnp.concatenate([kh_ref[r, :, cols], k_ref[r, rows, cols]], axis=0)
                    vv = jnp.concatenate([vh_ref[r, :, cols], v_ref[r, rows, cols]], axis=0)
                else:
                    hist = slice((c - 1) * ATTN_BLOCK, (c + 1) * ATTN_BLOCK)
                    kk = k_ref[r, hist, cols]
                    vv = v_ref[r, hist, cols]
                s = _dot_nt(q, kk) + bias[...]
                m = jnp.max(s, axis=-1, keepdims=True)
                p = jnp.exp2(s - m)
                l = jnp.sum(p, axis=-1, keepdims=True)
                o = _dot(p.astype(BF16), vv) / l
                o_ref[r, rows, cols] = o.astype(BF16)
                lse_tile = jnp.where(lane == h, m + jnp.log2(l), lse_tile)
            lse_ref[r, rows, :] = lse_tile


def _group_attention(q, k, v, step_rows):
    B, dil, L, W = q.shape
    rows = min(step_rows, L)
    n_res = min(step_rows // rows, dil)
    n_sub = rows // ATTN_BLOCK
    blk = lambda width: pl.BlockSpec((None, n_res, rows, width), lambda b, r, i: (b, r, i, 0))
    halo = pl.BlockSpec((None, n_res, ATTN_BLOCK, W),
                        lambda b, r, i: (b, r, jnp.maximum(i * n_sub - 1, 0), 0))
    return pl.pallas_call(
        _attn_kernel,
        grid=(B, dil // n_res, L // rows),
        in_specs=[blk(W), blk(W), blk(W), halo, halo,
                  pl.BlockSpec((2, ATTN_BLOCK, 2 * ATTN_BLOCK), lambda b, r, i: (0, 0, 0))],
        out_specs=[blk(W), blk(LANES)],
        out_shape=[
            jax.ShapeDtypeStruct((B, dil, L, W), BF16),
            jax.ShapeDtypeStruct((B, dil, L, LANES), F32),
        ],
        compiler_params=_params("parallel", "parallel", "parallel"),
        name="attn_band",
    )(q, k, v, k, v, _band_bias())


MLP_FF_CHUNK = 1024


def _project_and_mlp(mixed, x_ref, wo_ref, nw_ref, wu_ref, wd_ref, side_work=None):
    x_mid = x_ref[...] + _dot(mixed, wo_ref[...])
    h = _rmsnorm_rows(x_mid, nw_ref[...]).astype(BF16)
    acc = x_mid
    for f in range(D_FF // MLP_FF_CHUNK):
        cols = slice(f * MLP_FF_CHUNK, (f + 1) * MLP_FF_CHUNK)
        a = jnp.maximum(_dot(h, wu_ref[:, cols]), 0.0)
        acc = acc + _dot((a * a).astype(BF16), wd_ref[cols, :])
        if side_work is not None:
            side_work(f)
    return acc


def _attn_tail_kernel(*refs, n_riders):
    n_groups = len(DILATED_GROUPS)
    o_refs = [refs[g * ATTN_HEADS:(g + 1) * ATTN_HEADS] for g in range(n_groups)]
    lse_refs = refs[n_groups * ATTN_HEADS:n_groups * (ATTN_HEADS + 1)]
    rest = refs[n_groups * (ATTN_HEADS + 1):]
    x_ref, wo_ref, nw_ref, wu_ref, wd_ref = rest[:5]
    rider_in = rest[5:5 + n_riders]
    out_ref = rest[5 + n_riders]
    rider_out = rest[6 + n_riders:6 + 2 * n_riders]
    merged_ref, nat_ref = rest[6 + 2 * n_riders:]
    rows = x_ref.shape[0]
    _cast_blocks(rider_in, rider_out)
    step = pl.program_id(0)
    write_slot = step % 2
    read_slot = 1 - write_slot

    @pl.when(step == 0)
    def _():
        merged_ref[1] = jnp.zeros(merged_ref.shape[1:], BF16)

    def natural(src_ref, dil, slot):
        if dil == 1:
            return src_ref[0].astype(F32)
        for r in range(dil):
            nat_ref[slot, pl.ds(r, rows // dil, stride=dil), :] = src_ref[r].astype(F32)
        return nat_ref[slot]

    mixed = merged_ref[read_slot]
    heads_per_chunk = ATTN_HEADS // (D_FF // MLP_FF_CHUNK)
    alphas = []

    def merge_heads(f):
        if not alphas:
            lses = [natural(lse_refs[g], dil, g) for g, (_, dil) in enumerate(DILATED_GROUPS)]
            m = jnp.maximum(jnp.maximum(lses[0], lses[1]), lses[2])
            es = [jnp.exp2(l - m) for l in lses]
            inv = 1.0 / (es[0] + es[1] + es[2])
            alphas.extend(e * inv for e in es)
        for h in range(f * heads_per_chunk, (f + 1) * heads_per_chunk):
            o = None
            for g, (_, dil) in enumerate(DILATED_GROUPS):
                slot = n_groups + 2 * g + h % 2
                term = alphas[g][:, h:h + 1] * natural(o_refs[g][h], dil, slot)
                o = term if o is None else o + term
            merged_ref[write_slot, :, h * HEAD_DIM:(h + 1) * HEAD_DIM] = o.astype(BF16)

    out_ref[...] = _project_and_mlp(mixed, x_ref, wo_ref, nw_ref, wu_ref, wd_ref,
                                    side_work=merge_heads)


def _attn_tail(o_list, lse_list, x, w_out, nw, w_up, w_down, riders, rows):
    B, S, D = x.shape
    W = ATTN_WIDTH
    n_blocks = S // rows
    n_groups = len(DILATED_GROUPS)
    n_tiles = B * n_blocks

    def residue_spec(dil, col):
        def index(s):
            t = jnp.minimum(s, n_tiles - 1)
            return (t // n_blocks, 0, t % n_blocks, col)
        return pl.BlockSpec((None, dil, rows // dil, LANES), index)

    o_specs = [residue_spec(dil, h) for _, dil in DILATED_GROUPS for h in range(ATTN_HEADS)]
    o_args = [o for o in o_list for _ in range(ATTN_HEADS)]
    lse_specs = [residue_spec(dil, 0) for _, dil in DILATED_GROUPS]
    rider_in, rider_out, rider_shapes = _rider_specs(riders, n_tiles)
    row_spec = pl.BlockSpec((rows, D), lambda s: (jnp.maximum(s - 1, 0), 0))
    const = lambda shape: pl.BlockSpec(shape, lambda s: (0, 0), pipeline_mode=pl.Buffered(1))
    outs = pl.pallas_call(
        partial(_attn_tail_kernel, n_riders=len(riders)),
        grid=(n_tiles + 1,),
        in_specs=o_specs + lse_specs + [row_spec, const((W, D)), const((1, D)),
                                        const((D, D_FF)), const((D_FF, D))] + rider_in,
        out_specs=[row_spec] + rider_out,
        out_shape=[jax.ShapeDtypeStruct((B * S, D), F32)] + rider_shapes,
        scratch_shapes=[pltpu.VMEM((2, rows, W), BF16),
                        pltpu.VMEM((3 * n_groups, rows, LANES), F32)],
        compiler_params=_params("arbitrary"),
        name="attn_tail",
    )(*o_args, *lse_list, x.reshape(B * S, D), w_out, nw, w_up, w_down,
      *[w for w, _, _ in riders])
    return outs[0].reshape(B, S, D), outs[1:]


def _gla_tail_kernel(o_ref, x_ref, wo_ref, nw_ref, wu_ref, wd_ref, nwf_ref, out_ref):
    acc = _project_and_mlp(o_ref[...], x_ref, wo_ref, nw_ref, wu_ref, wd_ref)
    out_ref[...] = _rmsnorm_rows(acc, nwf_ref[...])


def _gla_tail(o, x, w_out, nw, w_up, w_down, nw_final, rows):
    B, S, D = x.shape
    T = B * S
    row_spec = lambda width: pl.BlockSpec((rows, width), lambda i: (i, 0))
    const = lambda shape: pl.BlockSpec(shape, lambda i: (0, 0), pipeline_mode=pl.Buffered(1))
    out = pl.pallas_call(
        _gla_tail_kernel,
        grid=(T // rows,),
        in_specs=[row_spec(o.shape[-1]), row_spec(D), const((o.shape[-1], D)), const((1, D)),
                  const((D, D_FF)), const((D_FF, D)), const((1, D))],
        out_specs=row_spec(D),
        out_shape=jax.ShapeDtypeStruct((T, D), F32),
        compiler_params=_params("parallel"),
        name="gla_tail",
    )(o, x.reshape(T, D), w_out, nw, w_up, w_down, nw_final)
    return out.reshape(B, S, D)


def _gla_in_kernel(x_ref, nw_ref, w_ref, wg_ref, wgu_ref, bg_ref,
                   q_ref, k_ref, v_ref, g_ref, la_ref):
    h = _rmsnorm_rows(x_ref[...], nw_ref[...]).astype(BF16)
    q_ref[...] = _dot(h, w_ref[:, :GLA_KEY_DIM])
    k_ref[...] = _dot(h, w_ref[:, GLA_KEY_DIM:2 * GLA_KEY_DIM])
    v_lo = 2 * GLA_KEY_DIM
    v_ref[...] = _dot(h, w_ref[:, v_lo:v_lo + GLA_VAL_DIM]).astype(BF16)
    g_lo = v_lo + GLA_VAL_DIM
    g_ref[...] = _dot(h, w_ref[:, g_lo:g_lo + GLA_VAL_DIM])
    gate_lr = _dot(h, wg_ref[...].astype(BF16)).astype(BF16)
    gk = _dot(gate_lr, wgu_ref[...].astype(BF16)) + bg_ref[...]
    log_sig = jnp.minimum(gk, 0.0) - jnp.log(1.0 + jnp.exp(-jnp.abs(gk)))
    la_ref[...] = log_sig * (1.0 / GLA_GATE_NORMALIZER)


def _gla_in_project(x, nw, w_main, w_gate, w_gate_up, b_gate, rows):
    B, S, D = x.shape
    T = B * S
    row_spec = lambda width: pl.BlockSpec((rows, width), lambda i: (i, 0))
    const = lambda shape: pl.BlockSpec(shape, lambda i: (0, 0))
    n_main = 2 * GLA_KEY_DIM + 2 * GLA_VAL_DIM
    return pl.pallas_call(
        _gla_in_kernel,
        grid=(T // rows,),
        in_specs=[row_spec(D), const((1, D)), const((D, n_main)), const((D, GLA_GATE_RANK)),
                  const((GLA_GATE_RANK, GLA_KEY_DIM)), const((1, GLA_KEY_DIM))],
        out_specs=[row_spec(GLA_KEY_DIM), row_spec(GLA_KEY_DIM), row_spec(GLA_VAL_DIM),
                   row_spec(GLA_VAL_DIM), row_spec(GLA_KEY_DIM)],
        out_shape=[
            jax.ShapeDtypeStruct((T, GLA_KEY_DIM), F32),
            jax.ShapeDtypeStruct((T, GLA_KEY_DIM), F32),
            jax.ShapeDtypeStruct((T, GLA_VAL_DIM), BF16),
            jax.ShapeDtypeStruct((T, GLA_VAL_DIM), F32),
            jax.ShapeDtypeStruct((T, GLA_KEY_DIM), F32),
        ],
        compiler_params=_params("parallel"),
        name="gla_in",
    )(x.reshape(T, D), nw, w_main, w_gate, w_gate_up, b_gate)


def _gla_kernel(q_ref, k_ref, v_ref, g_ref, la_ref, nw_ref, o_ref, state_ref, *, n_chunks):
    @pl.when(pl.program_id(1) == 0)
    def _():
        state_ref[...] = jnp.zeros_like(state_ref)

    pair = 2 * GLA_CHUNK
    ri = lax.broadcasted_iota(jnp.int32, (pair, pair), 0)
    cj = lax.broadcasted_iota(jnp.int32, (pair, pair), 1)
    causal = ri >= cj
    cumsum_mat = (causal & ((ri >= GLA_CHUNK) == (cj >= GLA_CHUNK))).astype(BF16)
    first = lax.broadcasted_iota(jnp.int32, (pair, GLA_KEY_DIM), 0) < GLA_CHUNK
    ones = jnp.ones((pair, LANES), BF16)
    nw = nw_ref[...]
    for p in range(n_chunks // 2):
        rows = slice(p * pair, (p + 1) * pair)
        la = la_ref[rows, :]
        la_hi = la.astype(BF16)
        la_lo = (la - la_hi.astype(F32)).astype(BF16)
        b = _dot(cumsum_mat, la_hi) + _dot(cumsum_mat, la_lo)
        bl0 = b[GLA_CHUNK - 1:GLA_CHUNK, :]
        bl1 = b[pair - 1:pair, :]
        ref_b = jnp.where(first, b - bl0, b)
        q_e = q_ref[rows, :] * jnp.exp(ref_b)
        k_e = k_ref[rows, :] * jnp.exp(-ref_b)
        scale = GLA_DK ** -0.5
        q_sc = (q_e * scale).astype(BF16)
        k_sc = k_e.astype(BF16)
        q_st = (q_e * (jnp.exp(bl0) * scale)).astype(BF16)
        k_up = (k_e * jnp.exp(bl1)).astype(BF16)
        for h in range(GLA_HEADS):
            kc = slice(h * GLA_DK, (h + 1) * GLA_DK)
            vc = slice(h * GLA_DV, (h + 1) * GLA_DV)
            v = v_ref[rows, vc]
            a = jnp.where(causal, _dot_nt(q_sc[:, kc], k_sc[:, kc]), 0.0).astype(BF16)
            state = state_ref[h]
            o = _dot(jnp.concatenate([a, q_st[:, kc]], axis=1),
                     jnp.concatenate([v, state.astype(BF16)], axis=0))
            bl_t = _dot_tn(la_hi[:, kc], ones) + _dot_tn(la_lo[:, kc], ones)
            decay = jnp.exp(bl_t)
            decay = jnp.concatenate([decay] * (GLA_DV // LANES), axis=1)
            state_ref[h] = decay * state + _dot_tn(k_up[:, kc], v)
            var = jnp.mean(o * o, axis=-1, keepdims=True)
            o = o * lax.rsqrt(var + NORM_EPS) * nw
            g = g_ref[rows, vc]
            half_g = 0.5 * g
            o_ref[rows, vc] = (o * (half_g * (1.0 + jnp.tanh(half_g)))).astype(BF16)


def _gla_recurrence(q, k, v, g, la, nw, B, S, rows):
    T = B * S
    n_blocks = S // rows
    row_spec = lambda width: pl.BlockSpec((rows, width), lambda b, i: (b * n_blocks + i, 0))
    return pl.pallas_call(
        partial(_gla_kernel, n_chunks=rows // GLA_CHUNK),
        grid=(B, n_blocks),
        in_specs=[row_spec(GLA_KEY_DIM), row_spec(GLA_KEY_DIM), row_spec(GLA_VAL_DIM),
                  row_spec(GLA_VAL_DIM), row_spec(GLA_KEY_DIM),
                  pl.BlockSpec((1, GLA_DV), lambda b, i: (0, 0))],
        out_specs=row_spec(GLA_VAL_DIM),
        out_shape=jax.ShapeDtypeStruct((T, GLA_VAL_DIM), BF16),
        scratch_shapes=[pltpu.VMEM((GLA_HEADS, GLA_DK, GLA_DV), F32)],
        compiler_params=_params("parallel", "arbitrary"),
        name="gla_recurrence",
    )(q, k, v, g, la, nw)


def _rope_tables(seq_len):
    pos = np.arange(seq_len, dtype=np.float64)
    inv_freq = ROPE_THETA ** (-np.arange(0, HEAD_DIM, 2, dtype=np.float64) / HEAD_DIM)
    ang = pos[:, None] * inv_freq[None, :]
    cos, sin = np.cos(ang), np.sin(ang)
    cos2 = np.concatenate([cos, cos], axis=-1).astype(np.float32)
    sin2 = np.concatenate([-sin, sin], axis=-1).astype(np.float32)
    return jnp.asarray(cos2), jnp.asarray(sin2)


def kernel(x, norm_mix_w, norm_mlp_w, final_norm_w, attn_w_in, attn_w_out, gla_w_in,
           gla_w_gate_up, gla_b_gate, gla_norm_w, gla_w_out, mlp_w_up, mlp_w_down):
    B, S, D = x.shape
    row = lambda w: w.reshape(1, -1).astype(F32)

    cos2, sin2 = _rope_tables(S)
    qkv, (w_out0, w_up0, w_down0) = _qkv_project(
        x, row(norm_mix_w[0]), attn_w_in[0].astype(BF16), cos2, sin2,
        riders=[(attn_w_out, 0, D), (mlp_w_up, 0, D_FF), (mlp_w_down, 0, D)], rows=RESIDENT_ROWS)
    outs, lses = [], []
    for g, (window, dil) in enumerate(DILATED_GROUPS):
        assert window // dil == ATTN_BLOCK and (S // dil) % ATTN_BLOCK == 0
        q, k, v = qkv[3 * g:3 * g + 3]
        o, lse = _group_attention(q, k, v, step_rows=STREAM_ROWS)
        outs.append(o)
        lses.append(lse)
    n_main = 2 * GLA_KEY_DIM + 2 * GLA_VAL_DIM
    x, (w_gla, w_out1, w_up1, w_down1) = _attn_tail(
        outs, lses, x, w_out0, row(norm_mlp_w[0]), w_up0, w_down0,
        riders=[(gla_w_in, 0, n_main), (gla_w_out, 0, D), (mlp_w_up, 1, D_FF), (mlp_w_down, 1, D)],
        rows=RESIDENT_ROWS)

    q, k, v, g, la = _gla_in_project(x, row(norm_mix_w[1]), w_gla, gla_w_in[0][:, n_main:],
                                     gla_w_gate_up[0], row(gla_b_gate[0]), rows=STREAM_ROWS)
    o = _gla_recurrence(q, k, v, g, la, row(gla_norm_w[0]), B, S, rows=RESIDENT_ROWS)
    return _gla_tail(o, x, w_out1, row(norm_mlp_w[1]), w_up1, w_down1, row(final_norm_w),
                     rows=RESIDENT_ROWS)
```

```python
from functools import partial

import jax
import jax.numpy as jnp
import numpy as np
from jax import lax
from jax.experimental import pallas as pl
from jax.experimental.pallas import tpu as pltpu

D_MODEL = 1024
NORM_EPS = 1e-5
DILATED_GROUPS = ((128, 1), (512, 4), (2048, 16))
ATTN_HEADS = 8
HEAD_DIM = 128
ATTN_WIDTH = ATTN_HEADS * HEAD_DIM
ATTN_BLOCK = 128
ROPE_THETA = 10000.0
GLA_HEADS = 4
GLA_DK = 128
GLA_DV = 256
GLA_KEY_DIM = GLA_HEADS * GLA_DK
GLA_VAL_DIM = GLA_HEADS * GLA_DV
GLA_GATE_RANK = 16
GLA_GATE_NORMALIZER = 16.0
GLA_CHUNK = 64
D_FF = 4 * D_MODEL

LANES = 128
VMEM_LIMIT_BYTES = 56 * 1024 * 1024
RESIDENT_ROWS = 512
STREAM_ROWS = 1024
ATTN_STEP_ROWS = 2048
NEG_BIG = -1e30

BF16 = jnp.bfloat16
F32 = jnp.float32


def _params(*semantics):
    return pltpu.CompilerParams(dimension_semantics=semantics,
                                vmem_limit_bytes=VMEM_LIMIT_BYTES)


def _rmsnorm_rows(x, w):
    var = jnp.mean(x * x, axis=-1, keepdims=True)
    return x * lax.rsqrt(var + NORM_EPS) * w


def _dot(a, b):
    return jnp.dot(a, b, preferred_element_type=F32)


def _dot_nt(a, b):
    return lax.dot_general(a, b, (((1,), (1,)), ((), ())), preferred_element_type=F32)


def _dot_tn(a, b):
    return lax.dot_general(a, b, (((0,), (0,)), ((), ())), preferred_element_type=F32)


def _cast_blocks(src_refs, dst_refs):
    for src_ref, dst_ref in zip(src_refs, dst_refs):
        dst_ref[...] = src_ref[:, :dst_ref.shape[1]].astype(BF16)


def _rider_specs(riders, n_steps):
    in_specs, out_specs, out_shapes = [], [], []
    for w, layer, n_keep in riders:
        _, k, n = w.shape
        assert k % n_steps == 0 and (k // n_steps) % 16 == 0
        blk = k // n_steps
        index = lambda s, layer=layer: (layer, jnp.minimum(s, n_steps - 1), 0)
        in_specs.append(pl.BlockSpec((None, blk, n), index))
        out_specs.append(pl.BlockSpec((blk, n_keep), lambda s: (jnp.minimum(s, n_steps - 1), 0)))
        out_shapes.append(jax.ShapeDtypeStruct((k, n_keep), BF16))
    return in_specs, out_specs, out_shapes


QKV_CHUNK = 256
Q_SCALE = HEAD_DIM ** -0.5 * float(np.log2(np.e))


def _qkv_kernel(*refs, n_riders):
    n_slabs = D_MODEL // LANES
    n_out = 3 * len(DILATED_GROUPS)
    x_refs = refs[:n_slabs]
    nw_ref, w_ref, cos_ref, sin_ref = refs[n_slabs:n_slabs + 4]
    rider_in = refs[n_slabs + 4:n_slabs + 4 + n_riders]
    out_refs = refs[n_slabs + 4 + n_riders:n_slabs + 4 + n_riders + n_out]
    rider_out = refs[n_slabs + 4 + n_riders + n_out:-1]
    tab_ref = refs[-1]
    rows = x_refs[0].shape[0]
    n_chunks = ATTN_WIDTH // QKV_CHUNK
    row_blocks = rows // ATTN_BLOCK
    _cast_blocks(rider_in, rider_out)

    def residue_major(src_ref, dil):
        if dil == 1:
            return src_ref[...]
        n = rows // dil
        return jnp.concatenate([src_ref[pl.ds(r, n, stride=dil), :] for r in range(dil)], axis=0)

    for g, (_, dil) in enumerate(DILATED_GROUPS):
        n = rows // dil
        xg = jnp.concatenate([residue_major(x_ref, dil) for x_ref in x_refs], axis=1)
        h = _rmsnorm_rows(xg, nw_ref[...]).astype(BF16)
        if dil == 1:
            cos_src, sin_src = cos_ref, sin_ref
        else:
            tab_ref[0] = residue_major(cos_ref, dil)
            tab_ref[1] = residue_major(sin_ref, dil)
            cos_src, sin_src = tab_ref.at[0], tab_ref.at[1]
        for part in range(3):
            o_ref = out_refs[3 * g + part]
            for j in range(n_chunks):
                col = (3 * g + part) * ATTN_WIDTH + j * QKV_CHUNK
                acc = _dot(h, w_ref[:, col:col + QKV_CHUNK])
                for half in range(QKV_CHUNK // HEAD_DIM):
                    lo = j * QKV_CHUNK + half * HEAD_DIM
                    for rb in range(row_blocks):
                        rs = slice(rb * ATTN_BLOCK, (rb + 1) * ATTN_BLOCK)
                        t = acc[rs, half * HEAD_DIM:(half + 1) * HEAD_DIM]
                        if part < 2:
                            t = t * cos_src[rs, :] + pltpu.roll(t, HEAD_DIM // 2, 1) * sin_src[rs, :]
                        if part == 0:
                            t = t * Q_SCALE
                        t = t.astype(BF16)
                        if n >= ATTN_BLOCK:
                            first = rb * ATTN_BLOCK
                            o_ref[first // n, pl.ds(first % n, ATTN_BLOCK), lo:lo + HEAD_DIM] = t
                        else:
                            for k in range(ATTN_BLOCK // n):
                                o_ref[rb * (ATTN_BLOCK // n) + k, :, lo:lo + HEAD_DIM] = (
                                    t[k * n:(k + 1) * n])


def _qkv_project(x, nw, w_in, cos2, sin2, riders, rows):
    B, S, D = x.shape
    n_blocks = S // rows
    out_shapes, out_specs = [], []
    for _, dil in DILATED_GROUPS:
        for _ in range(3):
            out_shapes.append(jax.ShapeDtypeStruct((B, dil, S // dil, ATTN_WIDTH), BF16))
            out_specs.append(pl.BlockSpec((None, dil, rows // dil, ATTN_WIDTH),
                                          lambda i: (i // n_blocks, 0, i % n_blocks, 0)))
    rider_in, rider_out, rider_shapes = _rider_specs(riders, B * n_blocks)
    tab_spec = pl.BlockSpec((rows, HEAD_DIM), lambda i: (i % n_blocks, 0))
    const = lambda shape: pl.BlockSpec(shape, lambda i: (0, 0), pipeline_mode=pl.Buffered(1))
    xf = x.reshape(B * S, D)
    slab_specs = [pl.BlockSpec((rows, LANES), lambda i, c=c: (i, c)) for c in range(D // LANES)]
    outs = pl.pallas_call(
        partial(_qkv_kernel, n_riders=len(riders)),
        grid=(B * n_blocks,),
        in_specs=slab_specs + [const((1, D)), const((D, w_in.shape[1])), tab_spec, tab_spec]
        + rider_in,
        out_specs=out_specs + rider_out,
        out_shape=out_shapes + rider_shapes,
        scratch_shapes=[pltpu.VMEM((2, rows, HEAD_DIM), F32)],
        compiler_params=_params("parallel"),
        name="attn_qkv",
    )(*([xf] * (D // LANES)), nw, w_in, cos2, sin2, *[w for w, _, _ in riders])
    return outs[:len(out_shapes)], outs[len(out_shapes):]


def _band_bias():
    qi = np.arange(ATTN_BLOCK)[:, None]
    kj = np.arange(2 * ATTN_BLOCK)[None, :]
    rel = qi - kj + ATTN_BLOCK
    band = (rel >= 0) & (rel <= ATTN_BLOCK)
    masks = np.stack([band & (kj >= ATTN_BLOCK), band])
    return jnp.asarray(np.where(masks, 0.0, NEG_BIG).astype(np.float32))


def _attn_kernel(q_ref, k_ref, v_ref, kh_ref, vh_ref, bias_ref, o_ref, lse_ref):
    first_bias = jnp.where(pl.program_id(2) > 0, 1, 0)
    lane = lax.broadcasted_iota(jnp.int32, (ATTN_BLOCK, LANES), 1)
    n_res, n_rows, _ = q_ref.shape
    for r in range(n_res):
        for c in range(n_rows // ATTN_BLOCK):
            rows = slice(c * ATTN_BLOCK, (c + 1) * ATTN_BLOCK)
            bias = bias_ref.at[first_bias if c == 0 else 1]
            lse_tile = jnp.zeros((ATTN_BLOCK, LANES), F32)
            for h in range(ATTN_HEADS):
                cols = slice(h * HEAD_DIM, (h + 1) * HEAD_DIM)
                q = q_ref[r, rows, cols]
                if c == 0:
                    kk = jnp.concatenate([kh_ref[r, :, cols], k_ref[r, rows, cols]], axis=0)
                    vv = jnp.concatenate([vh_ref[r, :, cols], v_ref[r, rows, cols]], axis=0)
                else:
                    hist = slice((c - 1) * ATTN_BLOCK, (c + 1) * ATTN_BLOCK)
                    kk = k_ref[r, hist, cols]
                    vv = v_ref[r, hist, cols]
                s = _dot_nt(q, kk) + bias[...]
                m = jnp.max(s, axis=-1, keepdims=True)
                p = jnp.exp2(s - m)
                l = jnp.sum(p, axis=-1, keepdims=True)
                o = _dot(p.astype(BF16), vv) / l
                o_ref[r, rows, cols] = o.astype(BF16)
                lse_tile = jnp.where(lane == h, m + jnp.log2(l), lse_tile)
            lse_ref[r, rows, :] = lse_tile


def _group_attention(q, k, v, step_rows):
    B, dil, L, W = q.shape
    rows = min(step_rows, L)
    n_res = min(step_rows // rows, dil)
    n_sub = rows // ATTN_BLOCK
    blk = lambda width: pl.BlockSpec((None, n_res, rows, width), lambda b, r, i: (b, r, i, 0))
    halo = pl.BlockSpec((None, n_res, ATTN_BLOCK, W),
                        lambda b, r, i: (b, r, jnp.maximum(i * n_sub - 1, 0), 0))
    return pl.pallas_call(
        _attn_kernel,
        grid=(B, dil // n_res, L // rows),
        in_specs=[blk(W), blk(W), blk(W), halo, halo,
                  pl.BlockSpec((2, ATTN_BLOCK, 2 * ATTN_BLOCK), lambda b, r, i: (0, 0, 0))],
        out_specs=[blk(W), blk(LANES)],
        out_shape=[
            jax.ShapeDtypeStruct((B, dil, L, W), BF16),
            jax.ShapeDtypeStruct((B, dil, L, LANES), F32),
        ],
        compiler_params=_params("parallel", "parallel", "parallel"),
        name="attn_band",
    )(q, k, v, k, v, _band_bias())


MLP_FF_CHUNK = 1024


def _project_and_mlp(mixed, x_ref, wo_ref, nw_ref, wu_ref, wd_ref, side_work=None):
    x_mid = x_ref[...] + _dot(mixed, wo_ref[...])
    h = _rmsnorm_rows(x_mid, nw_ref[...]).astype(BF16)
    acc = x_mid
    for f in range(D_FF // MLP_FF_CHUNK):
        cols = slice(f * MLP_FF_CHUNK, (f + 1) * MLP_FF_CHUNK)
        a = jnp.maximum(_dot(h, wu_ref[:, cols]), 0.0)
        acc = acc + _dot((a * a).astype(BF16), wd_ref[cols, :])
        if side_work is not None:
            side_work(f)
    return acc


def _attn_tail_kernel(*refs, n_riders):
    n_groups = len(DILATED_GROUPS)
    o_refs = [refs[g * ATTN_HEADS:(g + 1) * ATTN_HEADS] for g in range(n_groups)]
    lse_refs = refs[n_groups * ATTN_HEADS:n_groups * (ATTN_HEADS + 1)]
    rest = refs[n_groups * (ATTN_HEADS + 1):]
    x_ref, wo_ref, nw_ref, wu_ref, wd_ref = rest[:5]
    rider_in = rest[5:5 + n_riders]
    out_ref = rest[5 + n_riders]
    rider_out = rest[6 + n_riders:6 + 2 * n_riders]
    merged_ref, nat_ref = rest[6 + 2 * n_riders:]
    rows = x_ref.shape[0]
    _cast_blocks(rider_in, rider_out)
    step = pl.program_id(0)
    write_slot = step % 2
    read_slot = 1 - write_slot

    @pl.when(step == 0)
    def _():
        merged_ref[1] = jnp.zeros(merged_ref.shape[1:], BF16)

    def natural(src_ref, dil, slot):
        if dil == 1:
            return src_ref[0].astype(F32)
        for r in range(dil):
            nat_ref[slot, pl.ds(r, rows // dil, stride=dil), :] = src_ref[r].astype(F32)
        return nat_ref[slot]

    mixed = merged_ref[read_slot]
    heads_per_chunk = ATTN_HEADS // (D_FF // MLP_FF_CHUNK)
    alphas = []

    def merge_heads(f):
        if not alphas:
            lses = [natural(lse_refs[g], dil, g) for g, (_, dil) in enumerate(DILATED_GROUPS)]
            m = jnp.maximum(jnp.maximum(lses[0], lses[1]), lses[2])
            es = [jnp.exp2(l - m) for l in lses]
            inv = 1.0 / (es[0] + es[1] + es[2])
            alphas.extend(e * inv for e in es)
        for h in range(f * heads_per_chunk, (f + 1) * heads_per_chunk):
            o = None
            for g, (_, dil) in enumerate(DILATED_GROUPS):
                slot = n_groups + 2 * g + h % 2
                term = alphas[g][:, h:h + 1] * natural(o_refs[g][h], dil, slot)
                o = term if o is None else o + term
            merged_ref[write_slot, :, h * HEAD_DIM:(h + 1) * HEAD_DIM] = o.astype(BF16)

    out_ref[...] = _project_and_mlp(mixed, x_ref, wo_ref, nw_ref, wu_ref, wd_ref,
                                    side_work=merge_heads)


def _attn_tail(o_list, lse_list, x, w_out, nw, w_up, w_down, riders, rows):
    B, S, D = x.shape
    W = ATTN_WIDTH
    n_blocks = S // rows
    n_groups = len(DILATED_GROUPS)
    n_tiles = B * n_blocks

    def residue_spec(dil, col):
        def index(s):
            t = jnp.minimum(s, n_tiles - 1)
            return (t // n_blocks, 0, t % n_blocks, col)
        return pl.BlockSpec((None, dil, rows // dil, LANES), index)

    o_specs = [residue_spec(dil, h) for _, dil in DILATED_GROUPS for h in range(ATTN_HEADS)]
    o_args = [o for o in o_list for _ in range(ATTN_HEADS)]
    lse_specs = [residue_spec(dil, 0) for _, dil in DILATED_GROUPS]
    rider_in, rider_out, rider_shapes = _rider_specs(riders, n_tiles)
    row_spec = pl.BlockSpec((rows, D), lambda s: (jnp.maximum(s - 1, 0), 0))
    const = lambda shape: pl.BlockSpec(shape, lambda s: (0, 0), pipeline_mode=pl.Buffered(1))
    outs = pl.pallas_call(
        partial(_attn_tail_kernel, n_riders=len(riders)),
        grid=(n_tiles + 1,),
        in_specs=o_specs + lse_specs + [row_spec, const((W, D)), const((1, D)),
                                        const((D, D_FF)), const((D_FF, D))] + rider_in,
        out_specs=[row_spec] + rider_out,
        out_shape=[jax.ShapeDtypeStruct((B * S, D), F32)] + rider_shapes,
        scratch_shapes=[pltpu.VMEM((2, rows, W), BF16),
                        pltpu.VMEM((3 * n_groups, rows, LANES), F32)],
        compiler_params=_params("arbitrary"),
        name="attn_tail",
    )(*o_args, *lse_list, x.reshape(B * S, D), w_out, nw, w_up, w_down,
      *[w for w, _, _ in riders])
    return outs[0].reshape(B, S, D), outs[1:]


def _gla_tail_kernel(o_ref, x_ref, wo_ref, nw_ref, wu_ref, wd_ref, nwf_ref, out_ref):
    acc = _project_and_mlp(o_ref[...], x_ref, wo_ref, nw_ref, wu_ref, wd_ref)
    out_ref[...] = _rmsnorm_rows(acc, nwf_ref[...])


def _gla_tail(o, x, w_out, nw, w_up, w_down, nw_final, rows):
    B, S, D = x.shape
    T = B * S
    row_spec = lambda width: pl.BlockSpec((rows, width), lambda i: (i, 0))
    const = lambda shape: pl.BlockSpec(shape, lambda i: (0, 0), pipeline_mode=pl.Buffered(1))
    out = pl.pallas_call(
        _gla_tail_kernel,
        grid=(T // rows,),
        in_specs=[row_spec(o.shape[-1]), row_spec(D), const((o.shape[-1], D)), const((1, D)),
                  const((D, D_FF)), const((D_FF, D)), const((1, D))],
        out_specs=row_spec(D),
        out_shape=jax.ShapeDtypeStruct((T, D), F32),
        compiler_params=_params("parallel"),
        name="gla_tail",
    )(o, x.reshape(T, D), w_out, nw, w_up, w_down, nw_final)
    return out.reshape(B, S, D)


def _gla_in_kernel(x_ref, nw_ref, w_ref, wg_ref, wgu_ref, bg_ref,
                   q_ref, k_ref, v_ref, g_ref, la_ref):
    h = _rmsnorm_rows(x_ref[...], nw_ref[...]).astype(BF16)
    q_ref[...] = _dot(h, w_ref[:, :GLA_KEY_DIM])
    k_ref[...] = _dot(h, w_ref[:, GLA_KEY_DIM:2 * GLA_KEY_DIM])
    v_lo = 2 * GLA_KEY_DIM
    v_ref[...] = _dot(h, w_ref[:, v_lo:v_lo + GLA_VAL_DIM]).astype(BF16)
    g_lo = v_lo + GLA_VAL_DIM
    g_ref[...] = _dot(h, w_ref[:, g_lo:g_lo + GLA_VAL_DIM])
    gate_lr = _dot(h, wg_ref[...].astype(BF16)).astype(BF16)
    gk = _dot(gate_lr, wgu_ref[...].astype(BF16)) + bg_ref[...]
    log_sig = jnp.minimum(gk, 0.0) - jnp.log(1.0 + jnp.exp(-jnp.abs(gk)))
    la_ref[...] = log_sig * (1.0 / GLA_GATE_NORMALIZER)


def _gla_in_project(x, nw, w_main, w_gate, w_gate_up, b_gate, rows):
    B, S, D = x.shape
    T = B * S
    row_spec = lambda width: pl.BlockSpec((rows, width), lambda i: (i, 0))
    const = lambda shape: pl.BlockSpec(shape, lambda i: (0, 0))
    n_main = 2 * GLA_KEY_DIM + 2 * GLA_VAL_DIM
    return pl.pallas_call(
        _gla_in_kernel,
        grid=(T // rows,),
        in_specs=[row_spec(D), const((1, D)), const((D, n_main)), const((D, GLA_GATE_RANK)),
                  const((GLA_GATE_RANK, GLA_KEY_DIM)), const((1, GLA_KEY_DIM))],
        out_specs=[row_spec(GLA_KEY_DIM), row_spec(GLA_KEY_DIM), row_spec(GLA_VAL_DIM),
                   row_spec(GLA_VAL_DIM), row_spec(GLA_KEY_DIM)],
        out_shape=[
            jax.ShapeDtypeStruct((T, GLA_KEY_DIM), F32),
            jax.ShapeDtypeStruct((T, GLA_KEY_DIM), F32),
            jax.ShapeDtypeStruct((T, GLA_VAL_DIM), BF16),
            jax.ShapeDtypeStruct((T, GLA_VAL_DIM), F32),
            jax.ShapeDtypeStruct((T, GLA_KEY_DIM), F32),
        ],
        compiler_params=_params("parallel"),
        name="gla_in",
    )(x.reshape(T, D), nw, w_main, w_gate, w_gate_up, b_gate)


def _gla_kernel(q_ref, k_ref, v_ref, g_ref, la_ref, nw_ref, o_ref, state_ref, *, n_chunks):
    @pl.when(pl.program_id(1) == 0)
    def _():
        state_ref[...] = jnp.zeros_like(state_ref)

    pair = 2 * GLA_CHUNK
    ri = lax.broadcasted_iota(jnp.int32, (pair, pair), 0)
    cj = lax.broadcasted_iota(jnp.int32, (pair, pair), 1)
    causal = ri >= cj
    cumsum_mat = (causal & ((ri >= GLA_CHUNK) == (cj >= GLA_CHUNK))).astype(BF16)
    first = lax.broadcasted_iota(jnp.int32, (pair, GLA_KEY_DIM), 0) < GLA_CHUNK
    ones = jnp.ones((pair, LANES), BF16)
    nw = nw_ref[...]
    for p in range(n_chunks // 2):
        rows = slice(p * pair, (p + 1) * pair)
        la = la_ref[rows, :]
        la_hi = la.astype(BF16)
        la_lo = (la - la_hi.astype(F32)).astype(BF16)
        b = _dot(cumsum_mat, la_hi) + _dot(cumsum_mat, la_lo)
        bl0 = b[GLA_CHUNK - 1:GLA_CHUNK, :]
        bl1 = b[pair - 1:pair, :]
        ref_b = jnp.where(first, b - bl0, b)
        q_e = q_ref[rows, :] * jnp.exp(ref_b)
        k_e = k_ref[rows, :] * jnp.exp(-ref_b)
        scale = GLA_DK ** -0.5
        q_sc = (q_e * scale).astype(BF16)
        k_sc = k_e.astype(BF16)
        q_st = (q_e * (jnp.exp(bl0) * scale)).astype(BF16)
        k_up = (k_e * jnp.exp(bl1)).astype(BF16)
        for h in range(GLA_HEADS):
            kc = slice(h * GLA_DK, (h + 1) * GLA_DK)
            vc = slice(h * GLA_DV, (h + 1) * GLA_DV)
            v = v_ref[rows, vc]
            a = jnp.where(causal, _dot_nt(q_sc[:, kc], k_sc[:, kc]), 0.0).astype(BF16)
            state = state_ref[h]
            o = _dot(jnp.concatenate([a, q_st[:, kc]], axis=1),
                     jnp.concatenate([v, state.astype(BF16)], axis=0))
            bl_t = _dot_tn(la_hi[:, kc], ones) + _dot_tn(la_lo[:, kc], ones)
            decay = jnp.exp(bl_t)
            decay = jnp.concatenate([decay] * (GLA_DV // LANES), axis=1)
            state_ref[h] = decay * state + _dot_tn(k_up[:, kc], v)
            var = jnp.mean(o * o, axis=-1, keepdims=True)
            o = o * lax.rsqrt(var + NORM_EPS) * nw
            g = g_ref[rows, vc]
            half_g = 0.5 * g
            o_ref[rows, vc] = (o * (half_g * (1.0 + jnp.tanh(half_g)))).astype(BF16)


def _gla_recurrence(q, k, v, g, la, nw, B, S, rows):
    T = B * S
    n_blocks = S // rows
    row_spec = lambda width: pl.BlockSpec((rows, width), lambda b, i: (b * n_blocks + i, 0))
    return pl.pallas_call(
        partial(_gla_kernel, n_chunks=rows // GLA_CHUNK),
        grid=(B, n_blocks),
        in_specs=[row_spec(GLA_KEY_DIM), row_spec(GLA_KEY_DIM), row_spec(GLA_VAL_DIM),
                  row_spec(GLA_VAL_DIM), row_spec(GLA_KEY_DIM),
                  pl.BlockSpec((1, GLA_DV), lambda b, i: (0, 0))],
        out_specs=row_spec(GLA_VAL_DIM),
        out_shape=jax.ShapeDtypeStruct((T, GLA_VAL_DIM), BF16),
        scratch_shapes=[pltpu.VMEM((GLA_HEADS, GLA_DK, GLA_DV), F32)],
        compiler_params=_params("parallel", "arbitrary"),
        name="gla_recurrence",
    )(q, k, v, g, la, nw)


def _rope_tables(seq_len):
    pos = np.arange(seq_len, dtype=np.float64)
    inv_freq = ROPE_THETA ** (-np.arange(0, HEAD_DIM, 2, dtype=np.float64) / HEAD_DIM)
    ang = pos[:, None] * inv_freq[None, :]
    cos, sin = np.cos(ang), np.sin(ang)
    cos2 = np.concatenate([cos, cos], axis=-1).astype(np.float32)
    sin2 = np.concatenate([-sin, sin], axis=-1).astype(np.float32)
    return jnp.asarray(cos2), jnp.asarray(sin2)


def kernel(x, norm_mix_w, norm_mlp_w, final_norm_w, attn_w_in, attn_w_out, gla_w_in,
           gla_w_gate_up, gla_b_gate, gla_norm_w, gla_w_out, mlp_w_up, mlp_w_down):
    B, S, D = x.shape
    row = lambda w: w.reshape(1, -1).astype(F32)

    cos2, sin2 = _rope_tables(S)
    qkv, (w_out0, w_up0, w_down0) = _qkv_project(
        x, row(norm_mix_w[0]), attn_w_in[0].astype(BF16), cos2, sin2,
        riders=[(attn_w_out, 0, D), (mlp_w_up, 0, D_FF), (mlp_w_down, 0, D)], rows=RESIDENT_ROWS)
    outs, lses = [], []
    for g, (window, dil) in enumerate(DILATED_GROUPS):
        assert window // dil == ATTN_BLOCK and (S // dil) % ATTN_BLOCK == 0
        q, k, v = qkv[3 * g:3 * g + 3]
        o, lse = _group_attention(q, k, v, step_rows=ATTN_STEP_ROWS)
        outs.append(o)
        lses.append(lse)
    n_main = 2 * GLA_KEY_DIM + 2 * GLA_VAL_DIM
    x, (w_gla, w_out1, w_up1, w_down1) = _attn_tail(
        outs, lses, x, w_out0, row(norm_mlp_w[0]), w_up0, w_down0,
        riders=[(gla_w_in, 0, n_main), (gla_w_out, 0, D), (mlp_w_up, 1, D_FF), (mlp_w_down, 1, D)],
        rows=RESIDENT_ROWS)

    q, k, v, g, la = _gla_in_project(x, row(norm_mix_w[1]), w_gla, gla_w_in[0][:, n_main:],
                                     gla_w_gate_up[0], row(gla_b_gate[0]), rows=STREAM_ROWS)
    o = _gla_recurrence(q, k, v, g, la, row(gla_norm_w[0]), B, S, rows=RESIDENT_ROWS)
    return _gla_tail(o, x, w_out1, row(norm_mlp_w[1]), w_up1, w_down1, row(final_norm_w),
                     rows=RESIDENT_ROWS)
```

```python
from functools import partial

import jax
import jax.numpy as jnp
import numpy as np
from jax import lax
from jax.experimental import pallas as pl
from jax.experimental.pallas import tpu as pltpu

D_MODEL = 1024
NORM_EPS = 1e-5
DILATED_GROUPS = ((128, 1), (512, 4), (2048, 16))
ATTN_HEADS = 8
HEAD_DIM = 128
ATTN_WIDTH = ATTN_HEADS * HEAD_DIM
ATTN_BLOCK = 128
ROPE_THETA = 10000.0
GLA_HEADS = 4
GLA_DK = 128
GLA_DV = 256
GLA_KEY_DIM = GLA_HEADS * GLA_DK
GLA_VAL_DIM = GLA_HEADS * GLA_DV
GLA_GATE_RANK = 16
GLA_GATE_NORMALIZER = 16.0
GLA_CHUNK = 64
D_FF = 4 * D_MODEL

LANES = 128
VMEM_LIMIT_BYTES = 56 * 1024 * 1024
BF16_SUBLANES = 16
RESIDENT_ROWS = 512
STREAM_ROWS = 1024
NEG_BIG = -1e30

BF16 = jnp.bfloat16
F32 = jnp.float32


def _params(*semantics):
    return pltpu.CompilerParams(dimension_semantics=semantics,
                                vmem_limit_bytes=VMEM_LIMIT_BYTES)


def _rmsnorm_rows(x, w):
    var = jnp.mean(x * x, axis=-1, keepdims=True)
    return x * lax.rsqrt(var + NORM_EPS) * w


def _dot(a, b):
    return jnp.dot(a, b, preferred_element_type=F32)


def _dot_nt(a, b):
    return lax.dot_general(a, b, (((1,), (1,)), ((), ())), preferred_element_type=F32)


def _dot_tn(a, b):
    return lax.dot_general(a, b, (((0,), (0,)), ((), ())), preferred_element_type=F32)


def _cast_blocks(src_refs, dst_refs):
    for src_ref, dst_ref in zip(src_refs, dst_refs):
        dst_ref[...] = src_ref[:, :dst_ref.shape[1]].astype(BF16)


def _rider_specs(riders, n_steps):
    in_specs, out_specs, out_shapes = [], [], []
    for w, layer, n_keep in riders:
        _, k, n = w.shape
        assert k % n_steps == 0 and (k // n_steps) % BF16_SUBLANES == 0
        blk = k // n_steps
        index = lambda s, layer=layer: (layer, jnp.minimum(s, n_steps - 1), 0)
        in_specs.append(pl.BlockSpec((None, blk, n), index))
        out_specs.append(pl.BlockSpec((blk, n_keep), lambda s: (jnp.minimum(s, n_steps - 1), 0)))
        out_shapes.append(jax.ShapeDtypeStruct((k, n_keep), BF16))
    return in_specs, out_specs, out_shapes


QKV_CHUNK = 256
Q_SCALE = HEAD_DIM ** -0.5 * float(np.log2(np.e))


def _qkv_kernel(*refs, n_riders):
    n_slabs = D_MODEL // LANES
    n_out = 3 * len(DILATED_GROUPS)
    x_refs = refs[:n_slabs]
    nw_ref, w_ref, cos_ref, sin_ref = refs[n_slabs:n_slabs + 4]
    rider_in = refs[n_slabs + 4:n_slabs + 4 + n_riders]
    out_refs = refs[n_slabs + 4 + n_riders:n_slabs + 4 + n_riders + n_out]
    rider_out = refs[n_slabs + 4 + n_riders + n_out:-2]
    tab_ref, perm_ref = refs[-2:]
    rows = x_refs[0].shape[0]
    n_chunks = ATTN_WIDTH // QKV_CHUNK
    row_blocks = rows // ATTN_BLOCK

    def residue_major(src_ref, src_dil, dil):
        if dil == src_dil:
            return src_ref[...]
        pieces = [src_ref[pl.ds((r % src_dil) * (rows // src_dil) + r // src_dil, rows // dil,
                                stride=dil // src_dil), :] for r in range(dil)]
        return jnp.concatenate(pieces, axis=0)

    prev_dil = 1
    for g, (_, dil) in enumerate(DILATED_GROUPS):
        n = rows // dil
        x_src = x_refs if prev_dil == 1 else [perm_ref.at[c] for c in range(n_slabs)]
        slabs = [residue_major(src, prev_dil, dil) for src in x_src]
        if 1 < dil < DILATED_GROUPS[-1][1]:
            for c in range(n_slabs):
                perm_ref[c] = slabs[c]
        h = _rmsnorm_rows(jnp.concatenate(slabs, axis=1), nw_ref[...]).astype(BF16)
        if dil == 1:
            cos_src, sin_src = cos_ref, sin_ref
        else:
            tab_src = (cos_ref, sin_ref) if prev_dil == 1 else (tab_ref.at[g - 2, 0],
                                                                tab_ref.at[g - 2, 1])
            for t in range(2):
                tab_ref[g - 1, t] = residue_major(tab_src[t], prev_dil, dil)
            cos_src, sin_src = tab_ref.at[g - 1, 0], tab_ref.at[g - 1, 1]
        if dil > 1:
            prev_dil = dil
        for part in range(3):
            o_ref = out_refs[3 * g + part]
            for j in range(n_chunks):
                col = (3 * g + part) * ATTN_WIDTH + j * QKV_CHUNK
                acc = _dot(h, w_ref[:, col:col + QKV_CHUNK])
                for half in range(QKV_CHUNK // HEAD_DIM):
                    lo = j * QKV_CHUNK + half * HEAD_DIM
                    for rb in range(row_blocks):
                        rs = slice(rb * ATTN_BLOCK, (rb + 1) * ATTN_BLOCK)
                        t = acc[rs, half * HEAD_DIM:(half + 1) * HEAD_DIM]
                        if part < 2:
                            t = t * cos_src[rs, :] + pltpu.roll(t, HEAD_DIM // 2, 1) * sin_src[rs, :]
                        if part == 0:
                            t = t * Q_SCALE
                        t = t.astype(BF16)
                        if n >= ATTN_BLOCK:
                            first = rb * ATTN_BLOCK
                            o_ref[first // n, pl.ds(first % n, ATTN_BLOCK), lo:lo + HEAD_DIM] = t
                        else:
                            for k in range(ATTN_BLOCK // n):
                                o_ref[rb * (ATTN_BLOCK // n) + k, :, lo:lo + HEAD_DIM] = (
                                    t[k * n:(k + 1) * n])
    _cast_blocks(rider_in, rider_out)


def _qkv_project(x, nw, w_in, cos2, sin2, riders, rows):
    B, S, D = x.shape
    n_blocks = S // rows
    out_shapes, out_specs = [], []
    for _, dil in DILATED_GROUPS:
        for _ in range(3):
            out_shapes.append(jax.ShapeDtypeStruct((B, dil, S // dil, ATTN_WIDTH), BF16))
            out_specs.append(pl.BlockSpec((None, dil, rows // dil, ATTN_WIDTH),
                                          lambda i: (i // n_blocks, 0, i % n_blocks, 0)))
    rider_in, rider_out, rider_shapes = _rider_specs(riders, B * n_blocks)
    tab_spec = pl.BlockSpec((rows, HEAD_DIM), lambda i: (i % n_blocks, 0))
    const = lambda shape: pl.BlockSpec(shape, lambda i: (0, 0), pipeline_mode=pl.Buffered(1))
    xf = x.reshape(B * S, D)
    slab_specs = [pl.BlockSpec((rows, LANES), lambda i, c=c: (i, c)) for c in range(D // LANES)]
    outs = pl.pallas_call(
        partial(_qkv_kernel, n_riders=len(riders)),
        grid=(B * n_blocks,),
        in_specs=slab_specs + [const((1, D)), const((D, w_in.shape[1])), tab_spec, tab_spec]
        + rider_in,
        out_specs=out_specs + rider_out,
        out_shape=out_shapes + rider_shapes,
        scratch_shapes=[pltpu.VMEM((len(DILATED_GROUPS) - 1, 2, rows, HEAD_DIM), F32),
                        pltpu.VMEM((D // LANES, rows, LANES), F32)],
        compiler_params=_params("parallel"),
        name="attn_qkv",
    )(*([xf] * (D // LANES)), nw, w_in, cos2, sin2, *[w for w, _, _ in riders])
    return outs[:len(out_shapes)], outs[len(out_shapes):]


def _band_bias():
    qi = np.arange(ATTN_BLOCK)[:, None]
    kj = np.arange(2 * ATTN_BLOCK)[None, :]
    rel = qi - kj + ATTN_BLOCK
    band = (rel >= 0) & (rel <= ATTN_BLOCK)
    masks = np.stack([band & (kj >= ATTN_BLOCK), band])
    return jnp.asarray(np.where(masks, 0.0, NEG_BIG).astype(np.float32))


def _attn_kernel(q_ref, k_ref, v_ref, kh_ref, vh_ref, bias_ref, o_ref, lse_ref):
    first_bias = jnp.where(pl.program_id(2) > 0, 1, 0)
    lane = lax.broadcasted_iota(jnp.int32, (ATTN_BLOCK, LANES), 1)
    n_res, n_rows, _ = q_ref.shape
    for r in range(n_res):
        for c in range(n_rows // ATTN_BLOCK):
            rows = slice(c * ATTN_BLOCK, (c + 1) * ATTN_BLOCK)
            bias = bias_ref.at[first_bias if c == 0 else 1]
            lse_tile = jnp.zeros((ATTN_BLOCK, LANES), F32)
            for h in range(ATTN_HEADS):
                cols = slice(h * HEAD_DIM, (h + 1) * HEAD_DIM)
                q = q_ref[r, rows, cols]
                if c == 0:
                    kk = jnp.concatenate([kh_ref[r, :, cols], k_ref[r, rows, cols]], axis=0)
                    vv = jnp.concatenate([vh_ref[r, :, cols], v_ref[r, rows, cols]], axis=0)
                else:
                    hist = slice((c - 1) * ATTN_BLOCK, (c + 1) * ATTN_BLOCK)
                    kk = k_ref[r, hist, cols]
                    vv = v_ref[r, hist, cols]
                s = _dot_nt(q, kk) + bias[...]
                m = jnp.max(s, axis=-1, keepdims=True)
                p = jnp.exp2(s - m)
                l = jnp.sum(p, axis=-1, keepdims=True)
                o = _dot(p.astype(BF16), vv) / l
                o_ref[r, rows, cols] = o.astype(BF16)
                lse_tile = jnp.where(lane == h, m + jnp.log2(l), lse_tile)
            lse_ref[r, rows, :] = lse_tile


def _group_attention(q, k, v, step_rows):
    B, dil, L, W = q.shape
    rows = min(step_rows, L)
    n_res = min(step_rows // rows, dil)
    n_sub = rows // ATTN_BLOCK
    blk = lambda width: pl.BlockSpec((None, n_res, rows, width), lambda b, r, i: (b, r, i, 0))
    halo = pl.BlockSpec((None, n_res, ATTN_BLOCK, W),
                        lambda b, r, i: (b, r, jnp.maximum(i * n_sub - 1, 0), 0))
    return pl.pallas_call(
        _attn_kernel,
        grid=(B, dil // n_res, L // rows),
        in_specs=[blk(W), blk(W), blk(W), halo, halo,
                  pl.BlockSpec((2, ATTN_BLOCK, 2 * ATTN_BLOCK), lambda b, r, i: (0, 0, 0))],
        out_specs=[blk(W), blk(LANES)],
        out_shape=[
            jax.ShapeDtypeStruct((B, dil, L, W), BF16),
            jax.ShapeDtypeStruct((B, dil, L, LANES), F32),
        ],
        compiler_params=_params("parallel", "parallel", "parallel"),
        name="attn_band",
    )(q, k, v, k, v, _band_bias())


MLP_FF_CHUNK = 1024


def _project_and_mlp(mixed, x_ref, wo_ref, nw_ref, wu_ref, wd_ref, side_work=None):
    x_mid = x_ref[...] + _dot(mixed, wo_ref[...])
    h = _rmsnorm_rows(x_mid, nw_ref[...]).astype(BF16)
    acc = x_mid
    for f in range(D_FF // MLP_FF_CHUNK):
        cols = slice(f * MLP_FF_CHUNK, (f + 1) * MLP_FF_CHUNK)
        a = jnp.maximum(_dot(h, wu_ref[:, cols]), 0.0)
        acc = acc + _dot((a * a).astype(BF16), wd_ref[cols, :])
        if side_work is not None:
            side_work(f)
    return acc


def _attn_tail_kernel(*refs, n_riders):
    n_groups = len(DILATED_GROUPS)
    o_refs = refs[:n_groups]
    lse_refs = refs[n_groups:2 * n_groups]
    rest = refs[2 * n_groups:]
    x_ref, wo_ref, nw_ref, wu_ref, wd_ref = rest[:5]
    rider_in = rest[5:5 + n_riders]
    out_ref = rest[5 + n_riders]
    rider_out = rest[6 + n_riders:6 + 2 * n_riders]
    merged_ref, nat_ref = rest[6 + 2 * n_riders:]
    rows = x_ref.shape[0]
    step = pl.program_id(0)
    write_slot = step % 2
    read_slot = 1 - write_slot

    @pl.when(step == 0)
    def _():
        merged_ref[1] = jnp.zeros(merged_ref.shape[1:], BF16)

    def natural(src_ref, col, dil, slot):
        lanes = slice(col * LANES, (col + 1) * LANES)
        if dil == 1:
            return src_ref[0, :, lanes].astype(F32)
        for r in range(dil):
            nat_ref[slot, pl.ds(r, rows // dil, stride=dil), :] = src_ref[r, :, lanes].astype(F32)
        return nat_ref[slot]

    mixed = merged_ref[read_slot]
    heads_per_chunk = ATTN_HEADS // (D_FF // MLP_FF_CHUNK)
    alphas = []

    def merge_heads(f):
        if not alphas:
            lses = [natural(lse_refs[g], 0, dil, g) for g, (_, dil) in enumerate(DILATED_GROUPS)]
            m = jnp.maximum(jnp.maximum(lses[0], lses[1]), lses[2])
            es = [jnp.exp2(l - m) for l in lses]
            inv = 1.0 / (es[0] + es[1] + es[2])
            alphas.extend(e * inv for e in es)
        for h in range(f * heads_per_chunk, (f + 1) * heads_per_chunk):
            o = None
            for g, (_, dil) in enumerate(DILATED_GROUPS):
                slot = n_groups + 2 * g + h % 2
                term = alphas[g][:, h:h + 1] * natural(o_refs[g], h, dil, slot)
                o = term if o is None else o + term
            merged_ref[write_slot, :, h * HEAD_DIM:(h + 1) * HEAD_DIM] = o.astype(BF16)

    out_ref[...] = _project_and_mlp(mixed, x_ref, wo_ref, nw_ref, wu_ref, wd_ref,
                                    side_work=merge_heads)
    _cast_blocks(rider_in, rider_out)


def _attn_tail(o_list, lse_list, x, w_out, nw, w_up, w_down, riders, rows):
    B, S, D = x.shape
    W = ATTN_WIDTH
    n_blocks = S // rows
    n_groups = len(DILATED_GROUPS)
    n_tiles = B * n_blocks

    def residue_spec(dil, width):
        def index(s):
            t = jnp.minimum(s, n_tiles - 1)
            return (t // n_blocks, 0, t % n_blocks, 0)
        return pl.BlockSpec((None, dil, rows // dil, width), index)

    o_specs = [residue_spec(dil, W) for _, dil in DILATED_GROUPS]
    lse_specs = [residue_spec(dil, LANES) for _, dil in DILATED_GROUPS]
    rider_in, rider_out, rider_shapes = _rider_specs(riders, n_tiles)
    row_spec = pl.BlockSpec((rows, D), lambda s: (jnp.maximum(s - 1, 0), 0))
    const = lambda shape: pl.BlockSpec(shape, lambda s: (0, 0), pipeline_mode=pl.Buffered(1))
    outs = pl.pallas_call(
        partial(_attn_tail_kernel, n_riders=len(riders)),
        grid=(n_tiles + 1,),
        in_specs=o_specs + lse_specs + [row_spec, const((W, D)), const((1, D)),
                                        const((D, D_FF)), const((D_FF, D))] + rider_in,
        out_specs=[row_spec] + rider_out,
        out_shape=[jax.ShapeDtypeStruct((B * S, D), F32)] + rider_shapes,
        scratch_shapes=[pltpu.VMEM((2, rows, W), BF16),
                        pltpu.VMEM((3 * n_groups, rows, LANES), F32)],
        compiler_params=_params("arbitrary"),
        name="attn_tail",
    )(*o_list, *lse_list, x.reshape(B * S, D), w_out, nw, w_up, w_down,
      *[w for w, _, _ in riders])
    return outs[0].reshape(B, S, D), outs[1:]


def _gla_tail_kernel(o_ref, x_ref, wo_ref, nw_ref, wu_ref, wd_ref, nwf_ref, out_ref):
    acc = _project_and_mlp(o_ref[...], x_ref, wo_ref, nw_ref, wu_ref, wd_ref)
    out_ref[...] = _rmsnorm_rows(acc, nwf_ref[...])


def _gla_tail(o, x, w_out, nw, w_up, w_down, nw_final, rows):
    B, S, D = x.shape
    T = B * S
    row_spec = lambda width: pl.BlockSpec((rows, width), lambda i: (i, 0))
    const = lambda shape: pl.BlockSpec(shape, lambda i: (0, 0), pipeline_mode=pl.Buffered(1))
    out = pl.pallas_call(
        _gla_tail_kernel,
        grid=(T // rows,),
        in_specs=[row_spec(o.shape[-1]), row_spec(D), const((o.shape[-1], D)), const((1, D)),
                  const((D, D_FF)), const((D_FF, D)), const((1, D))],
        out_specs=row_spec(D),
        out_shape=jax.ShapeDtypeStruct((T, D), F32),
        compiler_params=_params("parallel"),
        name="gla_tail",
    )(o, x.reshape(T, D), w_out, nw, w_up, w_down, nw_final)
    return out.reshape(B, S, D)


def _gla_in_kernel(x_ref, nw_ref, w_ref, wg_ref, wgu_ref, bg_ref,
                   q_ref, k_ref, v_ref, g_ref, la_ref):
    h = _rmsnorm_rows(x_ref[...], nw_ref[...]).astype(BF16)
    q_ref[...] = _dot(h, w_ref[:, :GLA_KEY_DIM])
    k_ref[...] = _dot(h, w_ref[:, GLA_KEY_DIM:2 * GLA_KEY_DIM])
    v_lo = 2 * GLA_KEY_DIM
    v_ref[...] = _dot(h, w_ref[:, v_lo:v_lo + GLA_VAL_DIM]).astype(BF16)
    g_lo = v_lo + GLA_VAL_DIM
    g_ref[...] = _dot(h, w_ref[:, g_lo:g_lo + GLA_VAL_DIM])
    gate_lr = _dot(h, wg_ref[...].astype(BF16)).astype(BF16)
    gk = _dot(gate_lr, wgu_ref[...].astype(BF16)) + bg_ref[...]
    log_sig = jnp.minimum(gk, 0.0) - jnp.log(1.0 + jnp.exp(-jnp.abs(gk)))
    la_ref[...] = log_sig * (1.0 / GLA_GATE_NORMALIZER)


def _gla_in_project(x, nw, w_main, w_gate, w_gate_up, b_gate, rows):
    B, S, D = x.shape
    T = B * S
    row_spec = lambda width: pl.BlockSpec((rows, width), lambda i: (i, 0))
    const = lambda shape: pl.BlockSpec(shape, lambda i: (0, 0))
    n_main = 2 * GLA_KEY_DIM + 2 * GLA_VAL_DIM
    return pl.pallas_call(
        _gla_in_kernel,
        grid=(T // rows,),
        in_specs=[row_spec(D), const((1, D)), const((D, n_main)), const((D, GLA_GATE_RANK)),
                  const((GLA_GATE_RANK, GLA_KEY_DIM)), const((1, GLA_KEY_DIM))],
        out_specs=[row_spec(GLA_KEY_DIM), row_spec(GLA_KEY_DIM), row_spec(GLA_VAL_DIM),
                   row_spec(GLA_VAL_DIM), row_spec(GLA_KEY_DIM)],
        out_shape=[
            jax.ShapeDtypeStruct((T, GLA_KEY_DIM), F32),
            jax.ShapeDtypeStruct((T, GLA_KEY_DIM), F32),
            jax.ShapeDtypeStruct((T, GLA_VAL_DIM), BF16),
            jax.ShapeDtypeStruct((T, GLA_VAL_DIM), F32),
            jax.ShapeDtypeStruct((T, GLA_KEY_DIM), F32),
        ],
        compiler_params=_params("parallel"),
        name="gla_in",
    )(x.reshape(T, D), nw, w_main, w_gate, w_gate_up, b_gate)


def _gla_kernel(q_ref, k_ref, v_ref, g_ref, la_ref, nw_ref, o_ref, state_ref):
    @pl.when(pl.program_id(0) == 0)
    def _():
        state_ref[...] = jnp.zeros_like(state_ref)

    n_seq, n_rows, _ = q_ref.shape
    pair = 2 * GLA_CHUNK
    ri = lax.broadcasted_iota(jnp.int32, (pair, pair), 0)
    cj = lax.broadcasted_iota(jnp.int32, (pair, pair), 1)
    causal = ri >= cj
    cumsum_mat = (causal & ((ri >= GLA_CHUNK) == (cj >= GLA_CHUNK))).astype(BF16)
    first = lax.broadcasted_iota(jnp.int32, (pair, GLA_KEY_DIM), 0) < GLA_CHUNK
    ones = jnp.ones((pair, LANES), BF16)
    nw = nw_ref[...]

    def prepare(s, rows):
        la = la_ref[s, rows, :]
        la_hi = la.astype(BF16)
        la_lo = (la - la_hi.astype(F32)).astype(BF16)
        b = _dot(cumsum_mat, la_hi) + _dot(cumsum_mat, la_lo)
        bl0 = b[GLA_CHUNK - 1:GLA_CHUNK, :]
        bl1 = b[pair - 1:pair, :]
        ref_b = jnp.where(first, b - bl0, b)
        q_e = q_ref[s, rows, :] * jnp.exp(ref_b)
        k_e = k_ref[s, rows, :] * jnp.exp(-ref_b)
        scale = GLA_DK ** -0.5
        q_sc = (q_e * scale).astype(BF16)
        k_sc = k_e.astype(BF16)
        q_st = (q_e * (jnp.exp(bl0) * scale)).astype(BF16)
        k_up = (k_e * jnp.exp(bl1)).astype(BF16)
        return la_hi, la_lo, q_sc, k_sc, q_st, k_up

    for p in range(n_rows // pair):
        rows = slice(p * pair, (p + 1) * pair)
        prepared = [prepare(s, rows) for s in range(n_seq)]
        for h in range(GLA_HEADS):
            kc = slice(h * GLA_DK, (h + 1) * GLA_DK)
            vc = slice(h * GLA_DV, (h + 1) * GLA_DV)
            for s in range(n_seq):
                la_hi, la_lo, q_sc, k_sc, q_st, k_up = prepared[s]
                v = v_ref[s, rows, vc]
                a = jnp.where(causal, _dot_nt(q_sc[:, kc], k_sc[:, kc]), 0.0).astype(BF16)
                state = state_ref[s, h]
                o = _dot(jnp.concatenate([a, q_st[:, kc]], axis=1),
                         jnp.concatenate([v, state.astype(BF16)], axis=0))
                bl_t = _dot_tn(la_hi[:, kc], ones) + _dot_tn(la_lo[:, kc], ones)
                decay = jnp.exp(bl_t)
                decay = jnp.concatenate([decay] * (GLA_DV // LANES), axis=1)
                state_ref[s, h] = decay * state + _dot_tn(k_up[:, kc], v)
                var = jnp.mean(o * o, axis=-1, keepdims=True)
                o = o * lax.rsqrt(var + NORM_EPS) * nw
                half_g = 0.5 * g_ref[s, rows, vc]
                o_ref[s, rows, vc] = (o * (half_g * (1.0 + jnp.tanh(half_g)))).astype(BF16)


def _gla_recurrence(q, k, v, g, la, nw, B, S, rows):
    seq_spec = lambda width: pl.BlockSpec((B, rows, width), lambda i: (0, i, 0))
    by_seq = lambda t: t.reshape(B, S, t.shape[-1])
    out = pl.pallas_call(
        _gla_kernel,
        grid=(S // rows,),
        in_specs=[seq_spec(GLA_KEY_DIM), seq_spec(GLA_KEY_DIM), seq_spec(GLA_VAL_DIM),
                  seq_spec(GLA_VAL_DIM), seq_spec(GLA_KEY_DIM),
                  pl.BlockSpec((1, GLA_DV), lambda i: (0, 0))],
        out_specs=seq_spec(GLA_VAL_DIM),
        out_shape=jax.ShapeDtypeStruct((B, S, GLA_VAL_DIM), BF16),
        scratch_shapes=[pltpu.VMEM((B, GLA_HEADS, GLA_DK, GLA_DV), F32)],
        compiler_params=_params("arbitrary"),
        name="gla_recurrence",
    )(by_seq(q), by_seq(k), by_seq(v), by_seq(g), by_seq(la), nw)
    return out.reshape(B * S, GLA_VAL_DIM)


def _rope_tables(seq_len):
    pos = np.arange(seq_len, dtype=np.float64)
    inv_freq = ROPE_THETA ** (-np.arange(0, HEAD_DIM, 2, dtype=np.float64) / HEAD_DIM)
    ang = pos[:, None] * inv_freq[None, :]
    cos, sin = np.cos(ang), np.sin(ang)
    cos2 = np.concatenate([cos, cos], axis=-1).astype(np.float32)
    sin2 = np.concatenate([-sin, sin], axis=-1).astype(np.float32)
    return jnp.asarray(cos2), jnp.asarray(sin2)


def kernel(x, norm_mix_w, norm_mlp_w, final_norm_w, attn_w_in, attn_w_out, gla_w_in,
           gla_w_gate_up, gla_b_gate, gla_norm_w, gla_w_out, mlp_w_up, mlp_w_down):
    B, S, D = x.shape
    row = lambda w: w.reshape(1, -1).astype(F32)

    cos2, sin2 = _rope_tables(S)
    qkv, (w_out0, w_up0, w_down0) = _qkv_project(
        x, row(norm_mix_w[0]), attn_w_in[0].astype(BF16), cos2, sin2,
        riders=[(attn_w_out, 0, D), (mlp_w_up, 0, D_FF), (mlp_w_down, 0, D)], rows=RESIDENT_ROWS)
    outs, lses = [], []
    for g, (window, dil) in enumerate(DILATED_GROUPS):
        assert window // dil == ATTN_BLOCK and (S // dil) % ATTN_BLOCK == 0
        q, k, v = qkv[3 * g:3 * g + 3]
        o, lse = _group_attention(q, k, v, step_rows=STREAM_ROWS)
        outs.append(o)
        lses.append(lse)
    n_main = 2 * GLA_KEY_DIM + 2 * GLA_VAL_DIM
    x, (w_gla, w_out1, w_up1, w_down1) = _attn_tail(
        outs, lses, x, w_out0, row(norm_mlp_w[0]), w_up0, w_down0,
        riders=[(gla_w_in, 0, n_main), (gla_w_out, 0, D), (mlp_w_up, 1, D_FF), (mlp_w_down, 1, D)],
        rows=RESIDENT_ROWS)

    q, k, v, g, la = _gla_in_project(x, row(norm_mix_w[1]), w_gla, gla_w_in[0][:, n_main:],
                                     gla_w_gate_up[0], row(gla_b_gate[0]), rows=STREAM_ROWS)
    o = _gla_recurrence(q, k, v, g, la, row(gla_norm_w[0]), B, S, rows=RESIDENT_ROWS)
    return _gla_tail(o, x, w_out1, row(norm_mlp_w[1]), w_up1, w_down1, row(final_norm_w),
                     rows=RESIDENT_ROWS)
```

```python
from functools import partial

import jax
import jax.numpy as jnp
import numpy as np
from jax import lax
from jax.experimental import pallas as pl
from jax.experimental.pallas import tpu as pltpu

D_MODEL = 1024
NORM_EPS = 1e-5
DILATED_GROUPS = ((128, 1), (512, 4), (2048, 16))
ATTN_HEADS = 8
HEAD_DIM = 128
ATTN_WIDTH = ATTN_HEADS * HEAD_DIM
ATTN_BLOCK = 128
ROPE_THETA = 10000.0
GLA_HEADS = 4
GLA_DK = 128
GLA_DV = 256
GLA_KEY_DIM = GLA_HEADS * GLA_DK
GLA_VAL_DIM = GLA_HEADS * GLA_DV
GLA_GATE_RANK = 16
GLA_GATE_NORMALIZER = 16.0
GLA_CHUNK = 64
D_FF = 4 * D_MODEL

LANES = 128
VMEM_LIMIT_BYTES = 56 * 1024 * 1024
BF16_SUBLANES = 16
RESIDENT_ROWS = 512
STREAM_ROWS = 1024
NEG_BIG = -1e30

BF16 = jnp.bfloat16
F32 = jnp.float32


def _params(*semantics):
    return pltpu.CompilerParams(dimension_semantics=semantics,
                                vmem_limit_bytes=VMEM_LIMIT_BYTES)


def _rmsnorm_rows(x, w):
    var = jnp.mean(x * x, axis=-1, keepdims=True)
    return x * lax.rsqrt(var + NORM_EPS) * w


def _dot(a, b):
    return jnp.dot(a, b, preferred_element_type=F32)


def _dot_nt(a, b):
    return lax.dot_general(a, b, (((1,), (1,)), ((), ())), preferred_element_type=F32)


def _dot_tn(a, b):
    return lax.dot_general(a, b, (((0,), (0,)), ((), ())), preferred_element_type=F32)


def _cast_blocks(src_refs, dst_refs):
    for src_ref, dst_ref in zip(src_refs, dst_refs):
        dst_ref[...] = src_ref[:, :dst_ref.shape[1]].astype(BF16)


def _rider_specs(riders, n_steps):
    in_specs, out_specs, out_shapes = [], [], []
    for w, layer, n_keep in riders:
        _, k, n = w.shape
        assert k % n_steps == 0 and (k // n_steps) % BF16_SUBLANES == 0
        blk = k // n_steps
        index = lambda s, layer=layer: (layer, jnp.minimum(s, n_steps - 1), 0)
        in_specs.append(pl.BlockSpec((None, blk, n), index))
        out_specs.append(pl.BlockSpec((blk, n_keep), lambda s: (jnp.minimum(s, n_steps - 1), 0)))
        out_shapes.append(jax.ShapeDtypeStruct((k, n_keep), BF16))
    return in_specs, out_specs, out_shapes


QKV_CHUNK = 256
Q_SCALE = HEAD_DIM ** -0.5 * float(np.log2(np.e))


def _qkv_kernel(*refs, n_riders):
    n_slabs = D_MODEL // LANES
    n_out = 3 * len(DILATED_GROUPS)
    x_refs = refs[:n_slabs]
    nw_ref, w_ref, cos_ref, sin_ref = refs[n_slabs:n_slabs + 4]
    rider_in = refs[n_slabs + 4:n_slabs + 4 + n_riders]
    out_refs = refs[n_slabs + 4 + n_riders:n_slabs + 4 + n_riders + n_out]
    rider_out = refs[n_slabs + 4 + n_riders + n_out:-2]
    tab_ref, perm_ref = refs[-2:]
    rows = x_refs[0].shape[0]
    n_chunks = ATTN_WIDTH // QKV_CHUNK
    row_blocks = rows // ATTN_BLOCK

    def residue_major(src_ref, src_dil, dil):
        if dil == src_dil:
            return src_ref[...]
        pieces = [src_ref[pl.ds((r % src_dil) * (rows // src_dil) + r // src_dil, rows // dil,
                                stride=dil // src_dil), :] for r in range(dil)]
        return jnp.concatenate(pieces, axis=0)

    prev_dil = 1
    for g, (_, dil) in enumerate(DILATED_GROUPS):
        n = rows // dil
        x_src = x_refs if prev_dil == 1 else [perm_ref.at[c] for c in range(n_slabs)]
        slabs = [residue_major(src, prev_dil, dil) for src in x_src]
        if 1 < dil < DILATED_GROUPS[-1][1]:
            for c in range(n_slabs):
                perm_ref[c] = slabs[c]
        h = _rmsnorm_rows(jnp.concatenate(slabs, axis=1), nw_ref[...]).astype(BF16)
        if dil == 1:
            cos_src, sin_src = cos_ref, sin_ref
        else:
            tab_src = (cos_ref, sin_ref) if prev_dil == 1 else (tab_ref.at[g - 2, 0],
                                                                tab_ref.at[g - 2, 1])
            for t in range(2):
                tab_ref[g - 1, t] = residue_major(tab_src[t], prev_dil, dil)
            cos_src, sin_src = tab_ref.at[g - 1, 0], tab_ref.at[g - 1, 1]
        if dil > 1:
            prev_dil = dil
        for part in range(3):
            o_ref = out_refs[3 * g + part]
            for j in range(n_chunks):
                col = (3 * g + part) * ATTN_WIDTH + j * QKV_CHUNK
                acc = _dot(h, w_ref[:, col:col + QKV_CHUNK])
                for half in range(QKV_CHUNK // HEAD_DIM):
                    lo = j * QKV_CHUNK + half * HEAD_DIM
                    for rb in range(row_blocks):
                        rs = slice(rb * ATTN_BLOCK, (rb + 1) * ATTN_BLOCK)
                        t = acc[rs, half * HEAD_DIM:(half + 1) * HEAD_DIM]
                        if part < 2:
                            t = t * cos_src[rs, :] + pltpu.roll(t, HEAD_DIM // 2, 1) * sin_src[rs, :]
                        if part == 0:
                            t = t * Q_SCALE
                        t = t.astype(BF16)
                        if n >= ATTN_BLOCK:
                            first = rb * ATTN_BLOCK
                            o_ref[first // n, pl.ds(first % n, ATTN_BLOCK), lo:lo + HEAD_DIM] = t
                        else:
                            for k in range(ATTN_BLOCK // n):
                                o_ref[rb * (ATTN_BLOCK // n) + k, :, lo:lo + HEAD_DIM] = (
                                    t[k * n:(k + 1) * n])
    _cast_blocks(rider_in, rider_out)


def _qkv_project(x, nw, w_in, cos2, sin2, riders, rows):
    B, S, D = x.shape
    n_blocks = S // rows
    out_shapes, out_specs = [], []
    for _, dil in DILATED_GROUPS:
        for _ in range(3):
            out_shapes.append(jax.ShapeDtypeStruct((B, dil, S // dil, ATTN_WIDTH), BF16))
            out_specs.append(pl.BlockSpec((None, dil, rows // dil, ATTN_WIDTH),
                                          lambda i: (i // n_blocks, 0, i % n_blocks, 0)))
    rider_in, rider_out, rider_shapes = _rider_specs(riders, B * n_blocks)
    tab_spec = pl.BlockSpec((rows, HEAD_DIM), lambda i: (i % n_blocks, 0))
    const = lambda shape: pl.BlockSpec(shape, lambda i: (0, 0), pipeline_mode=pl.Buffered(1))
    xf = x.reshape(B * S, D)
    slab_specs = [pl.BlockSpec((rows, LANES), lambda i, c=c: (i, c)) for c in range(D // LANES)]
    outs = pl.pallas_call(
        partial(_qkv_kernel, n_riders=len(riders)),
        grid=(B * n_blocks,),
        in_specs=slab_specs + [const((1, D)), const((D, w_in.shape[1])), tab_spec, tab_spec]
        + rider_in,
        out_specs=out_specs + rider_out,
        out_shape=out_shapes + rider_shapes,
        scratch_shapes=[pltpu.VMEM((len(DILATED_GROUPS) - 1, 2, rows, HEAD_DIM), F32),
                        pltpu.VMEM((D // LANES, rows, LANES), F32)],
        compiler_params=_params("parallel"),
        name="attn_qkv",
    )(*([xf] * (D // LANES)), nw, w_in, cos2, sin2, *[w for w, _, _ in riders])
    return outs[:len(out_shapes)], outs[len(out_shapes):]


def _band_bias():
    qi = np.arange(ATTN_BLOCK)[:, None]
    kj = np.arange(2 * ATTN_BLOCK)[None, :]
    rel = qi - kj + ATTN_BLOCK
    band = (rel >= 0) & (rel <= ATTN_BLOCK)
    masks = np.stack([band & (kj >= ATTN_BLOCK), band])
    return jnp.asarray(np.where(masks, 0.0, NEG_BIG).astype(np.float32))


def _attn_kernel(q_ref, k_ref, v_ref, kh_ref, vh_ref, bias_ref, o_ref, lse_ref):
    first_bias = jnp.where(pl.program_id(2) > 0, 1, 0)
    lane = lax.broadcasted_iota(jnp.int32, (ATTN_BLOCK, LANES), 1)
    n_res, n_rows, _ = q_ref.shape
    for r in range(n_res):
        for c in range(n_rows // ATTN_BLOCK):
            rows = slice(c * ATTN_BLOCK, (c + 1) * ATTN_BLOCK)
            bias = bias_ref.at[first_bias if c == 0 else 1]
            lse_tile = jnp.zeros((ATTN_BLOCK, LANES), F32)
            for h in range(ATTN_HEADS):
                cols = slice(h * HEAD_DIM, (h + 1) * HEAD_DIM)
                q = q_ref[r, rows, cols]
                if c == 0:
                    kk = jnp.concatenate([kh_ref[r, :, cols], k_ref[r, rows, cols]], axis=0)
                    vv = jnp.concatenate([vh_ref[r, :, cols], v_ref[r, rows, cols]], axis=0)
                else:
                    hist = slice((c - 1) * ATTN_BLOCK, (c + 1) * ATTN_BLOCK)
                    kk = k_ref[r, hist, cols]
                    vv = v_ref[r, hist, cols]
                s = _dot_nt(q, kk) + bias[...]
                m = jnp.max(s, axis=-1, keepdims=True)
                p = jnp.exp2(s - m)
                l = jnp.sum(p, axis=-1, keepdims=True)
                o = _dot(p.astype(BF16), vv) / l
                o_ref[r, rows, cols] = o.astype(BF16)
                lse_tile = jnp.where(lane == h, m + jnp.log2(l), lse_tile)
            lse_ref[r, rows, :] = lse_tile


def _group_attention(q, k, v, step_rows):
    B, dil, L, W = q.shape
    rows = min(step_rows, L)
    n_res = min(step_rows // rows, dil)
    n_sub = rows // ATTN_BLOCK
    blk = lambda width: pl.BlockSpec((None, n_res, rows, width), lambda b, r, i: (b, r, i, 0))
    halo = pl.BlockSpec((None, n_res, ATTN_BLOCK, W),
                        lambda b, r, i: (b, r, jnp.maximum(i * n_sub - 1, 0), 0))
    return pl.pallas_call(
        _attn_kernel,
        grid=(B, dil // n_res, L // rows),
        in_specs=[blk(W), blk(W), blk(W), halo, halo,
                  pl.BlockSpec((2, ATTN_BLOCK, 2 * ATTN_BLOCK), lambda b, r, i: (0, 0, 0))],
        out_specs=[blk(W), blk(LANES)],
        out_shape=[
            jax.ShapeDtypeStruct((B, dil, L, W), BF16),
            jax.ShapeDtypeStruct((B, dil, L, LANES), F32),
        ],
        compiler_params=_params("parallel", "parallel", "parallel"),
        name="attn_band",
    )(q, k, v, k, v, _band_bias())


MLP_FF_CHUNK = 1024


def _project_and_mlp(mixed, x_ref, wo_ref, nw_ref, wu_ref, wd_ref, side_work=None):
    x_mid = x_ref[...] + _dot(mixed, wo_ref[...])
    h = _rmsnorm_rows(x_mid, nw_ref[...]).astype(BF16)
    acc = x_mid
    for f in range(D_FF // MLP_FF_CHUNK):
        cols = slice(f * MLP_FF_CHUNK, (f + 1) * MLP_FF_CHUNK)
        a = jnp.maximum(_dot(h, wu_ref[:, cols]), 0.0)
        acc = acc + _dot((a * a).astype(BF16), wd_ref[cols, :])
        if side_work is not None:
            side_work(f)
    return acc


def _attn_tail_kernel(*refs, n_riders):
    n_groups = len(DILATED_GROUPS)
    o_refs = refs[:n_groups]
    lse_refs = refs[n_groups:2 * n_groups]
    rest = refs[2 * n_groups:]
    x_ref, wo_ref, nw_ref, wu_ref, wd_ref = rest[:5]
    rider_in = rest[5:5 + n_riders]
    out_ref = rest[5 + n_riders]
    rider_out = rest[6 + n_riders:6 + 2 * n_riders]
    merged_ref, nat_ref = rest[6 + 2 * n_riders:]
    rows = x_ref.shape[0]
    step = pl.program_id(0)
    write_slot = step % 2
    read_slot = 1 - write_slot

    @pl.when(step == 0)
    def _():
        merged_ref[1] = jnp.zeros(merged_ref.shape[1:], BF16)

    def natural(src_ref, col, dil, slot):
        lanes = slice(col * LANES, (col + 1) * LANES)
        if dil == 1:
            return src_ref[0, :, lanes].astype(F32)
        for r in range(dil):
            nat_ref[slot, pl.ds(r, rows // dil, stride=dil), :] = src_ref[r, :, lanes].astype(F32)
        return nat_ref[slot]

    mixed = merged_ref[read_slot]
    heads_per_chunk = ATTN_HEADS // (D_FF // MLP_FF_CHUNK)
    alphas = []

    def merge_heads(f):
        if not alphas:
            lses = [natural(lse_refs[g], 0, dil, g) for g, (_, dil) in enumerate(DILATED_GROUPS)]
            m = jnp.maximum(jnp.maximum(lses[0], lses[1]), lses[2])
            es = [jnp.exp2(l - m) for l in lses]
            inv = 1.0 / (es[0] + es[1] + es[2])
            alphas.extend(e * inv for e in es)
        for h in range(f * heads_per_chunk, (f + 1) * heads_per_chunk):
            o = None
            for g, (_, dil) in enumerate(DILATED_GROUPS):
                slot = n_groups + 2 * g + h % 2
                term = alphas[g][:, h:h + 1] * natural(o_refs[g], h, dil, slot)
                o = term if o is None else o + term
            merged_ref[write_slot, :, h * HEAD_DIM:(h + 1) * HEAD_DIM] = o.astype(BF16)

    out_ref[...] = _project_and_mlp(mixed, x_ref, wo_ref, nw_ref, wu_ref, wd_ref,
                                    side_work=merge_heads)
    _cast_blocks(rider_in, rider_out)


def _attn_tail(o_list, lse_list, x, w_out, nw, w_up, w_down, riders, rows):
    B, S, D = x.shape
    W = ATTN_WIDTH
    n_blocks = S // rows
    n_groups = len(DILATED_GROUPS)
    n_tiles = B * n_blocks

    def residue_spec(dil, width):
        def index(s):
            t = jnp.minimum(s, n_tiles - 1)
            return (t // n_blocks, 0, t % n_blocks, 0)
        return pl.BlockSpec((None, dil, rows // dil, width), index)

    o_specs = [residue_spec(dil, W) for _, dil in DILATED_GROUPS]
    lse_specs = [residue_spec(dil, LANES) for _, dil in DILATED_GROUPS]
    rider_in, rider_out, rider_shapes = _rider_specs(riders, n_tiles)
    row_spec = pl.BlockSpec((rows, D), lambda s: (jnp.maximum(s - 1, 0), 0))
    const = lambda shape: pl.BlockSpec(shape, lambda s: (0, 0), pipeline_mode=pl.Buffered(1))
    outs = pl.pallas_call(
        partial(_attn_tail_kernel, n_riders=len(riders)),
        grid=(n_tiles + 1,),
        in_specs=o_specs + lse_specs + [row_spec, const((W, D)), const((1, D)),
                                        const((D, D_FF)), const((D_FF, D))] + rider_in,
        out_specs=[row_spec] + rider_out,
        out_shape=[jax.ShapeDtypeStruct((B * S, D), F32)] + rider_shapes,
        scratch_shapes=[pltpu.VMEM((2, rows, W), BF16),
                        pltpu.VMEM((3 * n_groups, rows, LANES), F32)],
        compiler_params=_params("arbitrary"),
        name="attn_tail",
    )(*o_list, *lse_list, x.reshape(B * S, D), w_out, nw, w_up, w_down,
      *[w for w, _, _ in riders])
    return outs[0].reshape(B, S, D), outs[1:]


def _gla_tail_kernel(o_ref, x_ref, wo_ref, nw_ref, wu_ref, wd_ref, nwf_ref, out_ref):
    acc = _project_and_mlp(o_ref[...], x_ref, wo_ref, nw_ref, wu_ref, wd_ref)
    out_ref[...] = _rmsnorm_rows(acc, nwf_ref[...])


def _gla_tail(o, x, w_out, nw, w_up, w_down, nw_final, rows):
    B, S, D = x.shape
    T = B * S
    row_spec = lambda width: pl.BlockSpec((rows, width), lambda i: (i, 0))
    const = lambda shape: pl.BlockSpec(shape, lambda i: (0, 0), pipeline_mode=pl.Buffered(1))
    out = pl.pallas_call(
        _gla_tail_kernel,
        grid=(T // rows,),
        in_specs=[row_spec(o.shape[-1]), row_spec(D), const((o.shape[-1], D)), const((1, D)),
                  const((D, D_FF)), const((D_FF, D)), const((1, D))],
        out_specs=row_spec(D),
        out_shape=jax.ShapeDtypeStruct((T, D), F32),
        compiler_params=_params("parallel"),
        name="gla_tail",
    )(o, x.reshape(T, D), w_out, nw, w_up, w_down, nw_final)
    return out.reshape(B, S, D)


def _gla_pair_constants():
    pair = 2 * GLA_CHUNK
    ri = lax.broadcasted_iota(jnp.int32, (pair, pair), 0)
    cj = lax.broadcasted_iota(jnp.int32, (pair, pair), 1)
    causal = ri >= cj
    cumsum_mat = (causal & ((ri >= GLA_CHUNK) == (cj >= GLA_CHUNK))).astype(BF16)
    first = lax.broadcasted_iota(jnp.int32, (pair, GLA_KEY_DIM), 0) < GLA_CHUNK
    return pair, causal, cumsum_mat, first


def _gla_in_kernel(x_ref, nw_ref, w_ref, wg_ref, wgu_ref, bg_ref,
                   qsc_ref, ksc_ref, qst_ref, kup_ref, v_ref, g_ref, la_ref, q_ref, k_ref):
    h = _rmsnorm_rows(x_ref[...], nw_ref[...]).astype(BF16)
    gate_lr = _dot(h, wg_ref[...].astype(BF16)).astype(BF16)
    gk = _dot(gate_lr, wgu_ref[...].astype(BF16)) + bg_ref[...]
    log_sig = jnp.minimum(gk, 0.0) - jnp.log(1.0 + jnp.exp(-jnp.abs(gk)))
    la_ref[...] = log_sig * (1.0 / GLA_GATE_NORMALIZER)
    q_ref[...] = _dot(h, w_ref[:, :GLA_KEY_DIM])
    k_ref[...] = _dot(h, w_ref[:, GLA_KEY_DIM:2 * GLA_KEY_DIM])
    v_lo = 2 * GLA_KEY_DIM
    g_lo = v_lo + GLA_VAL_DIM
    pair, _, cumsum_mat, first = _gla_pair_constants()
    n_pairs = x_ref.shape[0] // pair
    wide = [(v_ref, v_lo), (g_ref, g_lo)]
    for p in range(n_pairs):
        if p % (n_pairs // 2) == 0:
            dst_ref, col = wide[p // (n_pairs // 2)]
            dst_ref[...] = _dot(h, w_ref[:, col:col + GLA_VAL_DIM]).astype(dst_ref.dtype)
        rows = slice(p * pair, (p + 1) * pair)
        la = la_ref[rows, :]
        la_hi = la.astype(BF16)
        la_lo = (la - la_hi.astype(F32)).astype(BF16)
        b = _dot(cumsum_mat, la_hi) + _dot(cumsum_mat, la_lo)
        bl0 = b[GLA_CHUNK - 1:GLA_CHUNK, :]
        bl1 = b[pair - 1:pair, :]
        ref_b = jnp.where(first, b - bl0, b)
        q_e = q_ref[rows, :] * jnp.exp(ref_b)
        k_e = k_ref[rows, :] * jnp.exp(-ref_b)
        scale = GLA_DK ** -0.5
        qsc_ref[rows, :] = (q_e * scale).astype(BF16)
        ksc_ref[rows, :] = k_e.astype(BF16)
        qst_ref[rows, :] = (q_e * (jnp.exp(bl0) * scale)).astype(BF16)
        kup_ref[rows, :] = (k_e * jnp.exp(bl1)).astype(BF16)


def _gla_in_project(x, nw, w_main, w_gate, w_gate_up, b_gate, rows):
    B, S, D = x.shape
    T = B * S
    row_spec = lambda width: pl.BlockSpec((rows, width), lambda i: (i, 0))
    const = lambda shape: pl.BlockSpec(shape, lambda i: (0, 0))
    n_main = 2 * GLA_KEY_DIM + 2 * GLA_VAL_DIM
    key = jax.ShapeDtypeStruct((T, GLA_KEY_DIM), BF16)
    return pl.pallas_call(
        _gla_in_kernel,
        grid=(T // rows,),
        in_specs=[row_spec(D), const((1, D)), const((D, n_main)), const((D, GLA_GATE_RANK)),
                  const((GLA_GATE_RANK, GLA_KEY_DIM)), const((1, GLA_KEY_DIM))],
        out_specs=[row_spec(GLA_KEY_DIM)] * 4 + [row_spec(GLA_VAL_DIM), row_spec(GLA_VAL_DIM),
                                                 row_spec(GLA_KEY_DIM)],
        out_shape=[key, key, key, key,
                   jax.ShapeDtypeStruct((T, GLA_VAL_DIM), BF16),
                   jax.ShapeDtypeStruct((T, GLA_VAL_DIM), F32),
                   jax.ShapeDtypeStruct((T, GLA_KEY_DIM), F32)],
        scratch_shapes=[pltpu.VMEM((rows, GLA_KEY_DIM), F32), pltpu.VMEM((rows, GLA_KEY_DIM), F32)],
        compiler_params=_params("parallel"),
        name="gla_in",
    )(x.reshape(T, D), nw, w_main, w_gate, w_gate_up, b_gate)


def _gla_kernel(qsc_ref, ksc_ref, qst_ref, kup_ref, v_ref, g_ref, la_ref, nw_ref, o_ref, state_ref):
    @pl.when(pl.program_id(0) == 0)
    def _():
        state_ref[...] = jnp.zeros_like(state_ref)

    n_seq, n_rows, _ = qsc_ref.shape
    pair, causal, _, _ = _gla_pair_constants()
    ones = jnp.ones((pair, LANES), BF16)
    nw = nw_ref[...]
    for p in range(n_rows // pair):
        rows = slice(p * pair, (p + 1) * pair)
        for h in range(GLA_HEADS):
            kc = slice(h * GLA_DK, (h + 1) * GLA_DK)
            vc = slice(h * GLA_DV, (h + 1) * GLA_DV)
            for s in range(n_seq):
                v = v_ref[s, rows, vc]
                a = jnp.where(causal, _dot_nt(qsc_ref[s, rows, kc], ksc_ref[s, rows, kc]), 0.0)
                state = state_ref[s, h]
                o = _dot(jnp.concatenate([a.astype(BF16), qst_ref[s, rows, kc]], axis=1),
                         jnp.concatenate([v, state.astype(BF16)], axis=0))
                la = la_ref[s, rows, kc]
                la_hi = la.astype(BF16)
                la_lo = (la - la_hi.astype(F32)).astype(BF16)
                decay = jnp.exp(_dot_tn(la_hi, ones) + _dot_tn(la_lo, ones))
                decay = jnp.concatenate([decay] * (GLA_DV // LANES), axis=1)
                state_ref[s, h] = decay * state + _dot_tn(kup_ref[s, rows, kc], v)
                var = jnp.mean(o * o, axis=-1, keepdims=True)
                o = o * lax.rsqrt(var + NORM_EPS) * nw
                half_g = 0.5 * g_ref[s, rows, vc]
                o_ref[s, rows, vc] = (o * (half_g * (1.0 + jnp.tanh(half_g)))).astype(BF16)


def _gla_recurrence(q_sc, k_sc, q_st, k_up, v, g, la, nw, B, S, rows):
    seq_spec = lambda width: pl.BlockSpec((B, rows, width), lambda i: (0, i, 0))
    by_seq = lambda t: t.reshape(B, S, t.shape[-1])
    out = pl.pallas_call(
        _gla_kernel,
        grid=(S // rows,),
        in_specs=[seq_spec(GLA_KEY_DIM)] * 4 + [seq_spec(GLA_VAL_DIM), seq_spec(GLA_VAL_DIM),
                                                seq_spec(GLA_KEY_DIM),
                                                pl.BlockSpec((1, GLA_DV), lambda i: (0, 0))],
        out_specs=seq_spec(GLA_VAL_DIM),
        out_shape=jax.ShapeDtypeStruct((B, S, GLA_VAL_DIM), BF16),
        scratch_shapes=[pltpu.VMEM((B, GLA_HEADS, GLA_DK, GLA_DV), F32)],
        compiler_params=_params("arbitrary"),
        name="gla_recurrence",
    )(by_seq(q_sc), by_seq(k_sc), by_seq(q_st), by_seq(k_up), by_seq(v), by_seq(g), by_seq(la), nw)
    return out.reshape(B * S, GLA_VAL_DIM)


def _rope_tables(seq_len):
    pos = np.arange(seq_len, dtype=np.float64)
    inv_freq = ROPE_THETA ** (-np.arange(0, HEAD_DIM, 2, dtype=np.float64) / HEAD_DIM)
    ang = pos[:, None] * inv_freq[None, :]
    cos, sin = np.cos(ang), np.sin(ang)
    cos2 = np.concatenate([cos, cos], axis=-1).astype(np.float32)
    sin2 = np.concatenate([-sin, sin], axis=-1).astype(np.float32)
    return jnp.asarray(cos2), jnp.asarray(sin2)


def kernel(x, norm_mix_w, norm_mlp_w, final_norm_w, attn_w_in, attn_w_out, gla_w_in,
           gla_w_gate_up, gla_b_gate, gla_norm_w, gla_w_out, mlp_w_up, mlp_w_down):
    B, S, D = x.shape
    row = lambda w: w.reshape(1, -1).astype(F32)

    cos2, sin2 = _rope_tables(S)
    qkv, (w_out0, w_up0, w_down0) = _qkv_project(
        x, row(norm_mix_w[0]), attn_w_in[0].astype(BF16), cos2, sin2,
        riders=[(attn_w_out, 0, D), (mlp_w_up, 0, D_FF), (mlp_w_down, 0, D)], rows=RESIDENT_ROWS)
    outs, lses = [], []
    for g, (window, dil) in enumerate(DILATED_GROUPS):
        assert window // dil == ATTN_BLOCK and (S // dil) % ATTN_BLOCK == 0
        q, k, v = qkv[3 * g:3 * g + 3]
        o, lse = _group_attention(q, k, v, step_rows=STREAM_ROWS)
        outs.append(o)
        lses.append(lse)
    n_main = 2 * GLA_KEY_DIM + 2 * GLA_VAL_DIM
    x, (w_gla, w_out1, w_up1, w_down1) = _attn_tail(
        outs, lses, x, w_out0, row(norm_mlp_w[0]), w_up0, w_down0,
        riders=[(gla_w_in, 0, n_main), (gla_w_out, 0, D), (mlp_w_up, 1, D_FF), (mlp_w_down, 1, D)],
        rows=RESIDENT_ROWS)

    prepared = _gla_in_project(x, row(norm_mix_w[1]), w_gla, gla_w_in[0][:, n_main:],
                               gla_w_gate_up[0], row(gla_b_gate[0]), rows=STREAM_ROWS)
    o = _gla_recurrence(*prepared, row(gla_norm_w[0]), B, S, rows=RESIDENT_ROWS)
    return _gla_tail(o, x, w_out1, row(norm_mlp_w[1]), w_up1, w_down1, row(final_norm_w),
                     rows=RESIDENT_ROWS)
```

```python
from functools import partial

import jax
import jax.numpy as jnp
import numpy as np
from jax import lax
from jax.experimental import pallas as pl
from jax.experimental.pallas import tpu as pltpu

D_MODEL = 1024
NORM_EPS = 1e-5
DILATED_GROUPS = ((128, 1), (512, 4), (2048, 16))
ATTN_HEADS = 8
HEAD_DIM = 128
ATTN_WIDTH = ATTN_HEADS * HEAD_DIM
ATTN_BLOCK = 128
ROPE_THETA = 10000.0
GLA_HEADS = 4
GLA_DK = 128
GLA_DV = 256
GLA_KEY_DIM = GLA_HEADS * GLA_DK
GLA_VAL_DIM = GLA_HEADS * GLA_DV
GLA_GATE_RANK = 16
GLA_GATE_NORMALIZER = 16.0
GLA_CHUNK = 64
D_FF = 4 * D_MODEL

LANES = 128
VMEM_LIMIT_BYTES = 56 * 1024 * 1024
BF16_SUBLANES = 16
RESIDENT_ROWS = 512
STREAM_ROWS = 1024
NEG_BIG = -1e30

BF16 = jnp.bfloat16
F32 = jnp.float32


def _params(*semantics):
    return pltpu.CompilerParams(dimension_semantics=semantics,
                                vmem_limit_bytes=VMEM_LIMIT_BYTES)


def _rmsnorm_rows(x, w):
    var = jnp.mean(x * x, axis=-1, keepdims=True)
    return x * lax.rsqrt(var + NORM_EPS) * w


def _dot(a, b):
    return jnp.dot(a, b, preferred_element_type=F32)


def _dot_nt(a, b):
    return lax.dot_general(a, b, (((1,), (1,)), ((), ())), preferred_element_type=F32)


def _dot_tn(a, b):
    return lax.dot_general(a, b, (((0,), (0,)), ((), ())), preferred_element_type=F32)


def _cast_blocks(src_refs, dst_refs):
    for src_ref, dst_ref in zip(src_refs, dst_refs):
        dst_ref[...] = src_ref[:, :dst_ref.shape[1]].astype(BF16)


def _rider_specs(riders, n_steps):
    in_specs, out_specs, out_shapes = [], [], []
    for w, layer, n_keep in riders:
        _, k, n = w.shape
        assert k % n_steps == 0 and (k // n_steps) % BF16_SUBLANES == 0
        blk = k // n_steps
        index = lambda s, layer=layer: (layer, jnp.minimum(s, n_steps - 1), 0)
        in_specs.append(pl.BlockSpec((None, blk, n), index))
        out_specs.append(pl.BlockSpec((blk, n_keep), lambda s: (jnp.minimum(s, n_steps - 1), 0)))
        out_shapes.append(jax.ShapeDtypeStruct((k, n_keep), BF16))
    return in_specs, out_specs, out_shapes


QKV_CHUNK = 256
Q_SCALE = HEAD_DIM ** -0.5 * float(np.log2(np.e))


def _qkv_kernel(*refs, n_riders):
    n_slabs = D_MODEL // LANES
    n_out = 3 * len(DILATED_GROUPS)
    x_refs = refs[:n_slabs]
    nw_ref, w_ref, cos_ref, sin_ref = refs[n_slabs:n_slabs + 4]
    rider_in = refs[n_slabs + 4:n_slabs + 4 + n_riders]
    out_refs = refs[n_slabs + 4 + n_riders:n_slabs + 4 + n_riders + n_out]
    rider_out = refs[n_slabs + 4 + n_riders + n_out:-2]
    tab_ref, perm_ref = refs[-2:]
    rows = x_refs[0].shape[0]
    n_chunks = ATTN_WIDTH // QKV_CHUNK
    row_blocks = rows // ATTN_BLOCK

    def residue_major(src_ref, src_dil, dil):
        if dil == src_dil:
            return src_ref[...]
        pieces = [src_ref[pl.ds((r % src_dil) * (rows // src_dil) + r // src_dil, rows // dil,
                                stride=dil // src_dil), :] for r in range(dil)]
        return jnp.concatenate(pieces, axis=0)

    prev_dil = 1
    for g, (_, dil) in enumerate(DILATED_GROUPS):
        n = rows // dil
        x_src = x_refs if prev_dil == 1 else [perm_ref.at[c] for c in range(n_slabs)]
        slabs = [residue_major(src, prev_dil, dil) for src in x_src]
        if 1 < dil < DILATED_GROUPS[-1][1]:
            for c in range(n_slabs):
                perm_ref[c] = slabs[c]
        h = _rmsnorm_rows(jnp.concatenate(slabs, axis=1), nw_ref[...]).astype(BF16)
        if dil == 1:
            cos_src, sin_src = cos_ref, sin_ref
        else:
            tab_src = (cos_ref, sin_ref) if prev_dil == 1 else (tab_ref.at[g - 2, 0],
                                                                tab_ref.at[g - 2, 1])
            for t in range(2):
                tab_ref[g - 1, t] = residue_major(tab_src[t], prev_dil, dil)
            cos_src, sin_src = tab_ref.at[g - 1, 0], tab_ref.at[g - 1, 1]
        if dil > 1:
            prev_dil = dil
        for part in range(3):
            o_ref = out_refs[3 * g + part]
            for j in range(n_chunks):
                col = (3 * g + part) * ATTN_WIDTH + j * QKV_CHUNK
                acc = _dot(h, w_ref[:, col:col + QKV_CHUNK])
                for half in range(QKV_CHUNK // HEAD_DIM):
                    lo = j * QKV_CHUNK + half * HEAD_DIM
                    for rb in range(row_blocks):
                        rs = slice(rb * ATTN_BLOCK, (rb + 1) * ATTN_BLOCK)
                        t = acc[rs, half * HEAD_DIM:(half + 1) * HEAD_DIM]
                        if part < 2:
                            t = t * cos_src[rs, :] + pltpu.roll(t, HEAD_DIM // 2, 1) * sin_src[rs, :]
                        if part == 0:
                            t = t * Q_SCALE
                        t = t.astype(BF16)
                        if n >= ATTN_BLOCK:
                            first = rb * ATTN_BLOCK
                            o_ref[first // n, pl.ds(first % n, ATTN_BLOCK), lo:lo + HEAD_DIM] = t
                        else:
                            for k in range(ATTN_BLOCK // n):
                                o_ref[rb * (ATTN_BLOCK // n) + k, :, lo:lo + HEAD_DIM] = (
                                    t[k * n:(k + 1) * n])
    _cast_blocks(rider_in, rider_out)


def _qkv_project(x, nw, w_in, cos2, sin2, riders, rows):
    B, S, D = x.shape
    n_blocks = S // rows
    out_shapes, out_specs = [], []
    for _, dil in DILATED_GROUPS:
        for _ in range(3):
            out_shapes.append(jax.ShapeDtypeStruct((B, dil, S // dil, ATTN_WIDTH), BF16))
            out_specs.append(pl.BlockSpec((None, dil, rows // dil, ATTN_WIDTH),
                                          lambda i: (i // n_blocks, 0, i % n_blocks, 0)))
    rider_in, rider_out, rider_shapes = _rider_specs(riders, B * n_blocks)
    tab_spec = pl.BlockSpec((rows, HEAD_DIM), lambda i: (i % n_blocks, 0))
    const = lambda shape: pl.BlockSpec(shape, lambda i: (0, 0), pipeline_mode=pl.Buffered(1))
    xf = x.reshape(B * S, D)
    slab_specs = [pl.BlockSpec((rows, LANES), lambda i, c=c: (i, c)) for c in range(D // LANES)]
    outs = pl.pallas_call(
        partial(_qkv_kernel, n_riders=len(riders)),
        grid=(B * n_blocks,),
        in_specs=slab_specs + [const((1, D)), const((D, w_in.shape[1])), tab_spec, tab_spec]
        + rider_in,
        out_specs=out_specs + rider_out,
        out_shape=out_shapes + rider_shapes,
        scratch_shapes=[pltpu.VMEM((len(DILATED_GROUPS) - 1, 2, rows, HEAD_DIM), F32),
                        pltpu.VMEM((D // LANES, rows, LANES), F32)],
        compiler_params=_params("parallel"),
        name="attn_qkv",
    )(*([xf] * (D // LANES)), nw, w_in, cos2, sin2, *[w for w, _, _ in riders])
    return outs[:len(out_shapes)], outs[len(out_shapes):]


def _band_bias():
    qi = np.arange(ATTN_BLOCK)[:, None]
    kj = np.arange(2 * ATTN_BLOCK)[None, :]
    rel = qi - kj + ATTN_BLOCK
    band = (rel >= 0) & (rel <= ATTN_BLOCK)
    masks = np.stack([band & (kj >= ATTN_BLOCK), band])
    return jnp.asarray(np.where(masks, 0.0, NEG_BIG).astype(np.float32))


def _attn_kernel(q_ref, k_ref, v_ref, kh_ref, vh_ref, bias_ref, o_ref, lse_ref):
    first_bias = jnp.where(pl.program_id(2) > 0, 1, 0)
    lane = lax.broadcasted_iota(jnp.int32, (ATTN_BLOCK, LANES), 1)
    n_res, n_rows, _ = q_ref.shape
    for r in range(n_res):
        for c in range(n_rows // ATTN_BLOCK):
            rows = slice(c * ATTN_BLOCK, (c + 1) * ATTN_BLOCK)
            bias = bias_ref.at[first_bias if c == 0 else 1]
            lse_tile = jnp.zeros((ATTN_BLOCK, LANES), F32)
            for h in range(ATTN_HEADS):
                cols = slice(h * HEAD_DIM, (h + 1) * HEAD_DIM)
                q = q_ref[r, rows, cols]
                if c == 0:
                    kk = jnp.concatenate([kh_ref[r, :, cols], k_ref[r, rows, cols]], axis=0)
                    vv = jnp.concatenate([vh_ref[r, :, cols], v_ref[r, rows, cols]], axis=0)
                else:
                    hist = slice((c - 1) * ATTN_BLOCK, (c + 1) * ATTN_BLOCK)
                    kk = k_ref[r, hist, cols]
                    vv = v_ref[r, hist, cols]
                s = _dot_nt(q, kk) + bias[...]
                m = jnp.max(s, axis=-1, keepdims=True)
                p = jnp.exp2(s - m)
                l = jnp.sum(p, axis=-1, keepdims=True)
                o = _dot(p.astype(BF16), vv) / l
                o_ref[r, rows, cols] = o.astype(BF16)
                lse_tile = jnp.where(lane == h, m + jnp.log2(l), lse_tile)
            lse_ref[r, rows, :] = lse_tile


def _group_attention(q, k, v, step_rows):
    B, dil, L, W = q.shape
    rows = min(step_rows, L)
    n_res = min(step_rows // rows, dil)
    n_sub = rows // ATTN_BLOCK
    blk = lambda width: pl.BlockSpec((None, n_res, rows, width), lambda b, r, i: (b, r, i, 0))
    halo = pl.BlockSpec((None, n_res, ATTN_BLOCK, W),
                        lambda b, r, i: (b, r, jnp.maximum(i * n_sub - 1, 0), 0))
    return pl.pallas_call(
        _attn_kernel,
        grid=(B, dil // n_res, L // rows),
        in_specs=[blk(W), blk(W), blk(W), halo, halo,
                  pl.BlockSpec((2, ATTN_BLOCK, 2 * ATTN_BLOCK), lambda b, r, i: (0, 0, 0))],
        out_specs=[blk(W), blk(LANES)],
        out_shape=[
            jax.ShapeDtypeStruct((B, dil, L, W), BF16),
            jax.ShapeDtypeStruct((B, dil, L, LANES), F32),
        ],
        compiler_params=_params("parallel", "parallel", "parallel"),
        name="attn_band",
    )(q, k, v, k, v, _band_bias())


MLP_FF_CHUNK = 1024


def _project_and_mlp(mixed, x_ref, wo_ref, nw_ref, wu_ref, wd_ref, side_work=None):
    x_mid = x_ref[...] + _dot(mixed, wo_ref[...])
    h = _rmsnorm_rows(x_mid, nw_ref[...]).astype(BF16)
    acc = x_mid
    for f in range(D_FF // MLP_FF_CHUNK):
        cols = slice(f * MLP_FF_CHUNK, (f + 1) * MLP_FF_CHUNK)
        a = jnp.maximum(_dot(h, wu_ref[:, cols]), 0.0)
        acc = acc + _dot((a * a).astype(BF16), wd_ref[cols, :])
        if side_work is not None:
            side_work(f)
    return acc


def _attn_tail_kernel(*refs, n_riders):
    n_groups = len(DILATED_GROUPS)
    o_refs = refs[:n_groups]
    lse_refs = refs[n_groups:2 * n_groups]
    rest = refs[2 * n_groups:]
    x_ref, wo_ref, nw_ref, wu_ref, wd_ref = rest[:5]
    rider_in = rest[5:5 + n_riders]
    out_ref = rest[5 + n_riders]
    rider_out = rest[6 + n_riders:6 + 2 * n_riders]
    merged_ref, nat_ref = rest[6 + 2 * n_riders:]
    rows = x_ref.shape[0]
    step = pl.program_id(0)
    write_slot = step % 2
    read_slot = 1 - write_slot

    @pl.when(step == 0)
    def _():
        merged_ref[1] = jnp.zeros(merged_ref.shape[1:], BF16)

    def natural(src_ref, col, dil, slot):
        lanes = slice(col * LANES, (col + 1) * LANES)
        if dil == 1:
            return src_ref[0, :, lanes].astype(F32)
        for r in range(dil):
            nat_ref[slot, pl.ds(r, rows // dil, stride=dil), :] = src_ref[r, :, lanes].astype(F32)
        return nat_ref[slot]

    mixed = merged_ref[read_slot]
    heads_per_chunk = ATTN_HEADS // (D_FF // MLP_FF_CHUNK)
    alphas = []

    def merge_heads(f):
        if not alphas:
            lses = [natural(lse_refs[g], 0, dil, g) for g, (_, dil) in enumerate(DILATED_GROUPS)]
            m = jnp.maximum(jnp.maximum(lses[0], lses[1]), lses[2])
            es = [jnp.exp2(l - m) for l in lses]
            inv = 1.0 / (es[0] + es[1] + es[2])
            alphas.extend(e * inv for e in es)
        for h in range(f * heads_per_chunk, (f + 1) * heads_per_chunk):
            o = None
            for g, (_, dil) in enumerate(DILATED_GROUPS):
                slot = n_groups + 2 * g + h % 2
                term = alphas[g][:, h:h + 1] * natural(o_refs[g], h, dil, slot)
                o = term if o is None else o + term
            merged_ref[write_slot, :, h * HEAD_DIM:(h + 1) * HEAD_DIM] = o.astype(BF16)

    out_ref[...] = _project_and_mlp(mixed, x_ref, wo_ref, nw_ref, wu_ref, wd_ref,
                                    side_work=merge_heads)
    _cast_blocks(rider_in, rider_out)


def _attn_tail(o_list, lse_list, x, w_out, nw, w_up, w_down, riders, rows):
    B, S, D = x.shape
    W = ATTN_WIDTH
    n_blocks = S // rows
    n_groups = len(DILATED_GROUPS)
    n_tiles = B * n_blocks

    def residue_spec(dil, width):
        def index(s):
            t = jnp.minimum(s, n_tiles - 1)
            return (t // n_blocks, 0, t % n_blocks, 0)
        return pl.BlockSpec((None, dil, rows // dil, width), index)

    o_specs = [residue_spec(dil, W) for _, dil in DILATED_GROUPS]
    lse_specs = [residue_spec(dil, LANES) for _, dil in DILATED_GROUPS]
    rider_in, rider_out, rider_shapes = _rider_specs(riders, n_tiles)
    row_spec = pl.BlockSpec((rows, D), lambda s: (jnp.maximum(s - 1, 0), 0))
    const = lambda shape: pl.BlockSpec(shape, lambda s: (0, 0), pipeline_mode=pl.Buffered(1))
    outs = pl.pallas_call(
        partial(_attn_tail_kernel, n_riders=len(riders)),
        grid=(n_tiles + 1,),
        in_specs=o_specs + lse_specs + [row_spec, const((W, D)), const((1, D)),
                                        const((D, D_FF)), const((D_FF, D))] + rider_in,
        out_specs=[row_spec] + rider_out,
        out_shape=[jax.ShapeDtypeStruct((B * S, D), F32)] + rider_shapes,
        scratch_shapes=[pltpu.VMEM((2, rows, W), BF16),
                        pltpu.VMEM((3 * n_groups, rows, LANES), F32)],
        compiler_params=_params("arbitrary"),
        name="attn_tail",
    )(*o_list, *lse_list, x.reshape(B * S, D), w_out, nw, w_up, w_down,
      *[w for w, _, _ in riders])
    return outs[0].reshape(B, S, D), outs[1:]


def _gla_tail_kernel(o_ref, x_ref, wo_ref, nw_ref, wu_ref, wd_ref, nwf_ref, out_ref):
    acc = _project_and_mlp(o_ref[...], x_ref, wo_ref, nw_ref, wu_ref, wd_ref)
    out_ref[...] = _rmsnorm_rows(acc, nwf_ref[...])


def _gla_tail(o, x, w_out, nw, w_up, w_down, nw_final, rows):
    B, S, D = x.shape
    T = B * S
    row_spec = lambda width: pl.BlockSpec((rows, width), lambda i: (i, 0))
    const = lambda shape: pl.BlockSpec(shape, lambda i: (0, 0), pipeline_mode=pl.Buffered(1))
    out = pl.pallas_call(
        _gla_tail_kernel,
        grid=(T // rows,),
        in_specs=[row_spec(o.shape[-1]), row_spec(D), const((o.shape[-1], D)), const((1, D)),
                  const((D, D_FF)), const((D_FF, D)), const((1, D))],
        out_specs=row_spec(D),
        out_shape=jax.ShapeDtypeStruct((T, D), F32),
        compiler_params=_params("parallel"),
        name="gla_tail",
    )(o, x.reshape(T, D), w_out, nw, w_up, w_down, nw_final)
    return out.reshape(B, S, D)


def _gla_pair_constants():
    pair = 2 * GLA_CHUNK
    ri = lax.broadcasted_iota(jnp.int32, (pair, pair), 0)
    cj = lax.broadcasted_iota(jnp.int32, (pair, pair), 1)
    causal = ri >= cj
    cumsum_mat = (causal & ((ri >= GLA_CHUNK) == (cj >= GLA_CHUNK))).astype(BF16)
    first = lax.broadcasted_iota(jnp.int32, (pair, GLA_KEY_DIM), 0) < GLA_CHUNK
    return pair, causal, cumsum_mat, first


def _gla_in_kernel(x_ref, nw_ref, w_ref, wg_ref, wgu_ref, bg_ref,
                   qsc_ref, ksc_ref, qst_ref, kup_ref, v_ref, g_ref, la_ref, q_ref, k_ref):
    h = _rmsnorm_rows(x_ref[...], nw_ref[...]).astype(BF16)
    gate_lr = _dot(h, wg_ref[...].astype(BF16)).astype(BF16)
    gk = _dot(gate_lr, wgu_ref[...].astype(BF16)) + bg_ref[...]
    log_sig = jnp.minimum(gk, 0.0) - jnp.log(1.0 + jnp.exp(-jnp.abs(gk)))
    la_ref[...] = log_sig * (1.0 / GLA_GATE_NORMALIZER)
    q_ref[...] = _dot(h, w_ref[:, :GLA_KEY_DIM])
    k_ref[...] = _dot(h, w_ref[:, GLA_KEY_DIM:2 * GLA_KEY_DIM])
    v_lo = 2 * GLA_KEY_DIM
    pair, _, cumsum_mat, first = _gla_pair_constants()
    n_pairs = x_ref.shape[0] // pair
    width = 2 * GLA_VAL_DIM // n_pairs
    for p in range(n_pairs):
        col = p * width
        dst_ref, lo = (v_ref, col) if col < GLA_VAL_DIM else (g_ref, col - GLA_VAL_DIM)
        dst_ref[:, lo:lo + width] = _dot(h, w_ref[:, v_lo + col:v_lo + col + width]).astype(dst_ref.dtype)
        rows = slice(p * pair, (p + 1) * pair)
        la = la_ref[rows, :]
        la_hi = la.astype(BF16)
        la_lo = (la - la_hi.astype(F32)).astype(BF16)
        b = _dot(cumsum_mat, la_hi) + _dot(cumsum_mat, la_lo)
        bl0 = b[GLA_CHUNK - 1:GLA_CHUNK, :]
        bl1 = b[pair - 1:pair, :]
        ref_b = jnp.where(first, b - bl0, b)
        q_e = q_ref[rows, :] * jnp.exp(ref_b)
        k_e = k_ref[rows, :] * jnp.exp(-ref_b)
        scale = GLA_DK ** -0.5
        qsc_ref[rows, :] = (q_e * scale).astype(BF16)
        ksc_ref[rows, :] = k_e.astype(BF16)
        qst_ref[rows, :] = (q_e * (jnp.exp(bl0) * scale)).astype(BF16)
        kup_ref[rows, :] = (k_e * jnp.exp(bl1)).astype(BF16)


def _gla_in_project(x, nw, w_main, w_gate, w_gate_up, b_gate, rows):
    B, S, D = x.shape
    T = B * S
    row_spec = lambda width: pl.BlockSpec((rows, width), lambda i: (i, 0))
    const = lambda shape: pl.BlockSpec(shape, lambda i: (0, 0))
    n_main = 2 * GLA_KEY_DIM + 2 * GLA_VAL_DIM
    key = jax.ShapeDtypeStruct((T, GLA_KEY_DIM), BF16)
    return pl.pallas_call(
        _gla_in_kernel,
        grid=(T // rows,),
        in_specs=[row_spec(D), const((1, D)), const((D, n_main)), const((D, GLA_GATE_RANK)),
                  const((GLA_GATE_RANK, GLA_KEY_DIM)), const((1, GLA_KEY_DIM))],
        out_specs=[row_spec(GLA_KEY_DIM)] * 4 + [row_spec(GLA_VAL_DIM), row_spec(GLA_VAL_DIM),
                                                 row_spec(GLA_KEY_DIM)],
        out_shape=[key, key, key, key,
                   jax.ShapeDtypeStruct((T, GLA_VAL_DIM), BF16),
                   jax.ShapeDtypeStruct((T, GLA_VAL_DIM), F32),
                   jax.ShapeDtypeStruct((T, GLA_KEY_DIM), F32)],
        scratch_shapes=[pltpu.VMEM((rows, GLA_KEY_DIM), F32), pltpu.VMEM((rows, GLA_KEY_DIM), F32)],
        compiler_params=_params("parallel"),
        name="gla_in",
    )(x.reshape(T, D), nw, w_main, w_gate, w_gate_up, b_gate)


def _gla_kernel(qsc_ref, ksc_ref, qst_ref, kup_ref, v_ref, g_ref, la_ref, nw_ref, o_ref, state_ref):
    @pl.when(pl.program_id(0) == 0)
    def _():
        state_ref[...] = jnp.zeros_like(state_ref)

    n_seq, n_rows, _ = qsc_ref.shape
    pair, causal, _, _ = _gla_pair_constants()
    ones = jnp.ones((pair, LANES), BF16)
    nw = nw_ref[...]
    for p in range(n_rows // pair):
        rows = slice(p * pair, (p + 1) * pair)
        for h in range(GLA_HEADS):
            kc = slice(h * GLA_DK, (h + 1) * GLA_DK)
            vc = slice(h * GLA_DV, (h + 1) * GLA_DV)
            for s in range(n_seq):
                v = v_ref[s, rows, vc]
                a = jnp.where(causal, _dot_nt(qsc_ref[s, rows, kc], ksc_ref[s, rows, kc]), 0.0)
                state = state_ref[s, h]
                o = _dot(jnp.concatenate([a.astype(BF16), qst_ref[s, rows, kc]], axis=1),
                         jnp.concatenate([v, state.astype(BF16)], axis=0))
                la = la_ref[s, rows, kc]
                la_hi = la.astype(BF16)
                la_lo = (la - la_hi.astype(F32)).astype(BF16)
                decay = jnp.exp(_dot_tn(la_hi, ones) + _dot_tn(la_lo, ones))
                decay = jnp.concatenate([decay] * (GLA_DV // LANES), axis=1)
                state_ref[s, h] = decay * state + _dot_tn(kup_ref[s, rows, kc], v)
                var = jnp.mean(o * o, axis=-1, keepdims=True)
                o = o * lax.rsqrt(var + NORM_EPS) * nw
                half_g = 0.5 * g_ref[s, rows, vc]
                o_ref[s, rows, vc] = (o * (half_g * (1.0 + jnp.tanh(half_g)))).astype(BF16)


def _gla_recurrence(q_sc, k_sc, q_st, k_up, v, g, la, nw, B, S, rows):
    seq_spec = lambda width: pl.BlockSpec((B, rows, width), lambda i: (0, i, 0))
    by_seq = lambda t: t.reshape(B, S, t.shape[-1])
    out = pl.pallas_call(
        _gla_kernel,
        grid=(S // rows,),
        in_specs=[seq_spec(GLA_KEY_DIM)] * 4 + [seq_spec(GLA_VAL_DIM), seq_spec(GLA_VAL_DIM),
                                                seq_spec(GLA_KEY_DIM),
                                                pl.BlockSpec((1, GLA_DV), lambda i: (0, 0))],
        out_specs=seq_spec(GLA_VAL_DIM),
        out_shape=jax.ShapeDtypeStruct((B, S, GLA_VAL_DIM), BF16),
        scratch_shapes=[pltpu.VMEM((B, GLA_HEADS, GLA_DK, GLA_DV), F32)],
        compiler_params=_params("arbitrary"),
        name="gla_recurrence",
    )(by_seq(q_sc), by_seq(k_sc), by_seq(q_st), by_seq(k_up), by_seq(v), by_seq(g), by_seq(la), nw)
    return out.reshape(B * S, GLA_VAL_DIM)


def _rope_tables(seq_len):
    pos = np.arange(seq_len, dtype=np.float64)
    inv_freq = ROPE_THETA ** (-np.arange(0, HEAD_DIM, 2, dtype=np.float64) / HEAD_DIM)
    ang = pos[:, None] * inv_freq[None, :]
    cos, sin = np.cos(ang), np.sin(ang)
    cos2 = np.concatenate([cos, cos], axis=-1).astype(np.float32)
    sin2 = np.concatenate([-sin, sin], axis=-1).astype(np.float32)
    return jnp.asarray(cos2), jnp.asarray(sin2)


def kernel(x, norm_mix_w, norm_mlp_w, final_norm_w, attn_w_in, attn_w_out, gla_w_in,
           gla_w_gate_up, gla_b_gate, gla_norm_w, gla_w_out, mlp_w_up, mlp_w_down):
    B, S, D = x.shape
    row = lambda w: w.reshape(1, -1).astype(F32)

    cos2, sin2 = _rope_tables(S)
    qkv, (w_out0, w_up0, w_down0) = _qkv_project(
        x, row(norm_mix_w[0]), attn_w_in[0].astype(BF16), cos2, sin2,
        riders=[(attn_w_out, 0, D), (mlp_w_up, 0, D_FF), (mlp_w_down, 0, D)], rows=RESIDENT_ROWS)
    outs, lses = [], []
    for g, (window, dil) in enumerate(DILATED_GROUPS):
        assert window // dil == ATTN_BLOCK and (S // dil) % ATTN_BLOCK == 0
        q, k, v = qkv[3 * g:3 * g + 3]
        o, lse = _group_attention(q, k, v, step_rows=STREAM_ROWS)
        outs.append(o)
        lses.append(lse)
    n_main = 2 * GLA_KEY_DIM + 2 * GLA_VAL_DIM
    x, (w_gla, w_out1, w_up1, w_down1) = _attn_tail(
        outs, lses, x, w_out0, row(norm_mlp_w[0]), w_up0, w_down0,
        riders=[(gla_w_in, 0, n_main), (gla_w_out, 0, D), (mlp_w_up, 1, D_FF), (mlp_w_down, 1, D)],
        rows=RESIDENT_ROWS)

    prepared = _gla_in_project(x, row(norm_mix_w[1]), w_gla, gla_w_in[0][:, n_main:],
                               gla_w_gate_up[0], row(gla_b_gate[0]), rows=STREAM_ROWS)
    o = _gla_recurrence(*prepared, row(gla_norm_w[0]), B, S, rows=RESIDENT_ROWS)
    return _gla_tail(o, x, w_out1, row(norm_mlp_w[1]), w_up1, w_down1, row(final_norm_w),
                     rows=RESIDENT_ROWS)
```

```python
from functools import partial

import jax
import jax.numpy as jnp
import numpy as np
from jax import lax
from jax.experimental import pallas as pl
from jax.experimental.pallas import tpu as pltpu

D_MODEL = 1024
NORM_EPS = 1e-5
DILATED_GROUPS = ((128, 1), (512, 4), (2048, 16))
ATTN_HEADS = 8
HEAD_DIM = 128
ATTN_WIDTH = ATTN_HEADS * HEAD_DIM
ATTN_BLOCK = 128
ROPE_THETA = 10000.0
GLA_HEADS = 4
GLA_DK = 128
GLA_DV = 256
GLA_KEY_DIM = GLA_HEADS * GLA_DK
GLA_VAL_DIM = GLA_HEADS * GLA_DV
GLA_GATE_RANK = 16
GLA_GATE_NORMALIZER = 16.0
GLA_CHUNK = 64
D_FF = 4 * D_MODEL

LANES = 128
VMEM_LIMIT_BYTES = 56 * 1024 * 1024
BF16_SUBLANES = 16
RESIDENT_ROWS = 512
STREAM_ROWS = 1024
NEG_BIG = -1e30

BF16 = jnp.bfloat16
F32 = jnp.float32


def _params(*semantics):
    return pltpu.CompilerParams(dimension_semantics=semantics,
                                vmem_limit_bytes=VMEM_LIMIT_BYTES)


def _rmsnorm_rows(x, w):
    var = jnp.mean(x * x, axis=-1, keepdims=True)
    return x * lax.rsqrt(var + NORM_EPS) * w


def _dot(a, b):
    return jnp.dot(a, b, preferred_element_type=F32)


def _dot_nt(a, b):
    return lax.dot_general(a, b, (((1,), (1,)), ((), ())), preferred_element_type=F32)


def _dot_tn(a, b):
    return lax.dot_general(a, b, (((0,), (0,)), ((), ())), preferred_element_type=F32)


def _cast_blocks(src_refs, dst_refs):
    for src_ref, dst_ref in zip(src_refs, dst_refs):
        dst_ref[...] = src_ref[:, :dst_ref.shape[1]].astype(BF16)


def _rider_specs(riders, n_steps):
    in_specs, out_specs, out_shapes = [], [], []
    for w, layer, n_keep in riders:
        _, k, n = w.shape
        assert k % n_steps == 0 and (k // n_steps) % BF16_SUBLANES == 0
        blk = k // n_steps
        index = lambda s, layer=layer: (layer, jnp.minimum(s, n_steps - 1), 0)
        in_specs.append(pl.BlockSpec((None, blk, n), index))
        out_specs.append(pl.BlockSpec((blk, n_keep), lambda s: (jnp.minimum(s, n_steps - 1), 0)))
        out_shapes.append(jax.ShapeDtypeStruct((k, n_keep), BF16))
    return in_specs, out_specs, out_shapes


QKV_CHUNK = 256
Q_SCALE = HEAD_DIM ** -0.5 * float(np.log2(np.e))


def _qkv_kernel(*refs, n_riders):
    n_slabs = D_MODEL // LANES
    n_out = 3 * len(DILATED_GROUPS)
    x_refs = refs[:n_slabs]
    nw_ref, w_ref, cos_ref, sin_ref = refs[n_slabs:n_slabs + 4]
    rider_in = refs[n_slabs + 4:n_slabs + 4 + n_riders]
    out_refs = refs[n_slabs + 4 + n_riders:n_slabs + 4 + n_riders + n_out]
    rider_out = refs[n_slabs + 4 + n_riders + n_out:-2]
    tab_ref, perm_ref = refs[-2:]
    rows = x_refs[0].shape[0]
    n_chunks = ATTN_WIDTH // QKV_CHUNK
    row_blocks = rows // ATTN_BLOCK

    def residue_major(src_ref, src_dil, dil):
        if dil == src_dil:
            return src_ref[...]
        pieces = [src_ref[pl.ds((r % src_dil) * (rows // src_dil) + r // src_dil, rows // dil,
                                stride=dil // src_dil), :] for r in range(dil)]
        return jnp.concatenate(pieces, axis=0)

    prev_dil = 1
    for g, (_, dil) in enumerate(DILATED_GROUPS):
        n = rows // dil
        x_src = x_refs if prev_dil == 1 else [perm_ref.at[c] for c in range(n_slabs)]
        slabs = [residue_major(src, prev_dil, dil) for src in x_src]
        if 1 < dil < DILATED_GROUPS[-1][1]:
            for c in range(n_slabs):
                perm_ref[c] = slabs[c]
        h = _rmsnorm_rows(jnp.concatenate(slabs, axis=1), nw_ref[...]).astype(BF16)
        if dil == 1:
            cos_src, sin_src = cos_ref, sin_ref
        else:
            tab_src = (cos_ref, sin_ref) if prev_dil == 1 else (tab_ref.at[g - 2, 0],
                                                                tab_ref.at[g - 2, 1])
            for t in range(2):
                tab_ref[g - 1, t] = residue_major(tab_src[t], prev_dil, dil)
            cos_src, sin_src = tab_ref.at[g - 1, 0], tab_ref.at[g - 1, 1]
        if dil > 1:
            prev_dil = dil
        for part in range(3):
            o_ref = out_refs[3 * g + part]
            for j in range(n_chunks):
                col = (3 * g + part) * ATTN_WIDTH + j * QKV_CHUNK
                acc = _dot(h, w_ref[:, col:col + QKV_CHUNK])
                for half in range(QKV_CHUNK // HEAD_DIM):
                    lo = j * QKV_CHUNK + half * HEAD_DIM
                    for rb in range(row_blocks):
                        rs = slice(rb * ATTN_BLOCK, (rb + 1) * ATTN_BLOCK)
                        t = acc[rs, half * HEAD_DIM:(half + 1) * HEAD_DIM]
                        if part < 2:
                            t = t * cos_src[rs, :] + pltpu.roll(t, HEAD_DIM // 2, 1) * sin_src[rs, :]
                        if part == 0:
                            t = t * Q_SCALE
                        t = t.astype(BF16)
                        if n >= ATTN_BLOCK:
                            first = rb * ATTN_BLOCK
                            o_ref[first // n, pl.ds(first % n, ATTN_BLOCK), lo:lo + HEAD_DIM] = t
                        else:
                            for k in range(ATTN_BLOCK // n):
                                o_ref[rb * (ATTN_BLOCK // n) + k, :, lo:lo + HEAD_DIM] = (
                                    t[k * n:(k + 1) * n])
    _cast_blocks(rider_in, rider_out)


def _qkv_project(x, nw, w_in, cos2, sin2, riders, rows):
    B, S, D = x.shape
    n_blocks = S // rows
    out_shapes, out_specs = [], []
    for _, dil in DILATED_GROUPS:
        for _ in range(3):
            out_shapes.append(jax.ShapeDtypeStruct((B, dil, S // dil, ATTN_WIDTH), BF16))
            out_specs.append(pl.BlockSpec((None, dil, rows // dil, ATTN_WIDTH),
                                          lambda i: (i // n_blocks, 0, i % n_blocks, 0)))
    rider_in, rider_out, rider_shapes = _rider_specs(riders, B * n_blocks)
    tab_spec = pl.BlockSpec((rows, HEAD_DIM), lambda i: (i % n_blocks, 0))
    const = lambda shape: pl.BlockSpec(shape, lambda i: (0, 0), pipeline_mode=pl.Buffered(1))
    xf = x.reshape(B * S, D)
    slab_specs = [pl.BlockSpec((rows, LANES), lambda i, c=c: (i, c)) for c in range(D // LANES)]
    outs = pl.pallas_call(
        partial(_qkv_kernel, n_riders=len(riders)),
        grid=(B * n_blocks,),
        in_specs=slab_specs + [const((1, D)), const((D, w_in.shape[1])), tab_spec, tab_spec]
        + rider_in,
        out_specs=out_specs + rider_out,
        out_shape=out_shapes + rider_shapes,
        scratch_shapes=[pltpu.VMEM((len(DILATED_GROUPS) - 1, 2, rows, HEAD_DIM), F32),
                        pltpu.VMEM((D // LANES, rows, LANES), F32)],
        compiler_params=_params("parallel"),
        name="attn_qkv",
    )(*([xf] * (D // LANES)), nw, w_in, cos2, sin2, *[w for w, _, _ in riders])
    return outs[:len(out_shapes)], outs[len(out_shapes):]


def _band_bias():
    qi = np.arange(ATTN_BLOCK)[:, None]
    kj = np.arange(2 * ATTN_BLOCK)[None, :]
    rel = qi - kj + ATTN_BLOCK
    band = (rel >= 0) & (rel <= ATTN_BLOCK)
    masks = np.stack([band & (kj >= ATTN_BLOCK), band])
    return jnp.asarray(np.where(masks, 0.0, NEG_BIG).astype(np.float32))


def _attn_kernel(q_ref, k_ref, v_ref, kh_ref, vh_ref, bias_ref, o_ref, lse_ref):
    first_bias = jnp.where(pl.program_id(2) > 0, 1, 0)
    lane = lax.broadcasted_iota(jnp.int32, (ATTN_BLOCK, LANES), 1)
    n_res, n_rows, _ = q_ref.shape
    for r in range(n_res):
        for c in range(n_rows // ATTN_BLOCK):
            rows = slice(c * ATTN_BLOCK, (c + 1) * ATTN_BLOCK)
            bias = bias_ref.at[first_bias if c == 0 else 1]
            lse_tile = jnp.zeros((ATTN_BLOCK, LANES), F32)
            for h in range(ATTN_HEADS):
                cols = slice(h * HEAD_DIM, (h + 1) * HEAD_DIM)
                q = q_ref[r, rows, cols]
                if c == 0:
                    kk = jnp.concatenate([kh_ref[r, :, cols], k_ref[r, rows, cols]], axis=0)
                    vv = jnp.concatenate([vh_ref[r, :, cols], v_ref[r, rows, cols]], axis=0)
                else:
                    hist = slice((c - 1) * ATTN_BLOCK, (c + 1) * ATTN_BLOCK)
                    kk = k_ref[r, hist, cols]
                    vv = v_ref[r, hist, cols]
                s = _dot_nt(q, kk) + bias[...]
                m = jnp.max(s, axis=-1, keepdims=True)
                p = jnp.exp2(s - m)
                l = jnp.sum(p, axis=-1, keepdims=True)
                o = _dot(p.astype(BF16), vv) / l
                o_ref[r, rows, cols] = o.astype(BF16)
                lse_tile = jnp.where(lane == h, m + jnp.log2(l), lse_tile)
            lse_ref[r, rows, :] = lse_tile


def _group_attention(q, k, v, step_rows):
    B, dil, L, W = q.shape
    rows = min(step_rows, L)
    n_res = min(step_rows // rows, dil)
    n_sub = rows // ATTN_BLOCK
    blk = lambda width: pl.BlockSpec((None, n_res, rows, width), lambda b, r, i: (b, r, i, 0))
    halo = pl.BlockSpec((None, n_res, ATTN_BLOCK, W),
                        lambda b, r, i: (b, r, jnp.maximum(i * n_sub - 1, 0), 0))
    return pl.pallas_call(
        _attn_kernel,
        grid=(B, dil // n_res, L // rows),
        in_specs=[blk(W), blk(W), blk(W), halo, halo,
                  pl.BlockSpec((2, ATTN_BLOCK, 2 * ATTN_BLOCK), lambda b, r, i: (0, 0, 0))],
        out_specs=[blk(W), blk(LANES)],
        out_shape=[
            jax.ShapeDtypeStruct((B, dil, L, W), BF16),
            jax.ShapeDtypeStruct((B, dil, L, LANES), F32),
        ],
        compiler_params=_params("parallel", "parallel", "parallel"),
        name="attn_band",
    )(q, k, v, k, v, _band_bias())


MLP_FF_CHUNK = 1024


def _project_and_mlp(mixed, x_ref, wo_ref, nw_ref, wu_ref, wd_ref, side_work=None):
    x_mid = x_ref[...] + _dot(mixed, wo_ref[...])
    h = _rmsnorm_rows(x_mid, nw_ref[...]).astype(BF16)
    acc = x_mid
    for f in range(D_FF // MLP_FF_CHUNK):
        cols = slice(f * MLP_FF_CHUNK, (f + 1) * MLP_FF_CHUNK)
        a = jnp.maximum(_dot(h, wu_ref[:, cols]), 0.0)
        acc = acc + _dot((a * a).astype(BF16), wd_ref[cols, :])
        if side_work is not None:
            side_work(f)
    return acc


def _attn_tail_kernel(*refs, n_riders):
    n_groups = len(DILATED_GROUPS)
    o_refs = refs[:n_groups]
    lse_refs = refs[n_groups:2 * n_groups]
    rest = refs[2 * n_groups:]
    x_ref, wo_ref, nw_ref, wu_ref, wd_ref = rest[:5]
    rider_in = rest[5:5 + n_riders]
    out_ref = rest[5 + n_riders]
    rider_out = rest[6 + n_riders:6 + 2 * n_riders]
    merged_ref, nat_ref = rest[6 + 2 * n_riders:]
    rows = x_ref.shape[0]
    step = pl.program_id(0)
    write_slot = step % 2
    read_slot = 1 - write_slot

    @pl.when(step == 0)
    def _():
        merged_ref[1] = jnp.zeros(merged_ref.shape[1:], BF16)

    def natural(src_ref, col, dil, slot):
        lanes = slice(col * LANES, (col + 1) * LANES)
        if dil == 1:
            return src_ref[0, :, lanes].astype(F32)
        for r in range(dil):
            nat_ref[slot, pl.ds(r, rows // dil, stride=dil), :] = src_ref[r, :, lanes].astype(F32)
        return nat_ref[slot]

    mixed = merged_ref[read_slot]
    heads_per_chunk = ATTN_HEADS // (D_FF // MLP_FF_CHUNK)
    alphas = []

    def merge_heads(f):
        if not alphas:
            lses = [natural(lse_refs[g], 0, dil, g) for g, (_, dil) in enumerate(DILATED_GROUPS)]
            m = jnp.maximum(jnp.maximum(lses[0], lses[1]), lses[2])
            es = [jnp.exp2(l - m) for l in lses]
            inv = 1.0 / (es[0] + es[1] + es[2])
            alphas.extend(e * inv for e in es)
        for h in range(f * heads_per_chunk, (f + 1) * heads_per_chunk):
            o = None
            for g, (_, dil) in enumerate(DILATED_GROUPS):
                slot = n_groups + 2 * g + h % 2
                term = alphas[g][:, h:h + 1] * natural(o_refs[g], h, dil, slot)
                o = term if o is None else o + term
            merged_ref[write_slot, :, h * HEAD_DIM:(h + 1) * HEAD_DIM] = o.astype(BF16)

    out_ref[...] = _project_and_mlp(mixed, x_ref, wo_ref, nw_ref, wu_ref, wd_ref,
                                    side_work=merge_heads)
    _cast_blocks(rider_in, rider_out)


def _attn_tail(o_list, lse_list, x, w_out, nw, w_up, w_down, riders, rows):
    B, S, D = x.shape
    W = ATTN_WIDTH
    n_blocks = S // rows
    n_groups = len(DILATED_GROUPS)
    n_tiles = B * n_blocks

    def residue_spec(dil, width):
        def index(s):
            t = jnp.minimum(s, n_tiles - 1)
            return (t // n_blocks, 0, t % n_blocks, 0)
        return pl.BlockSpec((None, dil, rows // dil, width), index)

    o_specs = [residue_spec(dil, W) for _, dil in DILATED_GROUPS]
    lse_specs = [residue_spec(dil, LANES) for _, dil in DILATED_GROUPS]
    rider_in, rider_out, rider_shapes = _rider_specs(riders, n_tiles)
    row_spec = pl.BlockSpec((rows, D), lambda s: (jnp.maximum(s - 1, 0), 0))
    const = lambda shape: pl.BlockSpec(shape, lambda s: (0, 0), pipeline_mode=pl.Buffered(1))
    outs = pl.pallas_call(
        partial(_attn_tail_kernel, n_riders=len(riders)),
        grid=(n_tiles + 1,),
        in_specs=o_specs + lse_specs + [row_spec, const((W, D)), const((1, D)),
                                        const((D, D_FF)), const((D_FF, D))] + rider_in,
        out_specs=[row_spec] + rider_out,
        out_shape=[jax.ShapeDtypeStruct((B * S, D), F32)] + rider_shapes,
        scratch_shapes=[pltpu.VMEM((2, rows, W), BF16),
                        pltpu.VMEM((3 * n_groups, rows, LANES), F32)],
        compiler_params=_params("arbitrary"),
        name="attn_tail",
    )(*o_list, *lse_list, x.reshape(B * S, D), w_out, nw, w_up, w_down,
      *[w for w, _, _ in riders])
    return outs[0].reshape(B, S, D), outs[1:]


def _gla_tail_kernel(o_ref, x_ref, wo_ref, nw_ref, wu_ref, wd_ref, nwf_ref, out_ref):
    acc = _project_and_mlp(o_ref[...], x_ref, wo_ref, nw_ref, wu_ref, wd_ref)
    out_ref[...] = _rmsnorm_rows(acc, nwf_ref[...])


def _gla_tail(o, x, w_out, nw, w_up, w_down, nw_final, rows):
    B, S, D = x.shape
    T = B * S
    row_spec = lambda width: pl.BlockSpec((rows, width), lambda i: (i, 0))
    const = lambda shape: pl.BlockSpec(shape, lambda i: (0, 0), pipeline_mode=pl.Buffered(1))
    out = pl.pallas_call(
        _gla_tail_kernel,
        grid=(T // rows,),
        in_specs=[row_spec(o.shape[-1]), row_spec(D), const((o.shape[-1], D)), const((1, D)),
                  const((D, D_FF)), const((D_FF, D)), const((1, D))],
        out_specs=row_spec(D),
        out_shape=jax.ShapeDtypeStruct((T, D), F32),
        compiler_params=_params("parallel"),
        name="gla_tail",
    )(o, x.reshape(T, D), w_out, nw, w_up, w_down, nw_final)
    return out.reshape(B, S, D)


def _gla_pair_constants():
    pair = 2 * GLA_CHUNK
    ri = lax.broadcasted_iota(jnp.int32, (pair, pair), 0)
    cj = lax.broadcasted_iota(jnp.int32, (pair, pair), 1)
    causal = ri >= cj
    cumsum_mat = (causal & ((ri >= GLA_CHUNK) == (cj >= GLA_CHUNK))).astype(BF16)
    first = lax.broadcasted_iota(jnp.int32, (pair, GLA_KEY_DIM), 0) < GLA_CHUNK
    return pair, causal, cumsum_mat, first


def _gla_in_kernel(x_ref, nw_ref, w_ref, wg_ref, wgu_ref, bg_ref,
                   qsc_ref, ksc_ref, qst_ref, kup_ref, v_ref, g_ref, la_ref, q_ref, k_ref):
    h = _rmsnorm_rows(x_ref[...], nw_ref[...]).astype(BF16)
    gate_lr = _dot(h, wg_ref[...].astype(BF16)).astype(BF16)
    gk = _dot(gate_lr, wgu_ref[...].astype(BF16)) + bg_ref[...]
    log_sig = jnp.minimum(gk, 0.0) - jnp.log(1.0 + jnp.exp(-jnp.abs(gk)))
    la_ref[...] = log_sig * (1.0 / GLA_GATE_NORMALIZER)
    q_ref[...] = _dot(h, w_ref[:, :GLA_KEY_DIM])
    k_ref[...] = _dot(h, w_ref[:, GLA_KEY_DIM:2 * GLA_KEY_DIM])
    v_lo = 2 * GLA_KEY_DIM
    g_lo = v_lo + GLA_VAL_DIM
    pair, _, cumsum_mat, first = _gla_pair_constants()
    n_pairs = x_ref.shape[0] // pair
    wide = [(v_ref, v_lo), (g_ref, g_lo)]
    for p in range(n_pairs):
        if p % (n_pairs // 2) == 0:
            dst_ref, col = wide[p // (n_pairs // 2)]
            dst_ref[...] = _dot(h, w_ref[:, col:col + GLA_VAL_DIM]).astype(dst_ref.dtype)
        rows = slice(p * pair, (p + 1) * pair)
        la = la_ref[rows, :]
        la_hi = la.astype(BF16)
        la_lo = (la - la_hi.astype(F32)).astype(BF16)
        b = _dot(cumsum_mat, la_hi) + _dot(cumsum_mat, la_lo)
        bl0 = b[GLA_CHUNK - 1:GLA_CHUNK, :]
        bl1 = b[pair - 1:pair, :]
        ref_b = jnp.where(first, b - bl0, b)
        q_e = q_ref[rows, :] * jnp.exp(ref_b)
        k_e = k_ref[rows, :] * jnp.exp(-ref_b)
        scale = GLA_DK ** -0.5
        qsc_ref[rows, :] = (q_e * scale).astype(BF16)
        ksc_ref[rows, :] = k_e.astype(BF16)
        qst_ref[rows, :] = (q_e * (jnp.exp(bl0) * scale)).astype(BF16)
        kup_ref[rows, :] = (k_e * jnp.exp(bl1)).astype(BF16)


def _gla_in_project(x, nw, w_main, w_gate, w_gate_up, b_gate, rows):
    B, S, D = x.shape
    T = B * S
    row_spec = lambda width: pl.BlockSpec((rows, width), lambda i: (i, 0))
    const = lambda shape: pl.BlockSpec(shape, lambda i: (0, 0))
    n_main = 2 * GLA_KEY_DIM + 2 * GLA_VAL_DIM
    key = jax.ShapeDtypeStruct((T, GLA_KEY_DIM), BF16)
    return pl.pallas_call(
        _gla_in_kernel,
        grid=(T // rows,),
        in_specs=[row_spec(D), const((1, D)), const((D, n_main)), const((D, GLA_GATE_RANK)),
                  const((GLA_GATE_RANK, GLA_KEY_DIM)), const((1, GLA_KEY_DIM))],
        out_specs=[row_spec(GLA_KEY_DIM)] * 4 + [row_spec(GLA_VAL_DIM), row_spec(GLA_VAL_DIM),
                                                 row_spec(GLA_KEY_DIM)],
        out_shape=[key, key, key, key,
                   jax.ShapeDtypeStruct((T, GLA_VAL_DIM), BF16),
                   jax.ShapeDtypeStruct((T, GLA_VAL_DIM), F32),
                   jax.ShapeDtypeStruct((T, GLA_KEY_DIM), F32)],
        scratch_shapes=[pltpu.VMEM((rows, GLA_KEY_DIM), F32), pltpu.VMEM((rows, GLA_KEY_DIM), F32)],
        compiler_params=_params("parallel"),
        name="gla_in",
    )(x.reshape(T, D), nw, w_main, w_gate, w_gate_up, b_gate)


def _gla_kernel(qsc_ref, ksc_ref, qst_ref, kup_ref, v_ref, g_ref, la_ref, nw_ref, o_ref, state_ref):
    @pl.when(pl.program_id(0) == 0)
    def _():
        state_ref[...] = jnp.zeros_like(state_ref)

    n_seq, n_rows, _ = qsc_ref.shape
    pair, causal, _, _ = _gla_pair_constants()
    ones = jnp.ones((pair, LANES), BF16)
    nw = nw_ref[...]
    for p in range(n_rows // pair):
        rows = slice(p * pair, (p + 1) * pair)
        for h in range(GLA_HEADS):
            kc = slice(h * GLA_DK, (h + 1) * GLA_DK)
            vc = slice(h * GLA_DV, (h + 1) * GLA_DV)
            for s in range(n_seq):
                v = v_ref[s, rows, vc]
                a = jnp.where(causal, _dot_nt(qsc_ref[s, rows, kc], ksc_ref[s, rows, kc]), 0.0)
                state = state_ref[s, h]
                o = _dot(jnp.concatenate([a.astype(BF16), qst_ref[s, rows, kc]], axis=1),
                         jnp.concatenate([v, state.astype(BF16)], axis=0))
                la = la_ref[s, rows, kc]
                la_hi = la.astype(BF16)
                la_lo = (la - la_hi.astype(F32)).astype(BF16)
                decay = jnp.exp(_dot_tn(la_hi, ones) + _dot_tn(la_lo, ones))
                decay = jnp.concatenate([decay] * (GLA_DV // LANES), axis=1)
                state_ref[s, h] = decay * state + _dot_tn(kup_ref[s, rows, kc], v)
                var = jnp.mean(o * o, axis=-1, keepdims=True)
                o = o * lax.rsqrt(var + NORM_EPS) * nw
                half_g = 0.5 * g_ref[s, rows, vc]
                o_ref[s, rows, vc] = (o * (half_g * (1.0 + jnp.tanh(half_g)))).astype(BF16)


def _gla_recurrence(q_sc, k_sc, q_st, k_up, v, g, la, nw, B, S, rows):
    seq_spec = lambda width: pl.BlockSpec((B, rows, width), lambda i: (0, i, 0))
    by_seq = lambda t: t.reshape(B, S, t.shape[-1])
    out = pl.pallas_call(
        _gla_kernel,
        grid=(S // rows,),
        in_specs=[seq_spec(GLA_KEY_DIM)] * 4 + [seq_spec(GLA_VAL_DIM), seq_spec(GLA_VAL_DIM),
                                                seq_spec(GLA_KEY_DIM),
                                                pl.BlockSpec((1, GLA_DV), lambda i: (0, 0))],
        out_specs=seq_spec(GLA_VAL_DIM),
        out_shape=jax.ShapeDtypeStruct((B, S, GLA_VAL_DIM), BF16),
        scratch_shapes=[pltpu.VMEM((B, GLA_HEADS, GLA_DK, GLA_DV), F32)],
        compiler_params=_params("arbitrary"),
        name="gla_recurrence",
    )(by_seq(q_sc), by_seq(k_sc), by_seq(q_st), by_seq(k_up), by_seq(v), by_seq(g), by_seq(la), nw)
    return out.reshape(B * S, GLA_VAL_DIM)


def _rope_tables(seq_len):
    pos = np.arange(seq_len, dtype=np.float64)
    inv_freq = ROPE_THETA ** (-np.arange(0, HEAD_DIM, 2, dtype=np.float64) / HEAD_DIM)
    ang = pos[:, None] * inv_freq[None, :]
    cos, sin = np.cos(ang), np.sin(ang)
    cos2 = np.concatenate([cos, cos], axis=-1).astype(np.float32)
    sin2 = np.concatenate([-sin, sin], axis=-1).astype(np.float32)
    return jnp.asarray(cos2), jnp.asarray(sin2)


def kernel(x, norm_mix_w, norm_mlp_w, final_norm_w, attn_w_in, attn_w_out, gla_w_in,
           gla_w_gate_up, gla_b_gate, gla_norm_w, gla_w_out, mlp_w_up, mlp_w_down):
    B, S, D = x.shape
    row = lambda w: w.reshape(1, -1).astype(F32)

    cos2, sin2 = _rope_tables(S)
    qkv, (w_out0, w_up0, w_down0) = _qkv_project(
        x, row(norm_mix_w[0]), attn_w_in[0].astype(BF16), cos2, sin2,
        riders=[(attn_w_out, 0, D), (mlp_w_up, 0, D_FF), (mlp_w_down, 0, D)], rows=RESIDENT_ROWS)
    outs, lses = [], []
    for g, (window, dil) in enumerate(DILATED_GROUPS):
        assert window // dil == ATTN_BLOCK and (S // dil) % ATTN_BLOCK == 0
        q, k, v = qkv[3 * g:3 * g + 3]
        o, lse = _group_attention(q, k, v, step_rows=STREAM_ROWS)
        outs.append(o)
        lses.append(lse)
    n_main = 2 * GLA_KEY_DIM + 2 * GLA_VAL_DIM
    x, (w_gla, w_out1, w_up1, w_down1) = _attn_tail(
        outs, lses, x, w_out0, row(norm_mlp_w[0]), w_up0, w_down0,
        riders=[(gla_w_in, 0, n_main), (gla_w_out, 0, D), (mlp_w_up, 1, D_FF), (mlp_w_down, 1, D)],
        rows=RESIDENT_ROWS)

    prepared = _gla_in_project(x, row(norm_mix_w[1]), w_gla, gla_w_in[0][:, n_main:],
                               gla_w_gate_up[0], row(gla_b_gate[0]), rows=STREAM_ROWS)
    o = _gla_recurrence(*prepared, row(gla_norm_w[0]), B, S, rows=RESIDENT_ROWS)
    return _gla_tail(o, x, w_out1, row(norm_mlp_w[1]), w_up1, w_down1, row(final_norm_w),
                     rows=RESIDENT_ROWS)
```

```python
from functools import partial

import jax
import jax.numpy as jnp
import numpy as np
from jax import lax
from jax.experimental import pallas as pl
from jax.experimental.pallas import tpu as pltpu

D_MODEL = 1024
NORM_EPS = 1e-5
DILATED_GROUPS = ((128, 1), (512, 4), (2048, 16))
ATTN_HEADS = 8
HEAD_DIM = 128
ATTN_WIDTH = ATTN_HEADS * HEAD_DIM
ATTN_BLOCK = 128
ROPE_THETA = 10000.0
GLA_HEADS = 4
GLA_DK = 128
GLA_DV = 256
GLA_KEY_DIM = GLA_HEADS * GLA_DK
GLA_VAL_DIM = GLA_HEADS * GLA_DV
GLA_GATE_RANK = 16
GLA_GATE_NORMALIZER = 16.0
GLA_CHUNK = 64
D_FF = 4 * D_MODEL

LANES = 128
VMEM_LIMIT_BYTES = 56 * 1024 * 1024
BF16_SUBLANES = 16
RESIDENT_ROWS = 512
STREAM_ROWS = 1024
NEG_BIG = -1e30

BF16 = jnp.bfloat16
F32 = jnp.float32


def _params(*semantics):
    return pltpu.CompilerParams(dimension_semantics=semantics,
                                vmem_limit_bytes=VMEM_LIMIT_BYTES)


def _rmsnorm_rows(x, w):
    var = jnp.mean(x * x, axis=-1, keepdims=True)
    return x * lax.rsqrt(var + NORM_EPS) * w


def _dot(a, b):
    return jnp.dot(a, b, preferred_element_type=F32)


def _dot_nt(a, b):
    return lax.dot_general(a, b, (((1,), (1,)), ((), ())), preferred_element_type=F32)


def _dot_tn(a, b):
    return lax.dot_general(a, b, (((0,), (0,)), ((), ())), preferred_element_type=F32)


def _cast_blocks(src_refs, dst_refs):
    for src_ref, dst_ref in zip(src_refs, dst_refs):
        dst_ref[...] = src_ref[:, :dst_ref.shape[1]].astype(BF16)


def _rider_specs(riders, n_steps):
    in_specs, out_specs, out_shapes = [], [], []
    for w, layer, n_keep in riders:
        _, k, n = w.shape
        assert k % n_steps == 0 and (k // n_steps) % BF16_SUBLANES == 0
        blk = k // n_steps
        index = lambda s, layer=layer: (layer, jnp.minimum(s, n_steps - 1), 0)
        in_specs.append(pl.BlockSpec((None, blk, n), index))
        out_specs.append(pl.BlockSpec((blk, n_keep), lambda s: (jnp.minimum(s, n_steps - 1), 0)))
        out_shapes.append(jax.ShapeDtypeStruct((k, n_keep), BF16))
    return in_specs, out_specs, out_shapes


QKV_CHUNK = 256
Q_SCALE = HEAD_DIM ** -0.5 * float(np.log2(np.e))


def _qkv_kernel(*refs, n_riders):
    n_slabs = D_MODEL // LANES
    n_out = 3 * len(DILATED_GROUPS)
    x_refs = refs[:n_slabs]
    nw_ref, w_ref, cos_ref, sin_ref = refs[n_slabs:n_slabs + 4]
    rider_in = refs[n_slabs + 4:n_slabs + 4 + n_riders]
    out_refs = refs[n_slabs + 4 + n_riders:n_slabs + 4 + n_riders + n_out]
    rider_out = refs[n_slabs + 4 + n_riders + n_out:-2]
    tab_ref, perm_ref = refs[-2:]
    rows = x_refs[0].shape[0]
    n_chunks = ATTN_WIDTH // QKV_CHUNK
    row_blocks = rows // ATTN_BLOCK

    def residue_major(src_ref, src_dil, dil):
        if dil == src_dil:
            return src_ref[...]
        pieces = [src_ref[pl.ds((r % src_dil) * (rows // src_dil) + r // src_dil, rows // dil,
                                stride=dil // src_dil), :] for r in range(dil)]
        return jnp.concatenate(pieces, axis=0)

    prev_dil = 1
    for g, (_, dil) in enumerate(DILATED_GROUPS):
        n = rows // dil
        x_src = x_refs if prev_dil == 1 else [perm_ref.at[c] for c in range(n_slabs)]
        slabs = [residue_major(src, prev_dil, dil) for src in x_src]
        if 1 < dil < DILATED_GROUPS[-1][1]:
            for c in range(n_slabs):
                perm_ref[c] = slabs[c]
        h = _rmsnorm_rows(jnp.concatenate(slabs, axis=1), nw_ref[...]).astype(BF16)
        if dil == 1:
            cos_src, sin_src = cos_ref, sin_ref
        else:
            tab_src = (cos_ref, sin_ref) if prev_dil == 1 else (tab_ref.at[g - 2, 0],
                                                                tab_ref.at[g - 2, 1])
            for t in range(2):
                tab_ref[g - 1, t] = residue_major(tab_src[t], prev_dil, dil)
            cos_src, sin_src = tab_ref.at[g - 1, 0], tab_ref.at[g - 1, 1]
        if dil > 1:
            prev_dil = dil
        for part in range(3):
            o_ref = out_refs[3 * g + part]
            for j in range(n_chunks):
                col = (3 * g + part) * ATTN_WIDTH + j * QKV_CHUNK
                acc = _dot(h, w_ref[:, col:col + QKV_CHUNK])
                for half in range(QKV_CHUNK // HEAD_DIM):
                    lo = j * QKV_CHUNK + half * HEAD_DIM
                    for rb in range(row_blocks):
                        rs = slice(rb * ATTN_BLOCK, (rb + 1) * ATTN_BLOCK)
                        t = acc[rs, half * HEAD_DIM:(half + 1) * HEAD_DIM]
                        if part < 2:
                            t = t * cos_src[rs, :] + pltpu.roll(t, HEAD_DIM // 2, 1) * sin_src[rs, :]
                        if part == 0:
                            t = t * Q_SCALE
                        t = t.astype(BF16)
                        if n >= ATTN_BLOCK:
                            first = rb * ATTN_BLOCK
                            o_ref[first // n, pl.ds(first % n, ATTN_BLOCK), lo:lo + HEAD_DIM] = t
                        else:
                            for k in range(ATTN_BLOCK // n):
                                o_ref[rb * (ATTN_BLOCK // n) + k, :, lo:lo + HEAD_DIM] = (
                                    t[k * n:(k + 1) * n])
    _cast_blocks(rider_in, rider_out)


def _qkv_project(x, nw, w_in, cos2, sin2, riders, rows):
    B, S, D = x.shape
    n_blocks = S // rows
    out_shapes, out_specs = [], []
    for _, dil in DILATED_GROUPS:
        for _ in range(3):
            out_shapes.append(jax.ShapeDtypeStruct((B, dil, S // dil, ATTN_WIDTH), BF16))
            out_specs.append(pl.BlockSpec((None, dil, rows // dil, ATTN_WIDTH),
                                          lambda i: (i // n_blocks, 0, i % n_blocks, 0)))
    rider_in, rider_out, rider_shapes = _rider_specs(riders, B * n_blocks)
    tab_spec = pl.BlockSpec((rows, HEAD_DIM), lambda i: (i % n_blocks, 0))
    const = lambda shape: pl.BlockSpec(shape, lambda i: (0, 0), pipeline_mode=pl.Buffered(1))
    xf = x.reshape(B * S, D)
    slab_specs = [pl.BlockSpec((rows, LANES), lambda i, c=c: (i, c)) for c in range(D // LANES)]
    outs = pl.pallas_call(
        partial(_qkv_kernel, n_riders=len(riders)),
        grid=(B * n_blocks,),
        in_specs=slab_specs + [const((1, D)), const((D, w_in.shape[1])), tab_spec, tab_spec]
        + rider_in,
        out_specs=out_specs + rider_out,
        out_shape=out_shapes + rider_shapes,
        scratch_shapes=[pltpu.VMEM((len(DILATED_GROUPS) - 1, 2, rows, HEAD_DIM), F32),
                        pltpu.VMEM((D // LANES, rows, LANES), F32)],
        compiler_params=_params("parallel"),
        name="attn_qkv",
    )(*([xf] * (D // LANES)), nw, w_in, cos2, sin2, *[w for w, _, _ in riders])
    return outs[:len(out_shapes)], outs[len(out_shapes):]


def _band_bias():
    qi = np.arange(ATTN_BLOCK)[:, None]
    kj = np.arange(2 * ATTN_BLOCK)[None, :]
    rel = qi - kj + ATTN_BLOCK
    band = (rel >= 0) & (rel <= ATTN_BLOCK)
    masks = np.stack([band & (kj >= ATTN_BLOCK), band])
    return jnp.asarray(np.where(masks, 0.0, NEG_BIG).astype(np.float32))


def _attn_kernel(q_ref, k_ref, v_ref, kh_ref, vh_ref, bias_ref, o_ref, lse_ref):
    first_bias = jnp.where(pl.program_id(2) > 0, 1, 0)
    n_res, n_rows, _ = q_ref.shape
    for r in range(n_res):
        for c in range(n_rows // ATTN_BLOCK):
            rows = slice(c * ATTN_BLOCK, (c + 1) * ATTN_BLOCK)
            bias = bias_ref.at[first_bias if c == 0 else 1]
            for h in range(ATTN_HEADS):
                cols = slice(h * HEAD_DIM, (h + 1) * HEAD_DIM)
                q = q_ref[r, rows, cols]
                if c == 0:
                    kk = jnp.concatenate([kh_ref[r, :, cols], k_ref[r, rows, cols]], axis=0)
                    vv = jnp.concatenate([vh_ref[r, :, cols], v_ref[r, rows, cols]], axis=0)
                else:
                    hist = slice((c - 1) * ATTN_BLOCK, (c + 1) * ATTN_BLOCK)
                    kk = k_ref[r, hist, cols]
                    vv = v_ref[r, hist, cols]
                s = _dot_nt(q, kk) + bias[...]
                m = jnp.max(s, axis=-1, keepdims=True)
                p = jnp.exp2(s - m)
                l = jnp.sum(p, axis=-1, keepdims=True)
                o = _dot(p.astype(BF16), vv) / l
                o_ref[r, rows, cols] = o.astype(BF16)
                lse_ref[r, rows, h:h + 1] = m + jnp.log2(l)
            lse_ref[r, rows, ATTN_HEADS:] = jnp.zeros((ATTN_BLOCK, LANES - ATTN_HEADS), F32)


def _group_attention(q, k, v, step_rows):
    B, dil, L, W = q.shape
    rows = min(step_rows, L)
    n_res = min(step_rows // rows, dil)
    n_sub = rows // ATTN_BLOCK
    blk = lambda width: pl.BlockSpec((None, n_res, rows, width), lambda b, r, i: (b, r, i, 0))
    halo = pl.BlockSpec((None, n_res, ATTN_BLOCK, W),
                        lambda b, r, i: (b, r, jnp.maximum(i * n_sub - 1, 0), 0))
    return pl.pallas_call(
        _attn_kernel,
        grid=(B, dil // n_res, L // rows),
        in_specs=[blk(W), blk(W), blk(W), halo, halo,
                  pl.BlockSpec((2, ATTN_BLOCK, 2 * ATTN_BLOCK), lambda b, r, i: (0, 0, 0))],
        out_specs=[blk(W), blk(LANES)],
        out_shape=[
            jax.ShapeDtypeStruct((B, dil, L, W), BF16),
            jax.ShapeDtypeStruct((B, dil, L, LANES), F32),
        ],
        compiler_params=_params("parallel", "parallel", "parallel"),
        name="attn_band",
    )(q, k, v, k, v, _band_bias())


MLP_FF_CHUNK = 1024


def _project_and_mlp(mixed, x_ref, wo_ref, nw_ref, wu_ref, wd_ref, side_work=None):
    x_mid = x_ref[...] + _dot(mixed, wo_ref[...])
    h = _rmsnorm_rows(x_mid, nw_ref[...]).astype(BF16)
    acc = x_mid
    for f in range(D_FF // MLP_FF_CHUNK):
        cols = slice(f * MLP_FF_CHUNK, (f + 1) * MLP_FF_CHUNK)
        a = jnp.maximum(_dot(h, wu_ref[:, cols]), 0.0)
        acc = acc + _dot((a * a).astype(BF16), wd_ref[cols, :])
        if side_work is not None:
            side_work(f)
    return acc


def _attn_tail_kernel(*refs, n_riders):
    n_groups = len(DILATED_GROUPS)
    o_refs = refs[:n_groups]
    lse_refs = refs[n_groups:2 * n_groups]
    rest = refs[2 * n_groups:]
    x_ref, wo_ref, nw_ref, wu_ref, wd_ref = rest[:5]
    rider_in = rest[5:5 + n_riders]
    out_ref = rest[5 + n_riders]
    rider_out = rest[6 + n_riders:6 + 2 * n_riders]
    merged_ref, nat_ref = rest[6 + 2 * n_riders:]
    rows = x_ref.shape[0]
    step = pl.program_id(0)
    write_slot = step % 2
    read_slot = 1 - write_slot

    @pl.when(step == 0)
    def _():
        merged_ref[1] = jnp.zeros(merged_ref.shape[1:], BF16)

    def natural(src_ref, col, dil, slot):
        lanes = slice(col * LANES, (col + 1) * LANES)
        if dil == 1:
            return src_ref[0, :, lanes].astype(F32)
        for r in range(dil):
            nat_ref[slot, pl.ds(r, rows // dil, stride=dil), :] = src_ref[r, :, lanes].astype(F32)
        return nat_ref[slot]

    mixed = merged_ref[read_slot]
    heads_per_chunk = ATTN_HEADS // (D_FF // MLP_FF_CHUNK)
    alphas = []

    def merge_heads(f):
        if not alphas:
            lses = [natural(lse_refs[g], 0, dil, g) for g, (_, dil) in enumerate(DILATED_GROUPS)]
            m = jnp.maximum(jnp.maximum(lses[0], lses[1]), lses[2])
            es = [jnp.exp2(l - m) for l in lses]
            inv = 1.0 / (es[0] + es[1] + es[2])
            alphas.extend(e * inv for e in es)
        for h in range(f * heads_per_chunk, (f + 1) * heads_per_chunk):
            o = None
            for g, (_, dil) in enumerate(DILATED_GROUPS):
                slot = n_groups + 2 * g + h % 2
                term = alphas[g][:, h:h + 1] * natural(o_refs[g], h, dil, slot)
                o = term if o is None else o + term
            merged_ref[write_slot, :, h * HEAD_DIM:(h + 1) * HEAD_DIM] = o.astype(BF16)

    out_ref[...] = _project_and_mlp(mixed, x_ref, wo_ref, nw_ref, wu_ref, wd_ref,
                                    side_work=merge_heads)
    _cast_blocks(rider_in, rider_out)


def _attn_tail(o_list, lse_list, x, w_out, nw, w_up, w_down, riders, rows):
    B, S, D = x.shape
    W = ATTN_WIDTH
    n_blocks = S // rows
    n_groups = len(DILATED_GROUPS)
    n_tiles = B * n_blocks

    def residue_spec(dil, width):
        def index(s):
            t = jnp.minimum(s, n_tiles - 1)
            return (t // n_blocks, 0, t % n_blocks, 0)
        return pl.BlockSpec((None, dil, rows // dil, width), index)

    o_specs = [residue_spec(dil, W) for _, dil in DILATED_GROUPS]
    lse_specs = [residue_spec(dil, LANES) for _, dil in DILATED_GROUPS]
    rider_in, rider_out, rider_shapes = _rider_specs(riders, n_tiles)
    row_spec = pl.BlockSpec((rows, D), lambda s: (jnp.maximum(s - 1, 0), 0))
    const = lambda shape: pl.BlockSpec(shape, lambda s: (0, 0), pipeline_mode=pl.Buffered(1))
    outs = pl.pallas_call(
        partial(_attn_tail_kernel, n_riders=len(riders)),
        grid=(n_tiles + 1,),
        in_specs=o_specs + lse_specs + [row_spec, const((W, D)), const((1, D)),
                                        const((D, D_FF)), const((D_FF, D))] + rider_in,
        out_specs=[row_spec] + rider_out,
        out_shape=[jax.ShapeDtypeStruct((B * S, D), F32)] + rider_shapes,
        scratch_shapes=[pltpu.VMEM((2, rows, W), BF16),
                        pltpu.VMEM((3 * n_groups, rows, LANES), F32)],
        compiler_params=_params("arbitrary"),
        name="attn_tail",
    )(*o_list, *lse_list, x.reshape(B * S, D), w_out, nw, w_up, w_down,
      *[w for w, _, _ in riders])
    return outs[0].reshape(B, S, D), outs[1:]


def _gla_tail_kernel(o_ref, x_ref, wo_ref, nw_ref, wu_ref, wd_ref, nwf_ref, out_ref):
    acc = _project_and_mlp(o_ref[...], x_ref, wo_ref, nw_ref, wu_ref, wd_ref)
    out_ref[...] = _rmsnorm_rows(acc, nwf_ref[...])


def _gla_tail(o, x, w_out, nw, w_up, w_down, nw_final, rows):
    B, S, D = x.shape
    T = B * S
    row_spec = lambda width: pl.BlockSpec((rows, width), lambda i: (i, 0))
    const = lambda shape: pl.BlockSpec(shape, lambda i: (0, 0), pipeline_mode=pl.Buffered(1))
    out = pl.pallas_call(
        _gla_tail_kernel,
        grid=(T // rows,),
        in_specs=[row_spec(o.shape[-1]), row_spec(D), const((o.shape[-1], D)), const((1, D)),
                  const((D, D_FF)), const((D_FF, D)), const((1, D))],
        out_specs=row_spec(D),
        out_shape=jax.ShapeDtypeStruct((T, D), F32),
        compiler_params=_params("parallel"),
        name="gla_tail",
    )(o, x.reshape(T, D), w_out, nw, w_up, w_down, nw_final)
    return out.reshape(B, S, D)


def _gla_pair_constants():
    pair = 2 * GLA_CHUNK
    ri = lax.broadcasted_iota(jnp.int32, (pair, pair), 0)
    cj = lax.broadcasted_iota(jnp.int32, (pair, pair), 1)
    causal = ri >= cj
    cumsum_mat = (causal & ((ri >= GLA_CHUNK) == (cj >= GLA_CHUNK))).astype(BF16)
    first = lax.broadcasted_iota(jnp.int32, (pair, GLA_KEY_DIM), 0) < GLA_CHUNK
    return pair, causal, cumsum_mat, first


def _gla_in_kernel(x_ref, nw_ref, w_ref, wg_ref, wgu_ref, bg_ref,
                   qsc_ref, ksc_ref, qst_ref, kup_ref, v_ref, g_ref, la_ref, q_ref, k_ref):
    h = _rmsnorm_rows(x_ref[...], nw_ref[...]).astype(BF16)
    gate_lr = _dot(h, wg_ref[...].astype(BF16)).astype(BF16)
    gk = _dot(gate_lr, wgu_ref[...].astype(BF16)) + bg_ref[...]
    log_sig = jnp.minimum(gk, 0.0) - jnp.log(1.0 + jnp.exp(-jnp.abs(gk)))
    la_ref[...] = log_sig * (1.0 / GLA_GATE_NORMALIZER)
    q_ref[...] = _dot(h, w_ref[:, :GLA_KEY_DIM])
    k_ref[...] = _dot(h, w_ref[:, GLA_KEY_DIM:2 * GLA_KEY_DIM])
    v_lo = 2 * GLA_KEY_DIM
    g_lo = v_lo + GLA_VAL_DIM
    pair, _, cumsum_mat, first = _gla_pair_constants()
    n_pairs = x_ref.shape[0] // pair
    wide = [(v_ref, v_lo), (g_ref, g_lo)]
    for p in range(n_pairs):
        if p % (n_pairs // 2) == 0:
            dst_ref, col = wide[p // (n_pairs // 2)]
            dst_ref[...] = _dot(h, w_ref[:, col:col + GLA_VAL_DIM]).astype(dst_ref.dtype)
        rows = slice(p * pair, (p + 1) * pair)
        la = la_ref[rows, :]
        la_hi = la.astype(BF16)
        la_lo = (la - la_hi.astype(F32)).astype(BF16)
        b = _dot(cumsum_mat, la_hi) + _dot(cumsum_mat, la_lo)
        bl0 = b[GLA_CHUNK - 1:GLA_CHUNK, :]
        bl1 = b[pair - 1:pair, :]
        ref_b = jnp.where(first, b - bl0, b)
        q_e = q_ref[rows, :] * jnp.exp(ref_b)
        k_e = k_ref[rows, :] * jnp.exp(-ref_b)
        scale = GLA_DK ** -0.5
        qsc_ref[rows, :] = (q_e * scale).astype(BF16)
        ksc_ref[rows, :] = k_e.astype(BF16)
        qst_ref[rows, :] = (q_e * (jnp.exp(bl0) * scale)).astype(BF16)
        kup_ref[rows, :] = (k_e * jnp.exp(bl1)).astype(BF16)


def _gla_in_project(x, nw, w_main, w_gate, w_gate_up, b_gate, rows):
    B, S, D = x.shape
    T = B * S
    row_spec = lambda width: pl.BlockSpec((rows, width), lambda i: (i, 0))
    const = lambda shape: pl.BlockSpec(shape, lambda i: (0, 0))
    n_main = 2 * GLA_KEY_DIM + 2 * GLA_VAL_DIM
    key = jax.ShapeDtypeStruct((T, GLA_KEY_DIM), BF16)
    return pl.pallas_call(
        _gla_in_kernel,
        grid=(T // rows,),
        in_specs=[row_spec(D), const((1, D)), const((D, n_main)), const((D, GLA_GATE_RANK)),
                  const((GLA_GATE_RANK, GLA_KEY_DIM)), const((1, GLA_KEY_DIM))],
        out_specs=[row_spec(GLA_KEY_DIM)] * 4 + [row_spec(GLA_VAL_DIM), row_spec(GLA_VAL_DIM),
                                                 row_spec(GLA_KEY_DIM)],
        out_shape=[key, key, key, key,
                   jax.ShapeDtypeStruct((T, GLA_VAL_DIM), BF16),
                   jax.ShapeDtypeStruct((T, GLA_VAL_DIM), F32),
                   jax.ShapeDtypeStruct((T, GLA_KEY_DIM), F32)],
        scratch_shapes=[pltpu.VMEM((rows, GLA_KEY_DIM), F32), pltpu.VMEM((rows, GLA_KEY_DIM), F32)],
        compiler_params=_params("parallel"),
        name="gla_in",
    )(x.reshape(T, D), nw, w_main, w_gate, w_gate_up, b_gate)


def _gla_kernel(qsc_ref, ksc_ref, qst_ref, kup_ref, v_ref, g_ref, la_ref, nw_ref, o_ref, state_ref):
    @pl.when(pl.program_id(0) == 0)
    def _():
        state_ref[...] = jnp.zeros_like(state_ref)

    n_seq, n_rows, _ = qsc_ref.shape
    pair, causal, _, _ = _gla_pair_constants()
    ones = jnp.ones((pair, LANES), BF16)
    nw = nw_ref[...]
    for p in range(n_rows // pair):
        rows = slice(p * pair, (p + 1) * pair)
        for h in range(GLA_HEADS):
            kc = slice(h * GLA_DK, (h + 1) * GLA_DK)
            vc = slice(h * GLA_DV, (h + 1) * GLA_DV)
            for s in range(n_seq):
                v = v_ref[s, rows, vc]
                a = jnp.where(causal, _dot_nt(qsc_ref[s, rows, kc], ksc_ref[s, rows, kc]), 0.0)
                state = state_ref[s, h]
                o = _dot(jnp.concatenate([a.astype(BF16), qst_ref[s, rows, kc]], axis=1),
                         jnp.concatenate([v, state.astype(BF16)], axis=0))
                la = la_ref[s, rows, kc]
                la_hi = la.astype(BF16)
                la_lo = (la - la_hi.astype(F32)).astype(BF16)
                decay = jnp.exp(_dot_tn(la_hi, ones) + _dot_tn(la_lo, ones))
                decay = jnp.concatenate([decay] * (GLA_DV // LANES), axis=1)
                state_ref[s, h] = decay * state + _dot_tn(kup_ref[s, rows, kc], v)
                var = jnp.mean(o * o, axis=-1, keepdims=True)
                o = o * lax.rsqrt(var + NORM_EPS) * nw
                half_g = 0.5 * g_ref[s, rows, vc]
                o_ref[s, rows, vc] = (o * (half_g * (1.0 + jnp.tanh(half_g)))).astype(BF16)


def _gla_recurrence(q_sc, k_sc, q_st, k_up, v, g, la, nw, B, S, rows):
    seq_spec = lambda width: pl.BlockSpec((B, rows, width), lambda i: (0, i, 0))
    by_seq = lambda t: t.reshape(B, S, t.shape[-1])
    out = pl.pallas_call(
        _gla_kernel,
        grid=(S // rows,),
        in_specs=[seq_spec(GLA_KEY_DIM)] * 4 + [seq_spec(GLA_VAL_DIM), seq_spec(GLA_VAL_DIM),
                                                seq_spec(GLA_KEY_DIM),
                                                pl.BlockSpec((1, GLA_DV), lambda i: (0, 0))],
        out_specs=seq_spec(GLA_VAL_DIM),
        out_shape=jax.ShapeDtypeStruct((B, S, GLA_VAL_DIM), BF16),
        scratch_shapes=[pltpu.VMEM((B, GLA_HEADS, GLA_DK, GLA_DV), F32)],
        compiler_params=_params("arbitrary"),
        name="gla_recurrence",
    )(by_seq(q_sc), by_seq(k_sc), by_seq(q_st), by_seq(k_up), by_seq(v), by_seq(g), by_seq(la), nw)
    return out.reshape(B * S, GLA_VAL_DIM)


def _rope_tables(seq_len):
    pos = np.arange(seq_len, dtype=np.float64)
    inv_freq = ROPE_THETA ** (-np.arange(0, HEAD_DIM, 2, dtype=np.float64) / HEAD_DIM)
    ang = pos[:, None] * inv_freq[None, :]
    cos, sin = np.cos(ang), np.sin(ang)
    cos2 = np.concatenate([cos, cos], axis=-1).astype(np.float32)
    sin2 = np.concatenate([-sin, sin], axis=-1).astype(np.float32)
    return jnp.asarray(cos2), jnp.asarray(sin2)


def kernel(x, norm_mix_w, norm_mlp_w, final_norm_w, attn_w_in, attn_w_out, gla_w_in,
           gla_w_gate_up, gla_b_gate, gla_norm_w, gla_w_out, mlp_w_up, mlp_w_down):
    B, S, D = x.shape
    row = lambda w: w.reshape(1, -1).astype(F32)

    cos2, sin2 = _rope_tables(S)
    qkv, (w_out0, w_up0, w_down0) = _qkv_project(
        x, row(norm_mix_w[0]), attn_w_in[0].astype(BF16), cos2, sin2,
        riders=[(attn_w_out, 0, D), (mlp_w_up, 0, D_FF), (mlp_w_down, 0, D)], rows=RESIDENT_ROWS)
    outs, lses = [], []
    for g, (window, dil) in enumerate(DILATED_GROUPS):
        assert window // dil == ATTN_BLOCK and (S // dil) % ATTN_BLOCK == 0
        q, k, v = qkv[3 * g:3 * g + 3]
        o, lse = _group_attention(q, k, v, step_rows=STREAM_ROWS)
        outs.append(o)
        lses.append(lse)
    n_main = 2 * GLA_KEY_DIM + 2 * GLA_VAL_DIM
    x, (w_gla, w_out1, w_up1, w_down1) = _attn_tail(
        outs, lses, x, w_out0, row(norm_mlp_w[0]), w_up0, w_down0,
        riders=[(gla_w_in, 0, n_main), (gla_w_out, 0, D), (mlp_w_up, 1, D_FF), (mlp_w_down, 1, D)],
        rows=RESIDENT_ROWS)

    prepared = _gla_in_project(x, row(norm_mix_w[1]), w_gla, gla_w_in[0][:, n_main:],
                               gla_w_gate_up[0], row(gla_b_gate[0]), rows=STREAM_ROWS)
    o = _gla_recurrence(*prepared, row(gla_norm_w[0]), B, S, rows=RESIDENT_ROWS)
    return _gla_tail(o, x, w_out1, row(norm_mlp_w[1]), w_up1, w_down1, row(final_norm_w),
                     rows=RESIDENT_ROWS)
```
